```python
import math
import jax, jax.numpy as jnp
from jax import lax
import numpy as np

D_MODEL = 1024
BATCH = 8
SEQ = 2048
DEPTH = 4
DEC_BATCH = 128
DEC_SEQ = 8
PAST_LEN = 8192
PAGE_SIZE = 128

N_MIXERS = 4
EPS = 1e-6
Q_BLOCK = 128

MLA_HEADS = 8
Q_RANK = 384
KV_RANK = 256
NOPE_DIM = 128
ROPE_DIM = 64
V_DIM = 128
ROPE_THETA = 10000.0
MLA_ROW = KV_RANK + ROPE_DIM
MLA_SCALE = 1.0 / math.sqrt(NOPE_DIM + ROPE_DIM)

CONV_W = 3

WINDOWS = (128, 512, 2048)
DILATIONS = (1, 4, 16)
N_GROUPS_C = 3
HEADS_PER_GROUP = 4
N_HEADS_C = N_GROUPS_C * HEADS_PER_GROUP
HEAD_DIM_C = 128
SCALE_C = 1.0 / math.sqrt(HEAD_DIM_C)
N_BUCKETS = 32
MAX_DISTANCE = 2048

CHUNK = 128
SGU_WIDTH = D_MODEL
SGU_GROUPS = 8
SGU_GROUP_DIM = SGU_WIDTH // SGU_GROUPS

D_FF = 4 * D_MODEL

kernel_name = "hybrid_mla_conv_dilated_sgu_step"


def rmsnorm(x, g):
    xf = x.astype(jnp.float32)
    y = xf * lax.rsqrt(jnp.mean(xf * xf, axis=-1, keepdims=True) + EPS)
    return (y * g.astype(jnp.float32)).astype(x.dtype)


def rope(x, pos):
    half = ROPE_DIM // 2
    inv = ROPE_THETA ** (-jnp.arange(half, dtype=jnp.float32) / half)
    ang = pos.astype(jnp.float32)[:, None] * inv[None, :]
    cos = jnp.cos(ang)[None, :, None, :]
    sin = jnp.sin(ang)[None, :, None, :]
    xf = x.astype(jnp.float32)
    x1, x2 = xf[..., :half], xf[..., half:]
    return jnp.concatenate([x1 * cos - x2 * sin, x2 * cos + x1 * sin], axis=-1).astype(x.dtype)


def mla_project(h, pos, w_dq, g_q, w_uq, w_dkv, g_kv, w_uk):
    b, s, _ = h.shape
    cq = rmsnorm(h @ w_dq, g_q)
    q = (cq @ w_uq).reshape(b, s, MLA_HEADS, NOPE_DIM + ROPE_DIM)
    q_pe = rope(q[..., NOPE_DIM:], pos)
    q_lat = jnp.einsum("bshn,rhn->bshr", q[..., :NOPE_DIM], w_uk)
    ckv = h @ w_dkv
    c = rmsnorm(ckv[..., :KV_RANK], g_kv)
    k_pe = rope(ckv[..., None, KV_RANK:], pos)[..., 0, :]
    q_cat = jnp.concatenate([q_lat, q_pe], axis=-1)
    rows = jnp.concatenate([c, k_pe], axis=-1)
    return q_cat, rows


def mla_out(o_lat, w_uv, w_o_mla):
    b, s = o_lat.shape[:2]
    v = jnp.einsum("bshr,rhv->bshv", o_lat, w_uv)
    return v.reshape(b, s, MLA_HEADS * V_DIM) @ w_o_mla


def mla_prompt(h, w_dq, g_q, w_uq, w_dkv, g_kv, w_uk, w_uv, w_o_mla):
    b, s, _ = h.shape
    pos = jnp.arange(s, dtype=jnp.int32)
    q_cat, rows = mla_project(h, pos, w_dq, g_q, w_uq, w_dkv, g_kv, w_uk)
    lat = rows[..., :KV_RANK]

    def block(i):
        qb = lax.dynamic_slice_in_dim(q_cat, i * Q_BLOCK, Q_BLOCK, axis=1)
        logits = jnp.einsum("bqhc,bkc->bhqk", qb, rows, preferred_element_type=jnp.float32) * MLA_SCALE
        qpos = i * Q_BLOCK + jnp.arange(Q_BLOCK)
        logits = jnp.where(pos[None, :] <= qpos[:, None], logits, -jnp.inf)
        p = jax.nn.softmax(logits, axis=-1).astype(rows.dtype)
        return jnp.einsum("bhqk,bkr->bqhr", p, lat)

    o = lax.map(block, jnp.arange(s // Q_BLOCK))
    o = jnp.moveaxis(o, 0, 1).reshape(b, s, MLA_HEADS, KV_RANK)
    return mla_out(o, w_uv, w_o_mla), rows


def mla_sample(h, cache_mla, page_table, w_dq, g_q, w_uq, w_dkv, g_kv, w_uk, w_uv, w_o_mla):
    b, s, _ = h.shape
    past_len = page_table.shape[1] * cache_mla.shape[1]
    pos = past_len + jnp.arange(s, dtype=jnp.int32)
    q_cat, rows = mla_project(h, pos, w_dq, g_q, w_uq, w_dkv, g_kv, w_uk)
    past = cache_mla[page_table].reshape(b, past_len, MLA_ROW)
    lp = jnp.einsum("bqhc,bkc->bhqk", q_cat, past, preferred_element_type=jnp.float32) * MLA_SCALE
    ln = jnp.einsum("bqhc,bkc->bhqk", q_cat, rows, preferred_element_type=jnp.float32) * MLA_SCALE
    ln = jnp.where(jnp.tril(jnp.ones((s, s), dtype=bool)), ln, -jnp.inf)
    p = jax.nn.softmax(jnp.concatenate([lp, ln], axis=-1), axis=-1).astype(rows.dtype)
    o = (jnp.einsum("bhqk,bkr->bqhr", p[..., :past_len], past[..., :KV_RANK])
         + jnp.einsum("bhqk,bkr->bqhr", p[..., past_len:], rows[..., :KV_RANK]))
    return mla_out(o, w_uv, w_o_mla), rows


def short_conv(h, z_past, w_in_conv, w_conv, w_out_conv):
    b, s, d = h.shape
    bch = h @ w_in_conv
    gate_b, gate_c, hx = bch[..., :d], bch[..., d:2 * d], bch[..., 2 * d:]
    z = jnp.concatenate([z_past, gate_c * hx], axis=1)
    y = w_conv[0] * z[:, 0:s]
    for j in range(1, CONV_W):
        y = y + w_conv[j] * z[:, j:j + s]
    return (gate_b * y) @ w_out_conv, z[:, -(CONV_W - 1):]


def t5_bucket(dist):
    max_exact = N_BUCKETS // 2
    n = jnp.maximum(dist, 1).astype(jnp.float32)
    large = max_exact + (jnp.log(n / max_exact) / math.log(MAX_DISTANCE / max_exact)
                         * (N_BUCKETS - max_exact)).astype(jnp.int32)
    return jnp.where(dist < max_exact, dist, jnp.minimum(large, N_BUCKETS - 1))


def dilated_project(h, w_qkv_c):
    b, s, _ = h.shape
    qkv = (h @ w_qkv_c).reshape(b, s, 3, N_GROUPS_C, HEADS_PER_GROUP, HEAD_DIM_C)
    return qkv[:, :, 0], qkv[:, :, 1:]


def dilated_group_attn(q, kv, q_idx, dil, n_keys, bias):
    idx = q_idx[:, None] - dil * jnp.arange(n_keys)[None, :]
    valid = idx >= 0
    kvg = jnp.take(kv, jnp.maximum(idx, 0), axis=1)
    logits = (jnp.einsum("bqhd,bqkhd->bhqk", q, kvg[:, :, :, 0], preferred_element_type=jnp.float32)
              * SCALE_C + bias[None, :, None, :].astype(jnp.float32))
    logits = jnp.where(valid[None, None], logits, -jnp.inf)
    lse = jax.nn.logsumexp(logits, axis=-1, keepdims=True)
    p = jnp.exp(logits - lse).astype(kv.dtype)
    out = jnp.einsum("bhqk,bqkhd->bqhd", p, kvg[:, :, :, 1])
    return out, lse[..., 0]


def dilated_mix(q, kvs, q_idxs, rel_bias):
    outs, lses = [], []
    for g in range(N_GROUPS_C):
        n_keys = WINDOWS[g] // DILATIONS[g] + 1
        dist = DILATIONS[g] * jnp.arange(n_keys, dtype=jnp.int32)
        bias = rel_bias[t5_bucket(dist)][:, g * HEADS_PER_GROUP:(g + 1) * HEADS_PER_GROUP].T
        o, l = dilated_group_attn(q[:, :, g], kvs[g], q_idxs[g], DILATIONS[g], n_keys, bias)
        outs.append(o)
        lses.append(l)
    w = jax.nn.softmax(jnp.stack(lses), axis=0)
    w = jnp.transpose(w, (0, 1, 3, 2))[..., None].astype(outs[0].dtype)
    return jnp.sum(jnp.stack(outs) * w, axis=0)


def dilated_prompt(h, w_qkv_c, w_o_c, rel_bias):
    b, s, _ = h.shape
    q, kv = dilated_project(h, w_qkv_c)
    kvs = [kv[:, :, :, g] for g in range(N_GROUPS_C)]

    def block(i):
        qb = lax.dynamic_slice_in_dim(q, i * Q_BLOCK, Q_BLOCK, axis=1)
        q_idx = i * Q_BLOCK + jnp.arange(Q_BLOCK, dtype=jnp.int32)
        return dilated_mix(qb, kvs, [q_idx] * N_GROUPS_C, rel_bias)

    o = lax.map(block, jnp.arange(s // Q_BLOCK))
    o = jnp.moveaxis(o, 0, 1).reshape(b, s, HEADS_PER_GROUP * HEAD_DIM_C)
    new_rows = [kvs[g][:, -min(WINDOWS[g], s):] for g in range(N_GROUPS_C)]
    return o @ w_o_c, new_rows


def dilated_sample(h, bufs, w_qkv_c, w_o_c, rel_bias):
    b, s, _ = h.shape
    q, kv = dilated_project(h, w_qkv_c)
    new_rows = [kv[:, :, :, g] for g in range(N_GROUPS_C)]
    exts = [jnp.concatenate([bufs[g], new_rows[g]], axis=1) for g in range(N_GROUPS_C)]
    q_idxs = [bufs[g].shape[1] + jnp.arange(s, dtype=jnp.int32) for g in range(N_GROUPS_C)]
    o = dilated_mix(q, exts, q_idxs, rel_bias).reshape(b, s, HEADS_PER_GROUP * HEAD_DIM_C)
    return o @ w_o_c, new_rows


def chunk_sgu(h, chunk_len, w_uv_d, g_sgu, w_s, b_s, w_out_d):
    b, s, _ = h.shape
    uv = jax.nn.gelu(h @ w_uv_d)
    u, v = uv[..., :SGU_WIDTH], rmsnorm(uv[..., SGU_WIDTH:], g_sgu)
    vc = v.reshape(b, s // chunk_len, chunk_len, SGU_GROUPS, SGU_GROUP_DIM)
    ws = w_s[:, :chunk_len, :chunk_len] * jnp.tril(jnp.ones((chunk_len, chunk_len), w_s.dtype))
    mixed = jnp.einsum("gij,bnjgc->bnigc", ws, vc) + b_s[:, :chunk_len].T[None, None, :, :, None]
    return (u * mixed.reshape(b, s, SGU_WIDTH)) @ w_out_d, v


def sqrelu_mlp(h, w1, w2):
    return jnp.square(jax.nn.relu(h @ w1)) @ w2


def setup_inputs(seed: int = 0) -> dict:
    key = jax.random.key(seed)
    ks = jax.random.split(key, 40)
    it = iter(range(40))

    def nrm(shape, scale):
        return jax.random.normal(ks[next(it)], shape, jnp.float32) * scale

    def gain(shape):
        return 1.0 + 0.02 * jax.random.normal(ks[next(it)], shape, jnp.float32)

    n_pages = PAST_LEN // PAGE_SIZE
    n_used = DEC_BATCH * n_pages
    n_pool = n_used + n_used // 4
    page_table = jax.random.permutation(ks[next(it)], n_pool)[:n_used].reshape(DEC_BATCH, n_pages).astype(jnp.int32)
    win_shape = lambda w: (DEC_BATCH, min(w, PAST_LEN), 2, HEADS_PER_GROUP, HEAD_DIM_C)
    return {
        "x_prompt": nrm((BATCH, SEQ, D_MODEL), 1.0),
        "x_sample": nrm((DEC_BATCH, DEC_SEQ, D_MODEL), 1.0),
        "cache_mla": nrm((n_pool, PAGE_SIZE, MLA_ROW), 1.0),
        "page_table": page_table,
        "state_conv": nrm((DEC_BATCH, CONV_W - 1, D_MODEL), 1.0),
        "state_win1": nrm(win_shape(WINDOWS[0]), 1.0),
        "state_win2": nrm(win_shape(WINDOWS[1]), 1.0),
        "state_win3": nrm(win_shape(WINDOWS[2]), 1.0),
        "norm_mix": gain((DEPTH, D_MODEL)),
        "norm_ffn": gain((DEPTH, D_MODEL)),
        "norm_final": gain((D_MODEL,)),
        "w_dq": nrm((D_MODEL, Q_RANK), D_MODEL ** -0.5),
        "g_q": gain((Q_RANK,)),
        "w_uq": nrm((Q_RANK, MLA_HEADS * (NOPE_DIM + ROPE_DIM)), Q_RANK ** -0.5),
        "w_dkv": nrm((D_MODEL, KV_RANK + ROPE_DIM), D_MODEL ** -0.5),
        "g_kv": gain((KV_RANK,)),
        "w_uk": nrm((KV_RANK, MLA_HEADS, NOPE_DIM), KV_RANK ** -0.5),
        "w_uv": nrm((KV_RANK, MLA_HEADS, V_DIM), KV_RANK ** -0.5),
        "w_o_mla": nrm((MLA_HEADS * V_DIM, D_MODEL), (MLA_HEADS * V_DIM) ** -0.5),
        "w_in_conv": nrm((D_MODEL, 3 * D_MODEL), D_MODEL ** -0.5),
        "w_conv": nrm((CONV_W, D_MODEL), CONV_W ** -0.5),
        "w_out_conv": nrm((D_MODEL, D_MODEL), D_MODEL ** -0.5),
        "w_qkv_c": nrm((D_MODEL, 3 * N_HEADS_C * HEAD_DIM_C), D_MODEL ** -0.5),
        "w_o_c": nrm((HEADS_PER_GROUP * HEAD_DIM_C, D_MODEL), (HEADS_PER_GROUP * HEAD_DIM_C) ** -0.5),
        "rel_bias": nrm((N_BUCKETS, N_HEADS_C), 0.1),
        "w_uv_d": nrm((D_MODEL, 2 * SGU_WIDTH), D_MODEL ** -0.5),
        "g_sgu": gain((SGU_WIDTH,)),
        "w_s": nrm((SGU_GROUPS, CHUNK, CHUNK), CHUNK ** -0.5),
        "b_s": nrm((SGU_GROUPS, CHUNK), 0.02),
        "w_out_d": nrm((SGU_WIDTH, D_MODEL), SGU_WIDTH ** -0.5),
        "w_ffn1": nrm((DEPTH, D_MODEL, D_FF), D_MODEL ** -0.5),
        "w_ffn2": nrm((DEPTH, D_FF, D_MODEL), D_FF ** -0.5),
    }


def reference(x_prompt, x_sample, cache_mla, page_table, state_conv, state_win1, state_win2, state_win3,
              norm_mix, norm_ffn, norm_final,
              w_dq, g_q, w_uq, w_dkv, g_kv, w_uk, w_uv, w_o_mla,
              w_in_conv, w_conv, w_out_conv,
              w_qkv_c, w_o_c, rel_bias,
              w_uv_d, g_sgu, w_s, b_s, w_out_d,
              w_ffn1, w_ffn2):
    xp, xs = x_prompt, x_sample
    for i in range(DEPTH):
        hp = rmsnorm(xp, norm_mix[i])
        hs = rmsnorm(xs, norm_mix[i])
        kind = i % N_MIXERS
        if kind == 0:
            mp, mla_rows_p = mla_prompt(hp, w_dq, g_q, w_uq, w_dkv, g_kv, w_uk, w_uv, w_o_mla)
            ms, mla_rows_s = mla_sample(hs, cache_mla, page_table, w_dq, g_q, w_uq, w_dkv, g_kv,
                                        w_uk, w_uv, w_o_mla)
        elif kind == 1:
            zeros_p = jnp.zeros((hp.shape[0], CONV_W - 1, D_MODEL), hp.dtype)
            mp, conv_p = short_conv(hp, zeros_p, w_in_conv, w_conv, w_out_conv)
            ms, conv_s = short_conv(hs, state_conv, w_in_conv, w_conv, w_out_conv)
        elif kind == 2:
            mp, win_p = dilated_prompt(hp, w_qkv_c, w_o_c, rel_bias)
            ms, win_s = dilated_sample(hs, (state_win1, state_win2, state_win3), w_qkv_c, w_o_c, rel_bias)
        else:
            mp, _ = chunk_sgu(hp, CHUNK, w_uv_d, g_sgu, w_s, b_s, w_out_d)
            ms, sgu_v_s = chunk_sgu(hs, hs.shape[1], w_uv_d, g_sgu, w_s, b_s, w_out_d)
        xp = xp + mp
        xs = xs + ms
        xp = xp + sqrelu_mlp(rmsnorm(xp, norm_ffn[i]), w_ffn1[i], w_ffn2[i])
        xs = xs + sqrelu_mlp(rmsnorm(xs, norm_ffn[i]), w_ffn1[i], w_ffn2[i])
    y_prompt = rmsnorm(xp, norm_final)
    y_sample = rmsnorm(xs, norm_final)
    win1_p, win2_p, win3_p = win_p
    win1_s, win2_s, win3_s = win_s
    return (y_prompt, y_sample, mla_rows_p, mla_rows_s, conv_p, conv_s,
            win1_p, win1_s, win2_p, win2_s, win3_p, win3_s, sgu_v_s)
```

```python
import functools
import math

import jax
import jax.numpy as jnp
from jax import lax
from jax.experimental import pallas as pl
from jax.experimental.pallas import tpu as pltpu

EPS = 1e-6
ROPE_THETA = 10000.0
NEG = -1e30
BF = jnp.bfloat16
F32 = jnp.float32
LANES = 128
VMEM_LIMIT = 52 * 1024 * 1024

MLA_HEADS = 8
NOPE_DIM = 128
ROPE_DIM = 64
KV_RANK = 256
V_DIM = 128
WINDOWS = (128, 512, 2048)
DILATIONS = (1, 4, 16)
HPG = 4
HD = 128
N_BUCKETS = 32
MAX_DISTANCE = 2048
CHUNK = 128
SGU_GROUPS = 8
WIN_TILE = 128


def _cparams(*sem):
    return pltpu.CompilerParams(dimension_semantics=sem, vmem_limit_bytes=VMEM_LIMIT)


def _rms(x, g):
    return x * lax.rsqrt(jnp.mean(x * x, axis=-1, keepdims=True) + EPS) * g


def _dot(a, b):
    return jnp.dot(a, b, preferred_element_type=F32)


def _dot_nt(a, b):
    return lax.dot_general(a, b, (((1,), (1,)), ((), ())), preferred_element_type=F32)


def _dot_tn(a, b):
    return lax.dot_general(a, b, (((0,), (0,)), ((), ())), preferred_element_type=F32)


def _const_spec(shape):
    nd = len(shape)
    return pl.BlockSpec(shape, lambda *_: (0,) * nd)


def _ffn_kernel(x_ref, g_ref, w1_ref, w2_ref, gf_ref, o_ref, h_sc, acc_sc, *, nf, final_norm):
    f = pl.program_id(1)

    @pl.when(f == 0)
    def _():
        x = x_ref[...]
        h_sc[...] = _rms(x, g_ref[...]).astype(BF)
        acc_sc[...] = x

    a = _dot(h_sc[...], w1_ref[...])
    a = jnp.maximum(a, 0.0)
    acc_sc[...] += _dot((a * a).astype(BF), w2_ref[...])

    @pl.when(f == nf - 1)
    def _():
        y = acc_sc[...]
        if final_norm:
            y = _rms(y, gf_ref[...])
        o_ref[...] = y


def _ffn(x, g, w1, w2, gf, *, final_norm, tm=1024, tf=1024):
    t, d = x.shape
    dff = w1.shape[1]
    tm = min(tm, t)
    nf = dff // tf
    return pl.pallas_call(
        functools.partial(_ffn_kernel, nf=nf, final_norm=final_norm),
        grid=(t // tm, nf),
        in_specs=[
            pl.BlockSpec((tm, d), lambda i, f: (i, 0)),
            pl.BlockSpec((1, d), lambda i, f: (0, 0)),
            pl.BlockSpec((d, tf), lambda i, f: (0, f)),
            pl.BlockSpec((tf, d), lambda i, f: (f, 0)),
            pl.BlockSpec((1, d), lambda i, f: (0, 0)),
        ],
        out_specs=pl.BlockSpec((tm, d), lambda i, f: (i, 0)),
        out_shape=jax.ShapeDtypeStruct((t, d), F32),
        scratch_shapes=[pltpu.VMEM((tm, d), BF), pltpu.VMEM((tm, d), F32)],
        compiler_params=_cparams("parallel", "arbitrary"),
        name="ffn",
    )(x, g, w1, w2, gf)


def _proj_res_kernel(x_ref, a_ref, w_ref, o_ref):
    o_ref[...] = x_ref[...] + _dot(a_ref[...].astype(BF), w_ref[...])


def _proj_res(x, a, w, *, tm=512):
    t, d = x.shape
    k = a.shape[1]
    tm = min(tm, t)
    return pl.pallas_call(
        _proj_res_kernel,
        grid=(t // tm,),
        in_specs=[pl.BlockSpec((tm, d), lambda i: (i, 0)),
                  pl.BlockSpec((tm, k), lambda i: (i, 0)),
                  _const_spec(w.shape)],
        out_specs=pl.BlockSpec((tm, d), lambda i: (i, 0)),
        out_shape=jax.ShapeDtypeStruct((t, d), F32),
        compiler_params=_cparams("parallel"),
        name="proj_res",
    )(x, a, w)


def _mla_proj_kernel(x_ref, cos_ref, sin_ref, gm_ref, wdq_ref, gq_ref, wqn_ref, wqr_ref, wqt_ref,
                     wuk_ref, wkv_ref, wkt_ref, gkv_ref,
                     ql_ref, qp_ref, rows_ref, kc_ref, kp_ref, *, scale):
    h = _rms(x_ref[...], gm_ref[...]).astype(BF)
    cq = _rms(_dot(h, wdq_ref[...]), gq_ref[...]).astype(BF)
    cos = cos_ref[...]
    sin = sin_ref[...]
    q_nope = _dot(cq, wqn_ref[...])
    q_rope = _dot(cq, wqr_ref[...])
    q_rot = _dot(cq, wqt_ref[...])
    for hh in range(MLA_HEADS):
        sl = slice(hh * LANES, (hh + 1) * LANES)
        ql = _dot(q_nope[:, sl].astype(BF), wuk_ref[hh])
        ql_ref[hh] = (ql * scale).astype(ql_ref.dtype)
        qp = q_rope[:, sl] * cos + q_rot[:, sl] * sin
        qp_ref[hh] = (qp * scale).astype(qp_ref.dtype)
    ckv = _dot(h, wkv_ref[...])
    k_rot = _dot(h, wkt_ref[...])
    c = _rms(ckv[:, :KV_RANK], gkv_ref[...])
    kpe = ckv[:, KV_RANK:] * cos + k_rot * sin
    rows_ref[:, :KV_RANK] = c
    rows_ref[:, KV_RANK:] = kpe[:, :ROPE_DIM]
    kc_ref[...] = c.astype(BF)
    kp_ref[...] = kpe.astype(BF)


def _mla_proj(x, cos, sin, w, *, pos_blocks, q_dtype, tm=512):
    t, d = x.shape
    tm = min(tm, t)
    scale = 1.0 / math.sqrt(NOPE_DIM + ROPE_DIM)
    row = lambda i: (i, 0)
    consts = [w["g_mix0"], w["w_dq"], w["g_q"], w["w_uq_nope"], w["w_uq_rope"], w["w_uq_rot"],
              w["w_uk_t"], w["w_dkv"], w["w_dkv_rot"], w["g_kv"]]
    return pl.pallas_call(
        functools.partial(_mla_proj_kernel, scale=scale),
        grid=(t // tm,),
        in_specs=[pl.BlockSpec((tm, d), row),
                  pl.BlockSpec((tm, LANES), lambda i: (i % pos_blocks, 0)),
                  pl.BlockSpec((tm, LANES), lambda i: (i % pos_blocks, 0))]
                 + [_const_spec(c.shape) for c in consts],
        out_specs=[pl.BlockSpec((MLA_HEADS, tm, KV_RANK), lambda i: (0, i, 0)),
                   pl.BlockSpec((MLA_HEADS, tm, LANES), lambda i: (0, i, 0)),
                   pl.BlockSpec((tm, KV_RANK + ROPE_DIM), row),
                   pl.BlockSpec((tm, KV_RANK), row),
                   pl.BlockSpec((tm, LANES), row)],
        out_shape=[jax.ShapeDtypeStruct((MLA_HEADS, t, KV_RANK), q_dtype),
                   jax.ShapeDtypeStruct((MLA_HEADS, t, LANES), q_dtype),
                   jax.ShapeDtypeStruct((t, KV_RANK + ROPE_DIM), F32),
                   jax.ShapeDtypeStruct((t, KV_RANK), BF),
                   jax.ShapeDtypeStruct((t, LANES), BF)],
        compiler_params=_cparams("parallel"),
        name="mla_proj",
    )(x, cos, sin, *consts)


def _softmax_update(s, v, m_sc, l_sc, acc_sc):
    m_old = m_sc[...]
    m_new = jnp.maximum(m_old, jnp.max(s, axis=-1, keepdims=True))
    alpha = jnp.exp(m_old - m_new)
    p = jnp.exp(s - m_new)
    l_sc[...] = alpha * l_sc[...] + jnp.sum(p, axis=-1, keepdims=True)
    acc_sc[...] = alpha * acc_sc[...] + _dot(p.astype(BF), v)
    m_sc[...] = m_new


def _mla_attn_p_kernel(ql_ref, qp_ref, kc_ref, kp_ref, o_ref, m_sc, l_sc, acc_sc, *, tq, tk):
    i = pl.program_id(1)
    rows = MLA_HEADS * tq
    ql = ql_ref[...].reshape(rows, KV_RANK)
    qp = qp_ref[...].reshape(rows, LANES)
    m_sc[...] = jnp.full(m_sc.shape, NEG, F32)
    l_sc[...] = jnp.zeros(l_sc.shape, F32)
    acc_sc[...] = jnp.zeros(acc_sc.shape, F32)
    n_full = (i * tq) // tk

    def tile(j, masked):
        start = pl.multiple_of(j * tk, tk)
        kc = kc_ref[pl.ds(start, tk), :]
        kp = kp_ref[pl.ds(start, tk), :]
        s = _dot_nt(ql, kc) + _dot_nt(qp, kp)
        if masked:
            q_pos = i * tq + lax.broadcasted_iota(jnp.int32, s.shape, 0) % tq
            k_pos = j * tk + lax.broadcasted_iota(jnp.int32, s.shape, 1)
            s = jnp.where(k_pos <= q_pos, s, NEG)
        _softmax_update(s, kc, m_sc, l_sc, acc_sc)

    def body(j, carry):
        tile(j, False)
        return carry

    lax.fori_loop(0, n_full, body, 0)
    tile(n_full, True)
    o = acc_sc[...] / l_sc[...]
    o_ref[...] = o.reshape(MLA_HEADS, tq, KV_RANK).astype(o_ref.dtype)


def _mla_attn_prompt(ql, qp, kc, kp, *, batch, seq, tq=128, tk=256):
    t = batch * seq
    nq = seq // tq
    rows = MLA_HEADS * tq
    qmap = lambda b, i: (0, b * nq + i, 0)
    return pl.pallas_call(
        functools.partial(_mla_attn_p_kernel, tq=tq, tk=tk),
        grid=(batch, nq),
        in_specs=[pl.BlockSpec((MLA_HEADS, tq, KV_RANK), qmap),
                  pl.BlockSpec((MLA_HEADS, tq, LANES), qmap),
                  pl.BlockSpec((seq, KV_RANK), lambda b, i: (b, 0)),
                  pl.BlockSpec((seq, LANES), lambda b, i: (b, 0))],
        out_specs=pl.BlockSpec((MLA_HEADS, tq, KV_RANK), qmap),
        out_shape=jax.ShapeDtypeStruct((MLA_HEADS, t, KV_RANK), BF),
        scratch_shapes=[pltpu.VMEM((rows, 1), F32), pltpu.VMEM((rows, 1), F32),
                        pltpu.VMEM((rows, KV_RANK), F32)],
        compiler_params=_cparams("parallel", "arbitrary"),
        name="mla_attn_prompt",
    )(ql, qp, kc, kp)


def _mla_attn_s_kernel(pt_ref, ql_ref, qp_ref, new_ref, *rest, pages, n_steps, page_size, s_new):
    del pt_ref
    cache_refs = rest[:pages]
    o_ref = rest[pages]
    m_sc, l_sc, acc_sc, kc_sc, kp_sc = rest[pages + 1:]
    step = pl.program_id(1)
    rows = MLA_HEADS * s_new
    ql = ql_ref[...].reshape(rows, KV_RANK).astype(BF)
    qp = qp_ref[...].reshape(rows, LANES).astype(BF)

    @pl.when(step == 0)
    def _():
        m_sc[...] = jnp.full(m_sc.shape, NEG, F32)
        l_sc[...] = jnp.zeros(l_sc.shape, F32)
        acc_sc[...] = jnp.zeros(acc_sc.shape, F32)

    zpad = jnp.zeros((page_size, LANES - ROPE_DIM), F32)
    for k in range(pages):
        kb = cache_refs[k][...]
        sl = slice(k * page_size, (k + 1) * page_size)
        kc_sc[sl, :] = kb[:, :KV_RANK].astype(BF)
        kp_sc[sl, :] = jnp.concatenate([kb[:, KV_RANK:], zpad], axis=1).astype(BF)
    kc = kc_sc[...]
    s = _dot_nt(ql, kc) + _dot_nt(qp, kp_sc[...])
    _softmax_update(s, kc, m_sc, l_sc, acc_sc)

    @pl.when(step == n_steps - 1)
    def _():
        nr = new_ref[...]
        pad_r = jnp.zeros((LANES - s_new, KV_RANK + ROPE_DIM), F32)
        nb = jnp.concatenate([nr, pad_r], axis=0)
        nc = nb[:, :KV_RANK].astype(BF)
        npe = jnp.concatenate([nb[:, KV_RANK:], jnp.zeros((LANES, LANES - ROPE_DIM), F32)],
                              axis=1).astype(BF)
        sn = _dot_nt(ql, nc) + _dot_nt(qp, npe)
        q_pos = lax.broadcasted_iota(jnp.int32, sn.shape, 0) % s_new
        k_pos = lax.broadcasted_iota(jnp.int32, sn.shape, 1)
        sn = jnp.where(k_pos <= q_pos, sn, NEG)
        _softmax_update(sn, nc, m_sc, l_sc, acc_sc)
        o = acc_sc[...] / l_sc[...]
        o_ref[...] = o.reshape(MLA_HEADS, s_new, KV_RANK)


def _mla_attn_sample(ql, qp, rows_new, cache, page_table, *, pages=16):
    _, bs, s_new, _ = ql.shape
    n_pages = page_table.shape[1]
    page_size = cache.shape[1]
    n_steps = n_pages // pages
    rows = MLA_HEADS * s_new
    pt = page_table.reshape(-1)
    qmap = lambda b, s, pt: (0, b, 0, 0)

    def cache_spec(k):
        return pl.BlockSpec((None, page_size, cache.shape[2]),
                            lambda b, s, pt: (pt[b * n_pages + s * pages + k], 0, 0))

    grid_spec = pltpu.PrefetchScalarGridSpec(
        num_scalar_prefetch=1,
        grid=(bs, n_steps),
        in_specs=[pl.BlockSpec((MLA_HEADS, None, s_new, KV_RANK), qmap),
                  pl.BlockSpec((MLA_HEADS, None, s_new, LANES), qmap),
                  pl.BlockSpec((None, s_new, cache.shape[2]), lambda b, s, pt: (b, 0, 0))]
                 + [cache_spec(k) for k in range(pages)],
        out_specs=pl.BlockSpec((MLA_HEADS, None, s_new, KV_RANK), qmap),
        scratch_shapes=[pltpu.VMEM((rows, 1), F32), pltpu.VMEM((rows, 1), F32),
                        pltpu.VMEM((rows, KV_RANK), F32),
                        pltpu.VMEM((pages * page_size, KV_RANK), BF),
                        pltpu.VMEM((pages * page_size, LANES), BF)],
    )
    return pl.pallas_call(
        functools.partial(_mla_attn_s_kernel, pages=pages, n_steps=n_steps,
                          page_size=page_size, s_new=s_new),
        grid_spec=grid_spec,
        out_shape=jax.ShapeDtypeStruct((MLA_HEADS, bs, s_new, KV_RANK), F32),
        compiler_params=_cparams("parallel", "arbitrary"),
        name="mla_attn_sample",
    )(pt, ql, qp, rows_new, *([cache] * pages))


def _mla_out_kernel(x_ref, o_ref, wuv_ref, wo_ref, y_ref):
    vs = [_dot(o_ref[hh].astype(BF), wuv_ref[hh]).astype(BF) for hh in range(MLA_HEADS)]
    v = jnp.concatenate(vs, axis=1)
    y_ref[...] = x_ref[...] + _dot(v, wo_ref[...])


def _mla_out(x, o, w_uv_h, w_o, *, tm=512):
    t, d = x.shape
    tm = min(tm, t)
    return pl.pallas_call(
        _mla_out_kernel,
        grid=(t // tm,),
        in_specs=[pl.BlockSpec((tm, d), lambda i: (i, 0)),
                  pl.BlockSpec((MLA_HEADS, tm, KV_RANK), lambda i: (0, i, 0)),
                  _const_spec(w_uv_h.shape), _const_spec(w_o.shape)],
        out_specs=pl.BlockSpec((tm, d), lambda i: (i, 0)),
        out_shape=jax.ShapeDtypeStruct((t, d), F32),
        compiler_params=_cparams("parallel"),
        name="mla_out",
    )(x, o, w_uv_h, w_o)


def _conv_kernel(x_ref, st_ref, gm_ref, win_ref, wc_ref, wout_ref, y_ref, cs_ref, carry_sc,
                 *, tiles_per_seq, seq_rows):
    x = x_ref[...]
    tm, d = x.shape
    h = _rms(x, gm_ref[...]).astype(BF)
    bch = _dot(h, win_ref[...])
    gate_b = bch[:, :d]
    z = bch[:, d:2 * d] * bch[:, 2 * d:]
    z1 = pltpu.roll(z, 1, axis=0)
    z2 = pltpu.roll(z, 2, axis=0)
    row = lax.broadcasted_iota(jnp.int32, z.shape, 0)
    if tiles_per_seq is not None:
        i = pl.program_id(0)
        first = (i % tiles_per_seq) == 0
        st = st_ref[...]
        prev2 = jnp.where(first, st[0:1], carry_sc[6:7])
        prev1 = jnp.where(first, st[1:2], carry_sc[7:8])
        z1 = jnp.where(row == 0, prev1, z1)
        z2 = jnp.where(row == 0, prev2, jnp.where(row == 1, prev1, z2))
        carry_sc[...] = z[tm - 8:tm]
        cs_ref[...] = z[tm - 2:tm]
    else:
        nb = tm // seq_rows
        st = st_ref[...]
        j = lax.broadcasted_iota(jnp.int32, (nb, seq_rows, d), 1)
        z3 = z.reshape(nb, seq_rows, d)
        p1 = st[:, 1:2, :]
        p2 = st[:, 0:1, :]
        z1 = jnp.where(j == 0, p1, z1.reshape(nb, seq_rows, d)).reshape(tm, d)
        z2 = jnp.where(j == 0, p2, jnp.where(j == 1, p1, z2.reshape(nb, seq_rows, d))).reshape(tm, d)
        cs_ref[...] = z3[:, seq_rows - 2:, :]
    wc = wc_ref[...]
    y = wc[0:1] * z2 + wc[1:2] * z1 + wc[2:3] * z
    y_ref[...] = x + _dot((gate_b * y).astype(BF), wout_ref[...])


def _conv_prompt(x, state, gm, w_in, w_conv, w_out, *, batch, seq, tm=512):
    t, d = x.shape
    tps = seq // tm
    return pl.pallas_call(
        functools.partial(_conv_kernel, tiles_per_seq=tps, seq_rows=seq),
        grid=(t // tm,),
        in_specs=[pl.BlockSpec((tm, d), lambda i: (i, 0)),
                  pl.BlockSpec((None, 2, d), lambda i: (i // tps, 0, 0)),
                  _const_spec(gm.shape), _const_spec(w_in.shape), _const_spec(w_conv.shape),
                  _const_spec(w_out.shape)],
        out_specs=[pl.BlockSpec((tm, d), lambda i: (i, 0)),
                   pl.BlockSpec((None, 2, d), lambda i: (i // tps, 0, 0))],
        out_shape=[jax.ShapeDtypeStruct((t, d), F32), jax.ShapeDtypeStruct((batch, 2, d), F32)],
        scratch_shapes=[pltpu.VMEM((8, d), F32)],
        compiler_params=_cparams("arbitrary"),
        name="conv_prompt",
    )(x, state, gm, w_in, w_conv, w_out)


def _conv_sample(x, state, gm, w_in, w_conv, w_out, *, batch, seq, tm=512):
    t, d = x.shape
    tm = min(tm, t)
    nb = tm // seq
    return pl.pallas_call(
        functools.partial(_conv_kernel, tiles_per_seq=None, seq_rows=seq),
        grid=(t // tm,),
        in_specs=[pl.BlockSpec((tm, d), lambda i: (i, 0)),
                  pl.BlockSpec((nb, 2, d), lambda i: (i, 0, 0)),
                  _const_spec(gm.shape), _const_spec(w_in.shape), _const_spec(w_conv.shape),
                  _const_spec(w_out.shape)],
        out_specs=[pl.BlockSpec((tm, d), lambda i: (i, 0)),
                   pl.BlockSpec((nb, 2, d), lambda i: (i, 0, 0))],
        out_shape=[jax.ShapeDtypeStruct((t, d), F32), jax.ShapeDtypeStruct((batch, 2, d), F32)],
        scratch_shapes=[pltpu.VMEM((8, d), F32)],
        compiler_params=_cparams("parallel"),
        name="conv_sample",
    )(x, state, gm, w_in, w_conv, w_out)


GW = HPG * HD


def _dil_proj_kernel(x_ref, gm_ref, w_ref, q_ref, kv0_ref, kv1_ref, kv2_ref, *, scale):
    h = _rms(x_ref[...], gm_ref[...]).astype(BF)
    nq = 3 * GW
    q_ref[...] = (_dot(h, w_ref[:, :nq]) * scale).astype(q_ref.dtype)
    for g, kv_ref in enumerate((kv0_ref, kv1_ref, kv2_ref)):
        kv_ref[...] = _dot(h, w_ref[:, nq + g * 2 * GW: nq + (g + 1) * 2 * GW])


def _dil_proj(x, gm, w_qkv, *, q_dtype, tm=256):
    t, d = x.shape
    tm = min(tm, t)
    row = lambda i: (i, 0)
    return pl.pallas_call(
        functools.partial(_dil_proj_kernel, scale=1.0 / math.sqrt(HD)),
        grid=(t // tm,),
        in_specs=[pl.BlockSpec((tm, d), row), _const_spec(gm.shape), _const_spec(w_qkv.shape)],
        out_specs=[pl.BlockSpec((tm, 3 * GW), row)] + [pl.BlockSpec((tm, 2 * GW), row)] * 3,
        out_shape=[jax.ShapeDtypeStruct((t, 3 * GW), q_dtype)]
                  + [jax.ShapeDtypeStruct((t, 2 * GW), F32)] * 3,
        compiler_params=_cparams("parallel"),
        name="dil_proj",
    )(x, gm, w_qkv)


def _dil_attn_p_kernel(q_ref, kvp_ref, kvc_ref, bias_ref, o_ref, lse_ref):
    i = pl.program_id(2)
    q = q_ref[...]
    kvp = kvp_ref[...]
    kvc = kvc_ref[...]
    lane_grp = lax.broadcasted_iota(jnp.int32, (WIN_TILE, LANES), 1) // (LANES // HPG)
    lse_tile = jnp.zeros((WIN_TILE, LANES), F32)
    prev_ok = jnp.where(i > 0, 0.0, NEG)
    col = lax.broadcasted_iota(jnp.int32, (WIN_TILE, 2 * WIN_TILE), 1)
    prev_mask = jnp.where(col < WIN_TILE, prev_ok, 0.0)
    for hh in range(HPG):
        sl = slice(hh * HD, (hh + 1) * HD)
        vsl = slice(GW + hh * HD, GW + (hh + 1) * HD)
        k = jnp.concatenate([kvp[:, sl], kvc[:, sl]], axis=0).astype(BF)
        v = jnp.concatenate([kvp[:, vsl], kvc[:, vsl]], axis=0).astype(BF)
        s = _dot_nt(q[:, sl], k) + bias_ref[hh] + prev_mask
        m = jnp.max(s, axis=-1, keepdims=True)
        e = jnp.exp(s - m)
        l = jnp.sum(e, axis=-1, keepdims=True)
        p = (e / l).astype(BF)
        o_ref[:, sl] = _dot(p, v)
        lse_tile = jnp.where(lane_grp == hh, m + jnp.log(l), lse_tile)
    lse_ref[...] = lse_tile


def _dil_attn_prompt(q, kv, bias, *, g, batch, seq):
    dil = DILATIONS[g]
    ph_len = seq // dil
    n_i = ph_len // WIN_TILE
    t = batch * seq
    qv = q.reshape(batch, ph_len, dil * 3 * GW)
    kvv = kv.reshape(batch, ph_len, dil * 2 * GW)
    o, lse = pl.pallas_call(
        _dil_attn_p_kernel,
        grid=(batch, dil, n_i),
        in_specs=[pl.BlockSpec((None, WIN_TILE, GW), lambda b, r, i: (b, i, r * 3 + g)),
                  pl.BlockSpec((None, WIN_TILE, 2 * GW), lambda b, r, i: (b, jnp.maximum(i - 1, 0), r)),
                  pl.BlockSpec((None, WIN_TILE, 2 * GW), lambda b, r, i: (b, i, r)),
                  _const_spec(bias.shape)],
        out_specs=[pl.BlockSpec((None, WIN_TILE, GW), lambda b, r, i: (b, i, r)),
                   pl.BlockSpec((None, WIN_TILE, LANES), lambda b, r, i: (b, i, r))],
        out_shape=[jax.ShapeDtypeStruct((batch, ph_len, dil * GW), F32),
                   jax.ShapeDtypeStruct((batch, ph_len, dil * LANES), F32)],
        compiler_params=_cparams("parallel", "parallel", "arbitrary"),
        name=f"dil_attn_prompt_g{g}",
    )(qv, kvv, kvv, bias)
    return o.reshape(t, GW), lse.reshape(t, LANES)


def _dil_comb_kernel(x_ref, o0_ref, o1_ref, o2_ref, l0_ref, l1_ref, l2_ref, w_ref, y_ref):
    l0, l1, l2 = l0_ref[...], l1_ref[...], l2_ref[...]
    m = jnp.maximum(jnp.maximum(l0, l1), l2)
    e0, e1, e2 = jnp.exp(l0 - m), jnp.exp(l1 - m), jnp.exp(l2 - m)
    den = e0 + e1 + e2
    ws = (e0 / den, e1 / den, e2 / den)
    tm = x_ref.shape[0]
    parts = []
    for hh in range(HPG):
        sl = slice(hh * HD, (hh + 1) * HD)
        acc = jnp.zeros((tm, HD), F32)
        for wg, o_ref in zip(ws, (o0_ref, o1_ref, o2_ref)):
            wcol = wg[:, hh * (LANES // HPG): hh * (LANES // HPG) + 1]
            acc = acc + wcol * o_ref[:, sl]
        parts.append(acc.astype(BF))
    y_ref[...] = x_ref[...] + _dot(jnp.concatenate(parts, axis=1), w_ref[...])


def _dil_comb(x, os_, lses, w_o, *, tm=512):
    t, d = x.shape
    row = lambda i: (i, 0)
    return pl.pallas_call(
        _dil_comb_kernel,
        grid=(t // tm,),
        in_specs=[pl.BlockSpec((tm, d), row)] + [pl.BlockSpec((tm, GW), row)] * 3
                 + [pl.BlockSpec((tm, LANES), row)] * 3 + [_const_spec(w_o.shape)],
        out_specs=pl.BlockSpec((tm, d), row),
        out_shape=jax.ShapeDtypeStruct((t, d), F32),
        compiler_params=_cparams("parallel"),
        name="dil_comb",
    )(x, *os_, *lses, w_o)


def _dil_attn_s_kernel(q_ref, kn0_ref, kn1_ref, kn2_ref, b0_ref, b1_ref, b2_ref,
                       t0_ref, t1_ref, t2_ref, n0_ref, n1_ref, n2_ref, o_ref, *, s_new):
    q = q_ref[...]
    rows_w = lax.broadcasted_iota(jnp.int32, (LANES, GW), 0)
    lanes_w = lax.broadcasted_iota(jnp.int32, (LANES, GW), 1)
    head_sel = (lanes_w // HD) == (rows_w // s_new)
    logits = []
    vals = []
    for g, (kn_ref, b_ref, t_ref, n_ref) in enumerate((
            (kn0_ref, b0_ref, t0_ref, n0_ref), (kn1_ref, b1_ref, t1_ref, n1_ref),
            (kn2_ref, b2_ref, t2_ref, n2_ref))):
        qg = q[:, g * GW:(g + 1) * GW]
        wq = jnp.where(head_sel, jnp.tile(qg, (LANES // s_new, 1)), 0.0).astype(BF)
        buf = b_ref[...]
        new = jnp.concatenate([kn_ref[...], jnp.zeros((LANES - s_new, 2 * GW), F32)], axis=0)
        logits.append(_dot_nt(buf[:, :GW].astype(BF), wq) + t_ref[...])
        vals.append(buf[:, GW:].astype(BF))
        logits.append(_dot_nt(new[:, :GW].astype(BF), wq) + n_ref[...])
        vals.append(new[:, GW:].astype(BF))
    m = logits[0].max(axis=0, keepdims=True)
    for lg in logits[1:]:
        m = jnp.maximum(m, lg.max(axis=0, keepdims=True))
    ps = [jnp.exp(lg - m) for lg in logits]
    den = ps[0].sum(axis=0, keepdims=True)
    for p in ps[1:]:
        den = den + p.sum(axis=0, keepdims=True)
    inv = 1.0 / den
    out = jnp.zeros((LANES, GW), F32)
    for p, v in zip(ps, vals):
        out = out + _dot_tn((p * inv).astype(BF), v)
    for hh in range(HPG):
        o_ref[:, hh * HD:(hh + 1) * HD] = out[hh * s_new:(hh + 1) * s_new, hh * HD:(hh + 1) * HD]


def _dil_attn_sample(q, kv_new, bufs, tabs, ntabs, *, batch, s_new):
    t = batch * s_new
    bufs2 = [b.reshape(batch, b.shape[1], 2 * GW) for b in bufs]
    return pl.pallas_call(
        functools.partial(_dil_attn_s_kernel, s_new=s_new),
        grid=(batch,),
        in_specs=[pl.BlockSpec((s_new, 3 * GW), lambda b: (b, 0))]
                 + [pl.BlockSpec((s_new, 2 * GW), lambda b: (b, 0))] * 3
                 + [pl.BlockSpec((None, bb.shape[1], 2 * GW), lambda b: (b, 0, 0)) for bb in bufs2]
                 + [_const_spec(tb.shape) for tb in tabs]
                 + [_const_spec(tb.shape) for tb in ntabs],
        out_specs=pl.BlockSpec((s_new, GW), lambda b: (b, 0)),
        out_shape=jax.ShapeDtypeStruct((t, GW), F32),
        compiler_params=_cparams("parallel"),
        name="dil_attn_sample",
    )(q, *kv_new, *bufs2, *tabs, *ntabs)


def _sgu_kernel(x_ref, gm_ref, wuv_ref, gs_ref, wmix_ref, bmix_ref, wout_ref, y_ref, *rest, emit_v):
    if emit_v:
        v_ref, mixed_sc = rest
    else:
        (mixed_sc,) = rest
    x = x_ref[...]
    tm, d = x.shape
    h = _rms(x, gm_ref[...]).astype(BF)
    uv = jax.nn.gelu(_dot(h, wuv_ref[...]))
    w = uv.shape[1] // 2
    u = uv[:, :w]
    v = _rms(uv[:, w:], gs_ref[...])
    if emit_v:
        v_ref[...] = v
    vb = v.astype(BF)
    gd = w // SGU_GROUPS
    for c in range(tm // CHUNK):
        rs = slice(c * CHUNK, (c + 1) * CHUNK)
        for g in range(SGU_GROUPS):
            cs = slice(g * gd, (g + 1) * gd)
            mixed_sc[rs, cs] = _dot(wmix_ref[g], vb[rs, cs]) + bmix_ref[:, cs]
    y_ref[...] = x + _dot((u * mixed_sc[...]).astype(BF), wout_ref[...])


def _sgu(x, gm, w_uv, g_sgu, wmix, bmix, w_out, *, emit_v, tm=512):
    t, d = x.shape
    tm = min(tm, t)
    w = w_uv.shape[1] // 2
    row = lambda i: (i, 0)
    out_specs = [pl.BlockSpec((tm, d), row)]
    out_shape = [jax.ShapeDtypeStruct((t, d), F32)]
    if emit_v:
        out_specs.append(pl.BlockSpec((tm, w), row))
        out_shape.append(jax.ShapeDtypeStruct((t, w), F32))
    consts = [gm, w_uv, g_sgu, wmix, bmix, w_out]
    return pl.pallas_call(
        functools.partial(_sgu_kernel, emit_v=emit_v),
        grid=(t // tm,),
        in_specs=[pl.BlockSpec((tm, d), row)] + [_const_spec(c.shape) for c in consts],
        out_specs=out_specs,
        out_shape=out_shape,
        scratch_shapes=[pltpu.VMEM((tm, w), F32)],
        compiler_params=_cparams("parallel"),
        name="sgu",
    )(x, *consts)


def _rot_half_cols(w):
    half = w.shape[-1] // 2
    return jnp.concatenate([-w[..., half:], w[..., :half]], axis=-1)


def _pad_last(w, n):
    return jnp.pad(w, [(0, 0)] * (w.ndim - 1) + [(0, n - w.shape[-1])])


def _rope_tables(pos):
    half = ROPE_DIM // 2
    inv = ROPE_THETA ** (-jnp.arange(half, dtype=F32) / half)
    ang = pos.astype(F32)[:, None] * inv[None, :]
    cos, sin = jnp.cos(ang), jnp.sin(ang)
    return (_pad_last(jnp.concatenate([cos, cos], axis=1), LANES),
            _pad_last(jnp.concatenate([sin, sin], axis=1), LANES))


def _t5_bucket(dist):
    max_exact = N_BUCKETS // 2
    n = jnp.maximum(dist, 1).astype(F32)
    large = max_exact + (jnp.log(n / max_exact) / math.log(MAX_DISTANCE / max_exact)
                         * (N_BUCKETS - max_exact)).astype(jnp.int32)
    return jnp.where(dist < max_exact, dist, jnp.minimum(large, N_BUCKETS - 1))


def _group_bias(rel_bias, g):
    n_keys = WINDOWS[g] // DILATIONS[g] + 1
    dist = DILATIONS[g] * jnp.arange(n_keys, dtype=jnp.int32)
    return rel_bias[_t5_bucket(dist)][:, g * HPG:(g + 1) * HPG].T


def _prompt_bias_table(bias_g):
    qi = jnp.arange(WIN_TILE)[:, None]
    kj = jnp.arange(2 * WIN_TILE)[None, :] - WIN_TILE
    k = qi - kj
    ok = (k >= 0) & (k <= WIN_TILE)
    tab = bias_g[:, jnp.clip(k, 0, WIN_TILE)]
    return jnp.where(ok[None], tab, NEG).astype(F32)


def _sample_bias_tables(bias_g, g, s_new):
    win, dil = WINDOWS[g], DILATIONS[g]
    lane = jnp.arange(LANES)
    hh = jnp.minimum(lane // s_new, HPG - 1)
    j = lane % s_new
    used = lane < HPG * s_new

    def table(idx):
        delta = (win + j)[None, :] - idx[:, None]
        k = delta // dil
        ok = (delta >= 0) & (delta % dil == 0) & (k <= win // dil) & used[None, :]
        tab = bias_g[hh[None, :], jnp.clip(k, 0, win // dil)]
        return jnp.where(ok, tab, NEG).astype(F32)

    new_tab = jnp.pad(table(win + jnp.arange(s_new)), ((0, LANES - s_new), (0, 0)), constant_values=NEG)
    return table(jnp.arange(win)), new_tab


def _prep_weights(p, s_new):
    w = {}
    w["g_mix0"] = p["norm_mix"][0][None]
    w["w_dq"] = p["w_dq"].astype(BF)
    w["g_q"] = p["g_q"][None]
    q_rank = p["w_uq"].shape[0]
    wuq = p["w_uq"].reshape(q_rank, MLA_HEADS, NOPE_DIM + ROPE_DIM)
    w["w_uq_nope"] = wuq[:, :, :NOPE_DIM].reshape(q_rank, -1).astype(BF)
    wr = wuq[:, :, NOPE_DIM:]
    w["w_uq_rope"] = _pad_last(wr, LANES).reshape(q_rank, -1).astype(BF)
    w["w_uq_rot"] = _pad_last(_rot_half_cols(wr), LANES).reshape(q_rank, -1).astype(BF)
    w["w_uk_t"] = jnp.transpose(p["w_uk"], (1, 2, 0)).astype(BF)
    w["w_dkv"] = _pad_last(p["w_dkv"], KV_RANK + LANES).astype(BF)
    w["w_dkv_rot"] = _pad_last(_rot_half_cols(p["w_dkv"][:, KV_RANK:]), LANES).astype(BF)
    w["g_kv"] = p["g_kv"][None]
    w["w_uv_h"] = jnp.transpose(p["w_uv"], (1, 0, 2)).astype(BF)
    w["w_o_mla"] = p["w_o_mla"].astype(BF)
    w["w_in_conv"] = p["w_in_conv"].astype(BF)
    w["w_out_conv"] = p["w_out_conv"].astype(BF)
    d = p["w_qkv_c"].shape[0]
    wqkv = p["w_qkv_c"].reshape(d, 3, 3, GW)
    cols = [wqkv[:, 0].reshape(d, 3 * GW)]
    for g in range(3):
        cols += [wqkv[:, 1, g], wqkv[:, 2, g]]
    w["w_qkv"] = jnp.concatenate(cols, axis=1).astype(BF)
    w["w_o_c"] = p["w_o_c"].astype(BF)
    biases = [_group_bias(p["rel_bias"], g) for g in range(3)]
    w["bias_p"] = [_prompt_bias_table(b) for b in biases]
    tabs = [_sample_bias_tables(b, g, s_new) for g, b in enumerate(biases)]
    w["tab_s"] = [tb[0] for tb in tabs]
    w["ntab_s"] = [tb[1] for tb in tabs]
    w["w_uv_d"] = p["w_uv_d"].astype(BF)
    w["g_sgu"] = p["g_sgu"][None]
    w["w_out_d"] = p["w_out_d"].astype(BF)
    gd = p["w_uv_d"].shape[1] // 2 // SGU_GROUPS
    tril = jnp.tril(jnp.ones((CHUNK, CHUNK), F32))
    w["wmix_p"] = (p["w_s"] * tril).astype(BF)
    w["bmix_p"] = jnp.repeat(p["b_s"].T, gd, axis=1)
    ws_s = p["w_s"][:, :s_new, :s_new] * jnp.tril(jnp.ones((s_new, s_new), F32))
    eye = jnp.eye(CHUNK // s_new, dtype=F32)
    w["wmix_s"] = jnp.einsum("ab,gij->gaibj", eye, ws_s).reshape(SGU_GROUPS, CHUNK, CHUNK).astype(BF)
    w["bmix_s"] = jnp.repeat(jnp.tile(p["b_s"][:, :s_new].T, (CHUNK // s_new, 1)), gd, axis=1)
    w["w_ffn1"] = p["w_ffn1"].astype(BF)
    w["w_ffn2"] = p["w_ffn2"].astype(BF)
    return w


def kernel(x_prompt, x_sample, cache_mla, page_table, state_conv, state_win1, state_win2, state_win3, norm_mix, norm_ffn, norm_final, w_dq, g_q, w_uq, w_dkv, g_kv, w_uk, w_uv, w_o_mla, w_in_conv, w_conv, w_out_conv, w_qkv_c, w_o_c, rel_bias, w_uv_d, g_sgu, w_s, b_s, w_out_d, w_ffn1, w_ffn2):
    bp, sp, d = x_prompt.shape
    bs, ss, _ = x_sample.shape
    depth = norm_mix.shape[0]
    past_len = page_table.shape[1] * cache_mla.shape[1]
    params = dict(norm_mix=norm_mix, w_dq=w_dq, g_q=g_q, w_uq=w_uq, w_dkv=w_dkv, g_kv=g_kv, w_uk=w_uk,
                  w_uv=w_uv, w_o_mla=w_o_mla, w_in_conv=w_in_conv, w_out_conv=w_out_conv,
                  w_qkv_c=w_qkv_c, w_o_c=w_o_c, rel_bias=rel_bias, w_uv_d=w_uv_d, g_sgu=g_sgu,
                  w_s=w_s, b_s=b_s, w_out_d=w_out_d, w_ffn1=w_ffn1, w_ffn2=w_ffn2)
    w = _prep_weights(params, ss)
    xp = x_prompt.reshape(bp * sp, d)
    xs = x_sample.reshape(bs * ss, d)
    outs = {}
    for i in range(depth):
        gm = norm_mix[i][None]
        kind = i % 4
        if kind == 0:
            w["g_mix0"] = gm
            tm_p = 512
            cos_p, sin_p = _rope_tables(jnp.arange(sp, dtype=jnp.int32))
            ql, qp, rows_p, kc, kp = _mla_proj(xp, cos_p, sin_p, w, pos_blocks=sp // tm_p,
                                               q_dtype=BF, tm=tm_p)
            o = _mla_attn_prompt(ql, qp, kc, kp, batch=bp, seq=sp)
            xp = _mla_out(xp, o, w["w_uv_h"], w["w_o_mla"])
            pos_s = past_len + jnp.arange(ss, dtype=jnp.int32)
            cos_s, sin_s = _rope_tables(jnp.tile(pos_s, bs))
            tm_s = min(512, bs * ss)
            ql, qp, rows_s, _, _ = _mla_proj(xs, cos_s, sin_s, w, pos_blocks=bs * ss // tm_s,
                                             q_dtype=F32, tm=tm_s)
            o = _mla_attn_sample(ql.reshape(MLA_HEADS, bs, ss, KV_RANK),
                                 qp.reshape(MLA_HEADS, bs, ss, LANES),
                                 rows_s.reshape(bs, ss, -1), cache_mla, page_table)
            xs = _mla_out(xs, o.reshape(MLA_HEADS, bs * ss, KV_RANK), w["w_uv_h"], w["w_o_mla"])
            outs["mla_rows_p"] = rows_p.reshape(bp, sp, -1)
            outs["mla_rows_s"] = rows_s.reshape(bs, ss, -1)
        elif kind == 1:
            zeros_p = jnp.zeros((bp, 2, d), F32)
            xp, outs["conv_p"] = _conv_prompt(xp, zeros_p, gm, w["w_in_conv"], w_conv, w["w_out_conv"],
                                              batch=bp, seq=sp)
            xs, outs["conv_s"] = _conv_sample(xs, state_conv, gm, w["w_in_conv"], w_conv,
                                              w["w_out_conv"], batch=bs, seq=ss)
        elif kind == 2:
            q, kv0, kv1, kv2 = _dil_proj(xp, gm, w["w_qkv"], q_dtype=BF)
            kvs = (kv0, kv1, kv2)
            res = [_dil_attn_prompt(q, kvs[g], w["bias_p"][g], g=g, batch=bp, seq=sp) for g in range(3)]
            xp = _dil_comb(xp, [r[0] for r in res], [r[1] for r in res], w["w_o_c"])
            for g in range(3):
                rows = kvs[g].reshape(bp, sp, 2, HPG, HD)
                outs[f"win{g + 1}_p"] = rows[:, sp - min(WINDOWS[g], sp):]
            qs, kn0, kn1, kn2 = _dil_proj(xs, gm, w["w_qkv"], q_dtype=F32)
            kns = (kn0, kn1, kn2)
            o = _dil_attn_sample(qs, kns, (state_win1, state_win2, state_win3), w["tab_s"], w["ntab_s"],
                                 batch=bs, s_new=ss)
            xs = _proj_res(xs, o, w["w_o_c"])
            for g in range(3):
                outs[f"win{g + 1}_s"] = kns[g].reshape(bs, ss, 2, HPG, HD)
        else:
            (xp,) = _sgu(xp, gm, w["w_uv_d"], w["g_sgu"], w["wmix_p"], w["bmix_p"], w["w_out_d"],
                         emit_v=False)
            xs, v_s = _sgu(xs, gm, w["w_uv_d"], w["g_sgu"], w["wmix_s"], w["bmix_s"], w["w_out_d"],
                           emit_v=True)
            outs["sgu_v_s"] = v_s.reshape(bs, ss, -1)
        last = i == depth - 1
        gf = norm_final[None]
        xp = _ffn(xp, norm_ffn[i][None], w["w_ffn1"][i], w["w_ffn2"][i], gf, final_norm=last)
        xs = _ffn(xs, norm_ffn[i][None], w["w_ffn1"][i], w["w_ffn2"][i], gf, final_norm=last)
    return (xp.reshape(bp, sp, d), xs.reshape(bs, ss, d), outs["mla_rows_p"], outs["mla_rows_s"],
            outs["conv_p"], outs["conv_s"], outs["win1_p"], outs["win1_s"], outs["win2_p"],
            outs["win2_s"], outs["win3_p"], outs["win3_s"], outs["sgu_v_s"])
```

```python
import functools
import math

import jax
import jax.numpy as jnp
from jax import lax
from jax.experimental import pallas as pl
from jax.experimental.pallas import tpu as pltpu

EPS = 1e-6
ROPE_THETA = 10000.0
NEG = -1e30
BF = jnp.bfloat16
F32 = jnp.float32
LANES = 128
VMEM_LIMIT = 52 * 1024 * 1024

MLA_HEADS = 8
NOPE_DIM = 128
ROPE_DIM = 64
KV_RANK = 256
V_DIM = 128
WINDOWS = (128, 512, 2048)
DILATIONS = (1, 4, 16)
HPG = 4
HD = 128
N_BUCKETS = 32
MAX_DISTANCE = 2048
CHUNK = 128
SGU_GROUPS = 8
WIN_TILE = 128


def _cparams(*sem):
    return pltpu.CompilerParams(dimension_semantics=sem, vmem_limit_bytes=VMEM_LIMIT)


def _rms(x, g):
    return x * lax.rsqrt(jnp.mean(x * x, axis=-1, keepdims=True) + EPS) * g


def _dot(a, b):
    return jnp.dot(a, b, preferred_element_type=F32)


def _dot_nt(a, b):
    return lax.dot_general(a, b, (((1,), (1,)), ((), ())), preferred_element_type=F32)


def _dot_tn(a, b):
    return lax.dot_general(a, b, (((0,), (0,)), ((), ())), preferred_element_type=F32)


def _const_spec(shape):
    nd = len(shape)
    return pl.BlockSpec(shape, lambda *_: (0,) * nd)


def _ffn_kernel(x_ref, g_ref, w1_ref, w2_ref, gf_ref, o_ref, h_sc, acc_sc, *, nf, final_norm):
    f = pl.program_id(1)

    @pl.when(f == 0)
    def _():
        x = x_ref[...]
        h_sc[...] = _rms(x, g_ref[...]).astype(BF)
        acc_sc[...] = x

    a = _dot(h_sc[...], w1_ref[...])
    a = jnp.maximum(a, 0.0)
    acc_sc[...] += _dot((a * a).astype(BF), w2_ref[...])

    @pl.when(f == nf - 1)
    def _():
        y = acc_sc[...]
        if final_norm:
            y = _rms(y, gf_ref[...])
        o_ref[...] = y


def _ffn(x, g, w1, w2, gf, *, final_norm, tm=1024, tf=1024):
    t, d = x.shape
    dff = w1.shape[1]
    tm = min(tm, t)
    nf = dff // tf
    return pl.pallas_call(
        functools.partial(_ffn_kernel, nf=nf, final_norm=final_norm),
        grid=(t // tm, nf),
        in_specs=[
            pl.BlockSpec((tm, d), lambda i, f: (i, 0)),
            pl.BlockSpec((1, d), lambda i, f: (0, 0)),
            pl.BlockSpec((d, tf), lambda i, f: (0, f)),
            pl.BlockSpec((tf, d), lambda i, f: (f, 0)),
            pl.BlockSpec((1, d), lambda i, f: (0, 0)),
        ],
        out_specs=pl.BlockSpec((tm, d), lambda i, f: (i, 0)),
        out_shape=jax.ShapeDtypeStruct((t, d), F32),
        scratch_shapes=[pltpu.VMEM((tm, d), BF), pltpu.VMEM((tm, d), F32)],
        compiler_params=_cparams("parallel", "arbitrary"),
        name="ffn",
    )(x, g, w1, w2, gf)


def _proj_res_kernel(x_ref, a_ref, w_ref, o_ref):
    o_ref[...] = x_ref[...] + _dot(a_ref[...].astype(BF), w_ref[...])


def _proj_res(x, a, w, *, tm=512):
    t, d = x.shape
    k = a.shape[1]
    tm = min(tm, t)
    return pl.pallas_call(
        _proj_res_kernel,
        grid=(t // tm,),
        in_specs=[pl.BlockSpec((tm, d), lambda i: (i, 0)),
                  pl.BlockSpec((tm, k), lambda i: (i, 0)),
                  _const_spec(w.shape)],
        out_specs=pl.BlockSpec((tm, d), lambda i: (i, 0)),
        out_shape=jax.ShapeDtypeStruct((t, d), F32),
        compiler_params=_cparams("parallel"),
        name="proj_res",
    )(x, a, w)


def _mla_proj_kernel(x_ref, cos_ref, sin_ref, gm_ref, wdq_ref, gq_ref, wqn_ref, wqr_ref, wqt_ref,
                     wuk_ref, wkv_ref, wkt_ref, gkv_ref,
                     ql_ref, qp_ref, rows_ref, kc_ref, kp_ref, *, scale):
    h = _rms(x_ref[...], gm_ref[...]).astype(BF)
    cq = _rms(_dot(h, wdq_ref[...]), gq_ref[...]).astype(BF)
    cos = cos_ref[...]
    sin = sin_ref[...]
    q_nope = _dot(cq, wqn_ref[...])
    q_rope = _dot(cq, wqr_ref[...])
    q_rot = _dot(cq, wqt_ref[...])
    for hh in range(MLA_HEADS):
        sl = slice(hh * LANES, (hh + 1) * LANES)
        ql = _dot(q_nope[:, sl].astype(BF), wuk_ref[hh])
        ql_ref[hh] = (ql * scale).astype(ql_ref.dtype)
        qp = q_rope[:, sl] * cos + q_rot[:, sl] * sin
        qp_ref[hh] = (qp * scale).astype(qp_ref.dtype)
    ckv = _dot(h, wkv_ref[...])
    k_rot = _dot(h, wkt_ref[...])
    c = _rms(ckv[:, :KV_RANK], gkv_ref[...])
    kpe = ckv[:, KV_RANK:] * cos + k_rot * sin
    rows_ref[:, :KV_RANK] = c
    rows_ref[:, KV_RANK:] = kpe[:, :ROPE_DIM]
    kc_ref[...] = c.astype(BF)
    kp_ref[...] = kpe.astype(BF)


def _mla_proj(x, cos, sin, w, *, pos_blocks, q_dtype, tm=512):
    t, d = x.shape
    tm = min(tm, t)
    scale = 1.0 / math.sqrt(NOPE_DIM + ROPE_DIM)
    row = lambda i: (i, 0)
    consts = [w["g_mix0"], w["w_dq"], w["g_q"], w["w_uq_nope"], w["w_uq_rope"], w["w_uq_rot"],
              w["w_uk_t"], w["w_dkv"], w["w_dkv_rot"], w["g_kv"]]
    return pl.pallas_call(
        functools.partial(_mla_proj_kernel, scale=scale),
        grid=(t // tm,),
        in_specs=[pl.BlockSpec((tm, d), row),
                  pl.BlockSpec((tm, LANES), lambda i: (i % pos_blocks, 0)),
                  pl.BlockSpec((tm, LANES), lambda i: (i % pos_blocks, 0))]
                 + [_const_spec(c.shape) for c in consts],
        out_specs=[pl.BlockSpec((MLA_HEADS, tm, KV_RANK), lambda i: (0, i, 0)),
                   pl.BlockSpec((MLA_HEADS, tm, LANES), lambda i: (0, i, 0)),
                   pl.BlockSpec((tm, KV_RANK + ROPE_DIM), row),
                   pl.BlockSpec((tm, KV_RANK), row),
                   pl.BlockSpec((tm, LANES), row)],
        out_shape=[jax.ShapeDtypeStruct((MLA_HEADS, t, KV_RANK), q_dtype),
                   jax.ShapeDtypeStruct((MLA_HEADS, t, LANES), q_dtype),
                   jax.ShapeDtypeStruct((t, KV_RANK + ROPE_DIM), F32),
                   jax.ShapeDtypeStruct((t, KV_RANK), BF),
                   jax.ShapeDtypeStruct((t, LANES), BF)],
        compiler_params=_cparams("parallel"),
        name="mla_proj",
    )(x, cos, sin, *consts)


def _lane_rep(x, n):
    return x if n == LANES else jnp.concatenate([x] * (n // LANES), axis=1)


def _softmax_update(s, pv, m_ref, l_ref, acc_ref):
    m_old = m_ref[...]
    m_new = jnp.maximum(m_old, jnp.max(s, axis=-1, keepdims=True))
    alpha = jnp.exp(m_old - m_new)
    p = jnp.exp(s - _lane_rep(m_new, s.shape[1]))
    l_ref[...] = alpha * l_ref[...] + jnp.sum(p, axis=-1, keepdims=True)
    acc_ref[...] = _lane_rep(alpha, acc_ref.shape[1]) * acc_ref[...] + pv(p.astype(BF))
    m_ref[...] = m_new


def _mla_attn_p_kernel(ql_ref, qp_ref, kc_ref, kp_ref, o_ref, m_sc, l_sc, acc_sc, *, tq, tk, hc):
    i = pl.program_id(1)
    m_sc[...] = jnp.full(m_sc.shape, NEG, F32)
    l_sc[...] = jnp.zeros(l_sc.shape, F32)
    acc_sc[...] = jnp.zeros(acc_sc.shape, F32)
    n_full = (i * tq) // tk
    rc = hc * tq

    def tile(j, masked):
        start = pl.multiple_of(j * tk, tk)
        kc = kc_ref[pl.ds(start, tk), :]
        kp = kp_ref[pl.ds(start, tk), :]
        for c in range(MLA_HEADS // hc):
            ql = ql_ref[c * hc:(c + 1) * hc].reshape(rc, KV_RANK)
            qp = qp_ref[c * hc:(c + 1) * hc].reshape(rc, LANES)
            s = _dot_nt(ql, kc) + _dot_nt(qp, kp)
            if masked:
                q_pos = i * tq + lax.broadcasted_iota(jnp.int32, s.shape, 0) % tq
                k_pos = j * tk + lax.broadcasted_iota(jnp.int32, s.shape, 1)
                s = jnp.where(k_pos <= q_pos, s, NEG)
            rs = pl.ds(c * rc, rc)
            _softmax_update(s, lambda p: _dot(p, kc), m_sc.at[rs], l_sc.at[rs], acc_sc.at[rs])

    def body(j, carry):
        tile(j, False)
        return carry

    lax.fori_loop(0, n_full, body, 0)
    tile(n_full, True)
    o = acc_sc[...] / _lane_rep(l_sc[...], KV_RANK)
    o_ref[...] = o.reshape(MLA_HEADS, tq, KV_RANK).astype(o_ref.dtype)


def _mla_attn_prompt(ql, qp, kc, kp, *, batch, seq, tq=256, tk=512, hc=2):
    t = batch * seq
    nq = seq // tq
    rows = MLA_HEADS * tq
    qmap = lambda b, i: (0, b * nq + i, 0)
    return pl.pallas_call(
        functools.partial(_mla_attn_p_kernel, tq=tq, tk=tk, hc=hc),
        grid=(batch, nq),
        in_specs=[pl.BlockSpec((MLA_HEADS, tq, KV_RANK), qmap),
                  pl.BlockSpec((MLA_HEADS, tq, LANES), qmap),
                  pl.BlockSpec((seq, KV_RANK), lambda b, i: (b, 0)),
                  pl.BlockSpec((seq, LANES), lambda b, i: (b, 0))],
        out_specs=pl.BlockSpec((MLA_HEADS, tq, KV_RANK), qmap),
        out_shape=jax.ShapeDtypeStruct((MLA_HEADS, t, KV_RANK), BF),
        scratch_shapes=[pltpu.VMEM((rows, LANES), F32), pltpu.VMEM((rows, LANES), F32),
                        pltpu.VMEM((rows, KV_RANK), F32)],
        compiler_params=_cparams("parallel", "arbitrary"),
        name="mla_attn_prompt",
    )(ql, qp, kc, kp)


def _mla_attn_s_kernel(pt_ref, ql_ref, qp_ref, new_ref, *rest, pages, n_steps, page_size, s_new):
    del pt_ref
    cache_refs = rest[:pages]
    o_ref = rest[pages]
    m_sc, l_sc, acc_sc, kc_sc, kp_sc = rest[pages + 1:]
    step = pl.program_id(1)
    rows = MLA_HEADS * s_new
    ql = ql_ref[...].reshape(rows, KV_RANK).astype(BF)
    qp = qp_ref[...].reshape(rows, LANES).astype(BF)

    @pl.when(step == 0)
    def _():
        m_sc[...] = jnp.full(m_sc.shape, NEG, F32)
        l_sc[...] = jnp.zeros(l_sc.shape, F32)
        acc_sc[...] = jnp.zeros(acc_sc.shape, F32)

        kp_sc[ROPE_DIM:, :] = jnp.zeros((LANES - ROPE_DIM, kp_sc.shape[1]), BF)

    for k in range(pages):
        kb = cache_refs[k][...]
        sl = slice(k * page_size, (k + 1) * page_size)
        kc_sc[:, sl] = kb[:KV_RANK].astype(BF)
        kp_sc[:ROPE_DIM, sl] = kb[KV_RANK:].astype(BF)
    kc_t = kc_sc[...]
    s = _dot(ql, kc_t) + _dot(qp, kp_sc[...])
    _softmax_update(s, lambda p: _dot_nt(p, kc_t), m_sc, l_sc, acc_sc)

    @pl.when(step == n_steps - 1)
    def _():
        nr = new_ref[...]
        pad_r = jnp.zeros((LANES - s_new, KV_RANK + ROPE_DIM), F32)
        nb = jnp.concatenate([nr, pad_r], axis=0)
        nc = nb[:, :KV_RANK].astype(BF)
        npe = jnp.concatenate([nb[:, KV_RANK:], jnp.zeros((LANES, LANES - ROPE_DIM), F32)],
                              axis=1).astype(BF)
        sn = _dot_nt(ql, nc) + _dot_nt(qp, npe)
        q_pos = lax.broadcasted_iota(jnp.int32, sn.shape, 0) % s_new
        k_pos = lax.broadcasted_iota(jnp.int32, sn.shape, 1)
        sn = jnp.where(k_pos <= q_pos, sn, NEG)
        _softmax_update(sn, lambda p: _dot(p, nc), m_sc, l_sc, acc_sc)
        o = acc_sc[...] / _lane_rep(l_sc[...], KV_RANK)
        o_ref[...] = o.reshape(MLA_HEADS, s_new, KV_RANK)


def _mla_attn_sample(ql, qp, rows_new, cache, page_table, *, pages=16):
    _, bs, s_new, _ = ql.shape
    n_pages = page_table.shape[1]
    page_size = cache.shape[1]
    row_w = cache.shape[2]
    n_steps = n_pages // pages
    rows = MLA_HEADS * s_new
    pt = page_table.reshape(-1)
    qmap = lambda b, s, pt: (0, b, 0, 0)
    cache_t = jnp.transpose(cache, (0, 2, 1))

    def cache_spec(k):
        return pl.BlockSpec((None, row_w, page_size),
                            lambda b, s, pt: (pt[b * n_pages + s * pages + k], 0, 0))

    grid_spec = pltpu.PrefetchScalarGridSpec(
        num_scalar_prefetch=1,
        grid=(bs, n_steps),
        in_specs=[pl.BlockSpec((MLA_HEADS, None, s_new, KV_RANK), qmap),
                  pl.BlockSpec((MLA_HEADS, None, s_new, LANES), qmap),
                  pl.BlockSpec((None, s_new, row_w), lambda b, s, pt: (b, 0, 0))]
                 + [cache_spec(k) for k in range(pages)],
        out_specs=pl.BlockSpec((MLA_HEADS, None, s_new, KV_RANK), qmap),
        scratch_shapes=[pltpu.VMEM((rows, LANES), F32), pltpu.VMEM((rows, LANES), F32),
                        pltpu.VMEM((rows, KV_RANK), F32),
                        pltpu.VMEM((KV_RANK, pages * page_size), BF),
                        pltpu.VMEM((LANES, pages * page_size), BF)],
    )
    return pl.pallas_call(
        functools.partial(_mla_attn_s_kernel, pages=pages, n_steps=n_steps,
                          page_size=page_size, s_new=s_new),
        grid_spec=grid_spec,
        out_shape=jax.ShapeDtypeStruct((MLA_HEADS, bs, s_new, KV_RANK), F32),
        compiler_params=_cparams("parallel", "arbitrary"),
        name="mla_attn_sample",
    )(pt, ql, qp, rows_new, *([cache_t] * pages))


def _mla_out_kernel(x_ref, o_ref, wuv_ref, wo_ref, y_ref):
    vs = [_dot(o_ref[hh].astype(BF), wuv_ref[hh]).astype(BF) for hh in range(MLA_HEADS)]
    v = jnp.concatenate(vs, axis=1)
    y_ref[...] = x_ref[...] + _dot(v, wo_ref[...])


def _mla_out(x, o, w_uv_h, w_o, *, tm=512):
    t, d = x.shape
    tm = min(tm, t)
    return pl.pallas_call(
        _mla_out_kernel,
        grid=(t // tm,),
        in_specs=[pl.BlockSpec((tm, d), lambda i: (i, 0)),
                  pl.BlockSpec((MLA_HEADS, tm, KV_RANK), lambda i: (0, i, 0)),
                  _const_spec(w_uv_h.shape), _const_spec(w_o.shape)],
        out_specs=pl.BlockSpec((tm, d), lambda i: (i, 0)),
        out_shape=jax.ShapeDtypeStruct((t, d), F32),
        compiler_params=_cparams("parallel"),
        name="mla_out",
    )(x, o, w_uv_h, w_o)


def _conv_kernel(x_ref, st_ref, gm_ref, win_ref, wc_ref, wout_ref, y_ref, cs_ref, carry_sc,
                 *, tiles_per_seq, seq_rows):
    x = x_ref[...]
    tm, d = x.shape
    h = _rms(x, gm_ref[...]).astype(BF)
    bch = _dot(h, win_ref[...])
    gate_b = bch[:, :d]
    z = bch[:, d:2 * d] * bch[:, 2 * d:]
    z1 = pltpu.roll(z, 1, axis=0)
    z2 = pltpu.roll(z, 2, axis=0)
    row = lax.broadcasted_iota(jnp.int32, z.shape, 0)
    if tiles_per_seq is not None:
        i = pl.program_id(0)
        first = (i % tiles_per_seq) == 0
        st = st_ref[...]
        prev2 = jnp.where(first, st[0:1], carry_sc[6:7])
        prev1 = jnp.where(first, st[1:2], carry_sc[7:8])
        z1 = jnp.where(row == 0, prev1, z1)
        z2 = jnp.where(row == 0, prev2, jnp.where(row == 1, prev1, z2))
        carry_sc[...] = z[tm - 8:tm]
        cs_ref[...] = z[tm - 2:tm]
    else:
        nb = tm // seq_rows
        st = st_ref[...]
        j = lax.broadcasted_iota(jnp.int32, (nb, seq_rows, d), 1)
        z3 = z.reshape(nb, seq_rows, d)
        p1 = st[:, 1:2, :]
        p2 = st[:, 0:1, :]
        z1 = jnp.where(j == 0, p1, z1.reshape(nb, seq_rows, d)).reshape(tm, d)
        z2 = jnp.where(j == 0, p2, jnp.where(j == 1, p1, z2.reshape(nb, seq_rows, d))).reshape(tm, d)
        cs_ref[...] = z3[:, seq_rows - 2:, :]
    wc = wc_ref[...]
    y = wc[0:1] * z2 + wc[1:2] * z1 + wc[2:3] * z
    y_ref[...] = x + _dot((gate_b * y).astype(BF), wout_ref[...])


def _conv_prompt(x, state, gm, w_in, w_conv, w_out, *, batch, seq, tm=512):
    t, d = x.shape
    tps = seq // tm
    return pl.pallas_call(
        functools.partial(_conv_kernel, tiles_per_seq=tps, seq_rows=seq),
        grid=(t // tm,),
        in_specs=[pl.BlockSpec((tm, d), lambda i: (i, 0)),
                  pl.BlockSpec((None, 2, d), lambda i: (i // tps, 0, 0)),
                  _const_spec(gm.shape), _const_spec(w_in.shape), _const_spec(w_conv.shape),
                  _const_spec(w_out.shape)],
        out_specs=[pl.BlockSpec((tm, d), lambda i: (i, 0)),
                   pl.BlockSpec((None, 2, d), lambda i: (i // tps, 0, 0))],
        out_shape=[jax.ShapeDtypeStruct((t, d), F32), jax.ShapeDtypeStruct((batch, 2, d), F32)],
        scratch_shapes=[pltpu.VMEM((8, d), F32)],
        compiler_params=_cparams("arbitrary"),
        name="conv_prompt",
    )(x, state, gm, w_in, w_conv, w_out)


def _conv_sample(x, state, gm, w_in, w_conv, w_out, *, batch, seq, tm=512):
    t, d = x.shape
    tm = min(tm, t)
    nb = tm // seq
    return pl.pallas_call(
        functools.partial(_conv_kernel, tiles_per_seq=None, seq_rows=seq),
        grid=(t // tm,),
        in_specs=[pl.BlockSpec((tm, d), lambda i: (i, 0)),
                  pl.BlockSpec((nb, 2, d), lambda i: (i, 0, 0)),
                  _const_spec(gm.shape), _const_spec(w_in.shape), _const_spec(w_conv.shape),
                  _const_spec(w_out.shape)],
        out_specs=[pl.BlockSpec((tm, d), lambda i: (i, 0)),
                   pl.BlockSpec((nb, 2, d), lambda i: (i, 0, 0))],
        out_shape=[jax.ShapeDtypeStruct((t, d), F32), jax.ShapeDtypeStruct((batch, 2, d), F32)],
        scratch_shapes=[pltpu.VMEM((8, d), F32)],
        compiler_params=_cparams("parallel"),
        name="conv_sample",
    )(x, state, gm, w_in, w_conv, w_out)


GW = HPG * HD


def _dil_proj_kernel(x_ref, gm_ref, w_ref, q_ref, kv0_ref, kv1_ref, kv2_ref, *, scale):
    h = _rms(x_ref[...], gm_ref[...]).astype(BF)
    nq = 3 * GW
    q_ref[...] = (_dot(h, w_ref[:, :nq]) * scale).astype(q_ref.dtype)
    for g, kv_ref in enumerate((kv0_ref, kv1_ref, kv2_ref)):
        kv_ref[...] = _dot(h, w_ref[:, nq + g * 2 * GW: nq + (g + 1) * 2 * GW])


def _dil_proj(x, gm, w_qkv, *, q_dtype, tm=256):
    t, d = x.shape
    tm = min(tm, t)
    row = lambda i: (i, 0)
    return pl.pallas_call(
        functools.partial(_dil_proj_kernel, scale=1.0 / math.sqrt(HD)),
        grid=(t // tm,),
        in_specs=[pl.BlockSpec((tm, d), row), _const_spec(gm.shape), _const_spec(w_qkv.shape)],
        out_specs=[pl.BlockSpec((tm, 3 * GW), row)] + [pl.BlockSpec((tm, 2 * GW), row)] * 3,
        out_shape=[jax.ShapeDtypeStruct((t, 3 * GW), q_dtype)]
                  + [jax.ShapeDtypeStruct((t, 2 * GW), F32)] * 3,
        compiler_params=_cparams("parallel"),
        name="dil_proj",
    )(x, gm, w_qkv)


def _dil_attn_p_kernel(q_ref, kvp_ref, kvc_ref, bias_ref, o_ref, lse_ref):
    i = pl.program_id(2)
    q = q_ref[...]
    kvp = kvp_ref[...]
    kvc = kvc_ref[...]
    lane_grp = lax.broadcasted_iota(jnp.int32, (WIN_TILE, LANES), 1) // (LANES // HPG)
    lse_tile = jnp.zeros((WIN_TILE, LANES), F32)
    prev_ok = jnp.where(i > 0, 0.0, NEG)
    col = lax.broadcasted_iota(jnp.int32, (WIN_TILE, 2 * WIN_TILE), 1)
    prev_mask = jnp.where(col < WIN_TILE, prev_ok, 0.0)
    for hh in range(HPG):
        sl = slice(hh * HD, (hh + 1) * HD)
        vsl = slice(GW + hh * HD, GW + (hh + 1) * HD)
        k = jnp.concatenate([kvp[:, sl], kvc[:, sl]], axis=0).astype(BF)
        v = jnp.concatenate([kvp[:, vsl], kvc[:, vsl]], axis=0).astype(BF)
        s = _dot_nt(q[:, sl], k) + bias_ref[hh] + prev_mask
        m = jnp.max(s, axis=-1, keepdims=True)
        e = jnp.exp(s - m)
        l = jnp.sum(e, axis=-1, keepdims=True)
        p = (e / l).astype(BF)
        o_ref[:, sl] = _dot(p, v)
        lse_tile = jnp.where(lane_grp == hh, m + jnp.log(l), lse_tile)
    lse_ref[...] = lse_tile


def _dil_attn_prompt(q, kv, bias, *, g, batch, seq):
    dil = DILATIONS[g]
    ph_len = seq // dil
    n_i = ph_len // WIN_TILE
    t = batch * seq
    qv = q.reshape(batch, ph_len, dil * 3 * GW)
    kvv = kv.reshape(batch, ph_len, dil * 2 * GW)
    o, lse = pl.pallas_call(
        _dil_attn_p_kernel,
        grid=(batch, dil, n_i),
        in_specs=[pl.BlockSpec((None, WIN_TILE, GW), lambda b, r, i: (b, i, r * 3 + g)),
                  pl.BlockSpec((None, WIN_TILE, 2 * GW), lambda b, r, i: (b, jnp.maximum(i - 1, 0), r)),
                  pl.BlockSpec((None, WIN_TILE, 2 * GW), lambda b, r, i: (b, i, r)),
                  _const_spec(bias.shape)],
        out_specs=[pl.BlockSpec((None, WIN_TILE, GW), lambda b, r, i: (b, i, r)),
                   pl.BlockSpec((None, WIN_TILE, LANES), lambda b, r, i: (b, i, r))],
        out_shape=[jax.ShapeDtypeStruct((batch, ph_len, dil * GW), F32),
                   jax.ShapeDtypeStruct((batch, ph_len, dil * LANES), F32)],
        compiler_params=_cparams("parallel", "parallel", "arbitrary"),
        name=f"dil_attn_prompt_g{g}",
    )(qv, kvv, kvv, bias)
    return o.reshape(t, GW), lse.reshape(t, LANES)


def _dil_comb_kernel(x_ref, o0_ref, o1_ref, o2_ref, l0_ref, l1_ref, l2_ref, w_ref, y_ref):
    l0, l1, l2 = l0_ref[...], l1_ref[...], l2_ref[...]
    m = jnp.maximum(jnp.maximum(l0, l1), l2)
    e0, e1, e2 = jnp.exp(l0 - m), jnp.exp(l1 - m), jnp.exp(l2 - m)
    den = e0 + e1 + e2
    ws = (e0 / den, e1 / den, e2 / den)
    tm = x_ref.shape[0]
    parts = []
    for hh in range(HPG):
        sl = slice(hh * HD, (hh + 1) * HD)
        acc = jnp.zeros((tm, HD), F32)
        for wg, o_ref in zip(ws, (o0_ref, o1_ref, o2_ref)):
            wcol = wg[:, hh * (LANES // HPG): hh * (LANES // HPG) + 1]
            acc = acc + wcol * o_ref[:, sl]
        parts.append(acc.astype(BF))
    y_ref[...] = x_ref[...] + _dot(jnp.concatenate(parts, axis=1), w_ref[...])


def _dil_comb(x, os_, lses, w_o, *, tm=512):
    t, d = x.shape
    row = lambda i: (i, 0)
    return pl.pallas_call(
        _dil_comb_kernel,
        grid=(t // tm,),
        in_specs=[pl.BlockSpec((tm, d), row)] + [pl.BlockSpec((tm, GW), row)] * 3
                 + [pl.BlockSpec((tm, LANES), row)] * 3 + [_const_spec(w_o.shape)],
        out_specs=pl.BlockSpec((tm, d), row),
        out_shape=jax.ShapeDtypeStruct((t, d), F32),
        compiler_params=_cparams("parallel"),
        name="dil_comb",
    )(x, *os_, *lses, w_o)


def _dil_attn_s_kernel(q_ref, kn0_ref, kn1_ref, kn2_ref, b0_ref, b1_ref, b2_ref,
                       t0_ref, t1_ref, t2_ref, n0_ref, n1_ref, n2_ref, o_ref, *, s_new):
    q = q_ref[...]
    rows_w = lax.broadcasted_iota(jnp.int32, (LANES, GW), 0)
    lanes_w = lax.broadcasted_iota(jnp.int32, (LANES, GW), 1)
    head_sel = (lanes_w // HD) == (rows_w // s_new)
    logits = []
    vals = []
    for g, (kn_ref, b_ref, t_ref, n_ref) in enumerate((
            (kn0_ref, b0_ref, t0_ref, n0_ref), (kn1_ref, b1_ref, t1_ref, n1_ref),
            (kn2_ref, b2_ref, t2_ref, n2_ref))):
        qg = q[:, g * GW:(g + 1) * GW]
        wq = jnp.where(head_sel, jnp.tile(qg, (LANES // s_new, 1)), 0.0).astype(BF)
        n_rows = b_ref.shape[0] // (2 * HPG)
        buf_k = jnp.concatenate([b_ref[pl.ds(s, n_rows, stride=2 * HPG), :] for s in range(HPG)], axis=1)
        buf_v = jnp.concatenate([b_ref[pl.ds(HPG + s, n_rows, stride=2 * HPG), :] for s in range(HPG)],
                                axis=1)
        new = jnp.concatenate([kn_ref[...], jnp.zeros((LANES - s_new, 2 * GW), F32)], axis=0)
        logits.append(_dot_nt(buf_k.astype(BF), wq) + t_ref[...])
        vals.append(buf_v.astype(BF))
        logits.append(_dot_nt(new[:, :GW].astype(BF), wq) + n_ref[...])
        vals.append(new[:, GW:].astype(BF))
    m = logits[0].max(axis=0, keepdims=True)
    for lg in logits[1:]:
        m = jnp.maximum(m, lg.max(axis=0, keepdims=True))
    ps = [jnp.exp(lg - m) for lg in logits]
    den = ps[0].sum(axis=0, keepdims=True)
    for p in ps[1:]:
        den = den + p.sum(axis=0, keepdims=True)
    inv = 1.0 / den
    out = jnp.zeros((LANES, GW), F32)
    for p, v in zip(ps, vals):
        out = out + _dot_tn((p * inv).astype(BF), v)
    for hh in range(HPG):
        o_ref[:, hh * HD:(hh + 1) * HD] = out[hh * s_new:(hh + 1) * s_new, hh * HD:(hh + 1) * HD]


def _dil_attn_sample(q, kv_new, bufs, tabs, ntabs, *, batch, s_new):
    t = batch * s_new
    bufs2 = [b.reshape(batch, b.shape[1] * 2 * HPG, HD) for b in bufs]
    return pl.pallas_call(
        functools.partial(_dil_attn_s_kernel, s_new=s_new),
        grid=(batch,),
        in_specs=[pl.BlockSpec((s_new, 3 * GW), lambda b: (b, 0))]
                 + [pl.BlockSpec((s_new, 2 * GW), lambda b: (b, 0))] * 3
                 + [pl.BlockSpec((None, bb.shape[1], HD), lambda b: (b, 0, 0)) for bb in bufs2]
                 + [_const_spec(tb.shape) for tb in tabs]
                 + [_const_spec(tb.shape) for tb in ntabs],
        out_specs=pl.BlockSpec((s_new, GW), lambda b: (b, 0)),
        out_shape=jax.ShapeDtypeStruct((t, GW), F32),
        compiler_params=_cparams("parallel"),
        name="dil_attn_sample",
    )(q, *kv_new, *bufs2, *tabs, *ntabs)


def _sgu_kernel(x_ref, gm_ref, wuv_ref, gs_ref, wmix_ref, bmix_ref, wout_ref, y_ref, *rest, emit_v):
    if emit_v:
        v_ref, mixed_sc = rest
    else:
        (mixed_sc,) = rest
    x = x_ref[...]
    tm, d = x.shape
    h = _rms(x, gm_ref[...]).astype(BF)
    uv = jax.nn.gelu(_dot(h, wuv_ref[...]))
    w = uv.shape[1] // 2
    u = uv[:, :w]
    v = _rms(uv[:, w:], gs_ref[...])
    if emit_v:
        v_ref[...] = v
    vb = v.astype(BF)
    gd = w // SGU_GROUPS
    for c in range(tm // CHUNK):
        rs = slice(c * CHUNK, (c + 1) * CHUNK)
        for g in range(SGU_GROUPS):
            cs = slice(g * gd, (g + 1) * gd)
            mixed_sc[rs, cs] = _dot(wmix_ref[g], vb[rs, cs]) + bmix_ref[:, cs]
    y_ref[...] = x + _dot((u * mixed_sc[...]).astype(BF), wout_ref[...])


def _sgu(x, gm, w_uv, g_sgu, wmix, bmix, w_out, *, emit_v, tm=512):
    t, d = x.shape
    tm = min(tm, t)
    w = w_uv.shape[1] // 2
    row = lambda i: (i, 0)
    out_specs = [pl.BlockSpec((tm, d), row)]
    out_shape = [jax.ShapeDtypeStruct((t, d), F32)]
    if emit_v:
        out_specs.append(pl.BlockSpec((tm, w), row))
        out_shape.append(jax.ShapeDtypeStruct((t, w), F32))
    consts = [gm, w_uv, g_sgu, wmix, bmix, w_out]
    return pl.pallas_call(
        functools.partial(_sgu_kernel, emit_v=emit_v),
        grid=(t // tm,),
        in_specs=[pl.BlockSpec((tm, d), row)] + [_const_spec(c.shape) for c in consts],
        out_specs=out_specs,
        out_shape=out_shape,
        scratch_shapes=[pltpu.VMEM((tm, w), F32)],
        compiler_params=_cparams("parallel"),
        name="sgu",
    )(x, *consts)


def _rot_half_cols(w):
    half = w.shape[-1] // 2
    return jnp.concatenate([-w[..., half:], w[..., :half]], axis=-1)


def _pad_last(w, n):
    return jnp.pad(w, [(0, 0)] * (w.ndim - 1) + [(0, n - w.shape[-1])])


def _rope_tables(pos):
    half = ROPE_DIM // 2
    inv = ROPE_THETA ** (-jnp.arange(half, dtype=F32) / half)
    ang = pos.astype(F32)[:, None] * inv[None, :]
    cos, sin = jnp.cos(ang), jnp.sin(ang)
    return (_pad_last(jnp.concatenate([cos, cos], axis=1), LANES),
            _pad_last(jnp.concatenate([sin, sin], axis=1), LANES))


def _t5_bucket(dist):
    max_exact = N_BUCKETS // 2
    n = jnp.maximum(dist, 1).astype(F32)
    large = max_exact + (jnp.log(n / max_exact) / math.log(MAX_DISTANCE / max_exact)
                         * (N_BUCKETS - max_exact)).astype(jnp.int32)
    return jnp.where(dist < max_exact, dist, jnp.minimum(large, N_BUCKETS - 1))


def _group_bias(rel_bias, g):
    n_keys = WINDOWS[g] // DILATIONS[g] + 1
    dist = DILATIONS[g] * jnp.arange(n_keys, dtype=jnp.int32)
    return rel_bias[_t5_bucket(dist)][:, g * HPG:(g + 1) * HPG].T


def _prompt_bias_table(bias_g):
    period = 3 * WIN_TILE
    line = jnp.concatenate([bias_g[:, ::-1], jnp.full((HPG, period - WIN_TILE - 1), NEG, F32)], axis=1)
    flat = jnp.tile(line, (1, WIN_TILE))[:, :WIN_TILE * (period - 1)]
    return flat.reshape(HPG, WIN_TILE, period - 1)[:, :, :2 * WIN_TILE]


def _sample_bias_tables(bias_g, g, s_new):
    win, dil = WINDOWS[g], DILATIONS[g]
    n_keys = win // dil + 1
    line = jnp.concatenate([bias_g[:, :, None], jnp.full((HPG, n_keys, dil - 1), NEG, F32)], axis=2)
    line = line.reshape(HPG, n_keys * dil)
    line = jnp.concatenate([line[:, :win + 1], jnp.full((HPG, s_new), NEG, F32)], axis=1)
    buf = jnp.stack([line[:, j + 1: win + j + 1][:, ::-1] for j in range(s_new)])
    buf = jnp.transpose(buf, (2, 1, 0)).reshape(win, HPG * s_new)
    lpad = jnp.concatenate([jnp.full((HPG, s_new - 1), NEG, F32), line[:, :s_new]], axis=1)
    new = jnp.stack([lpad[:, j: j + s_new][:, ::-1] for j in range(s_new)])
    new = jnp.transpose(new, (2, 1, 0)).reshape(s_new, HPG * s_new)
    pad = lambda t, r: jnp.pad(t, ((0, r - t.shape[0]), (0, LANES - t.shape[1])), constant_values=NEG)
    return pad(buf, win), pad(new, LANES)


def _prep_weights(p, s_new):
    w = {}
    w["g_mix0"] = p["norm_mix"][0][None]
    w["w_dq"] = p["w_dq"].astype(BF)
    w["g_q"] = p["g_q"][None]
    q_rank = p["w_uq"].shape[0]
    wuq = p["w_uq"].reshape(q_rank, MLA_HEADS, NOPE_DIM + ROPE_DIM)
    w["w_uq_nope"] = wuq[:, :, :NOPE_DIM].reshape(q_rank, -1).astype(BF)
    wr = wuq[:, :, NOPE_DIM:]
    w["w_uq_rope"] = _pad_last(wr, LANES).reshape(q_rank, -1).astype(BF)
    w["w_uq_rot"] = _pad_last(_rot_half_cols(wr), LANES).reshape(q_rank, -1).astype(BF)
    w["w_uk_t"] = jnp.transpose(p["w_uk"], (1, 2, 0)).astype(BF)
    w["w_dkv"] = _pad_last(p["w_dkv"], KV_RANK + LANES).astype(BF)
    w["w_dkv_rot"] = _pad_last(_rot_half_cols(p["w_dkv"][:, KV_RANK:]), LANES).astype(BF)
    w["g_kv"] = p["g_kv"][None]
    w["w_uv_h"] = jnp.transpose(p["w_uv"], (1, 0, 2)).astype(BF)
    w["w_o_mla"] = p["w_o_mla"].astype(BF)
    w["w_in_conv"] = p["w_in_conv"].astype(BF)
    w["w_out_conv"] = p["w_out_conv"].astype(BF)
    d = p["w_qkv_c"].shape[0]
    wqkv = p["w_qkv_c"].reshape(d, 3, 3, GW)
    cols = [wqkv[:, 0].reshape(d, 3 * GW)]
    for g in range(3):
        cols += [wqkv[:, 1, g], wqkv[:, 2, g]]
    w["w_qkv"] = jnp.concatenate(cols, axis=1).astype(BF)
    w["w_o_c"] = p["w_o_c"].astype(BF)
    biases = [_group_bias(p["rel_bias"], g) for g in range(3)]
    w["bias_p"] = [_prompt_bias_table(b) for b in biases]
    tabs = [_sample_bias_tables(b, g, s_new) for g, b in enumerate(biases)]
    w["tab_s"] = [tb[0] for tb in tabs]
    w["ntab_s"] = [tb[1] for tb in tabs]
    w["w_uv_d"] = p["w_uv_d"].astype(BF)
    w["g_sgu"] = p["g_sgu"][None]
    w["w_out_d"] = p["w_out_d"].astype(BF)
    gd = p["w_uv_d"].shape[1] // 2 // SGU_GROUPS
    tril = jnp.tril(jnp.ones((CHUNK, CHUNK), F32))
    w["wmix_p"] = (p["w_s"] * tril).astype(BF)
    w["bmix_p"] = jnp.repeat(p["b_s"].T, gd, axis=1)
    ws_s = p["w_s"][:, :s_new, :s_new] * jnp.tril(jnp.ones((s_new, s_new), F32))
    eye = jnp.eye(CHUNK // s_new, dtype=F32)
    w["wmix_s"] = jnp.einsum("ab,gij->gaibj", eye, ws_s).reshape(SGU_GROUPS, CHUNK, CHUNK).astype(BF)
    w["bmix_s"] = jnp.repeat(jnp.tile(p["b_s"][:, :s_new].T, (CHUNK // s_new, 1)), gd, axis=1)
    w["w_ffn1"] = p["w_ffn1"].astype(BF)
    w["w_ffn2"] = p["w_ffn2"].astype(BF)
    return w


def kernel(x_prompt, x_sample, cache_mla, page_table, state_conv, state_win1, state_win2, state_win3, norm_mix, norm_ffn, norm_final, w_dq, g_q, w_uq, w_dkv, g_kv, w_uk, w_uv, w_o_mla, w_in_conv, w_conv, w_out_conv, w_qkv_c, w_o_c, rel_bias, w_uv_d, g_sgu, w_s, b_s, w_out_d, w_ffn1, w_ffn2):
    bp, sp, d = x_prompt.shape
    bs, ss, _ = x_sample.shape
    depth = norm_mix.shape[0]
    past_len = page_table.shape[1] * cache_mla.shape[1]
    params = dict(norm_mix=norm_mix, w_dq=w_dq, g_q=g_q, w_uq=w_uq, w_dkv=w_dkv, g_kv=g_kv, w_uk=w_uk,
                  w_uv=w_uv, w_o_mla=w_o_mla, w_in_conv=w_in_conv, w_out_conv=w_out_conv,
                  w_qkv_c=w_qkv_c, w_o_c=w_o_c, rel_bias=rel_bias, w_uv_d=w_uv_d, g_sgu=g_sgu,
                  w_s=w_s, b_s=b_s, w_out_d=w_out_d, w_ffn1=w_ffn1, w_ffn2=w_ffn2)
    w = _prep_weights(params, ss)
    xp = x_prompt.reshape(bp * sp, d)
    xs = x_sample.reshape(bs * ss, d)
    outs = {}
    for i in range(depth):
        gm = norm_mix[i][None]
        kind = i % 4
        if kind == 0:
            w["g_mix0"] = gm
            tm_p = 512
            cos_p, sin_p = _rope_tables(jnp.arange(sp, dtype=jnp.int32))
            ql, qp, rows_p, kc, kp = _mla_proj(xp, cos_p, sin_p, w, pos_blocks=sp // tm_p,
                                               q_dtype=BF, tm=tm_p)
            o = _mla_attn_prompt(ql, qp, kc, kp, batch=bp, seq=sp)
            xp = _mla_out(xp, o, w["w_uv_h"], w["w_o_mla"])
            pos_s = past_len + jnp.arange(ss, dtype=jnp.int32)
            cos_s, sin_s = _rope_tables(jnp.tile(pos_s, bs))
            tm_s = min(512, bs * ss)
            ql, qp, rows_s, _, _ = _mla_proj(xs, cos_s, sin_s, w, pos_blocks=bs * ss // tm_s,
                                             q_dtype=F32, tm=tm_s)
            o = _mla_attn_sample(ql.reshape(MLA_HEADS, bs, ss, KV_RANK),
                                 qp.reshape(MLA_HEADS, bs, ss, LANES),
                                 rows_s.reshape(bs, ss, -1), cache_mla, page_table)
            xs = _mla_out(xs, o.reshape(MLA_HEADS, bs * ss, KV_RANK), w["w_uv_h"], w["w_o_mla"])
            outs["mla_rows_p"] = rows_p.reshape(bp, sp, -1)
            outs["mla_rows_s"] = rows_s.reshape(bs, ss, -1)
        elif kind == 1:
            zeros_p = jnp.zeros((bp, 2, d), F32)
            xp, outs["conv_p"] = _conv_prompt(xp, zeros_p, gm, w["w_in_conv"], w_conv, w["w_out_conv"],
                                              batch=bp, seq=sp)
            xs, outs["conv_s"] = _conv_sample(xs, state_conv, gm, w["w_in_conv"], w_conv,
                                              w["w_out_conv"], batch=bs, seq=ss)
        elif kind == 2:
            q, kv0, kv1, kv2 = _dil_proj(xp, gm, w["w_qkv"], q_dtype=BF)
            kvs = (kv0, kv1, kv2)
            res = [_dil_attn_prompt(q, kvs[g], w["bias_p"][g], g=g, batch=bp, seq=sp) for g in range(3)]
            xp = _dil_comb(xp, [r[0] for r in res], [r[1] for r in res], w["w_o_c"])
            for g in range(3):
                rows = kvs[g].reshape(bp, sp, 2, HPG, HD)
                outs[f"win{g + 1}_p"] = rows[:, sp - min(WINDOWS[g], sp):]
            qs, kn0, kn1, kn2 = _dil_proj(xs, gm, w["w_qkv"], q_dtype=F32)
            kns = (kn0, kn1, kn2)
            o = _dil_attn_sample(qs, kns, (state_win1, state_win2, state_win3), w["tab_s"], w["ntab_s"],
                                 batch=bs, s_new=ss)
            xs = _proj_res(xs, o, w["w_o_c"])
            for g in range(3):
                outs[f"win{g + 1}_s"] = kns[g].reshape(bs, ss, 2, HPG, HD)
        else:
            (xp,) = _sgu(xp, gm, w["w_uv_d"], w["g_sgu"], w["wmix_p"], w["bmix_p"], w["w_out_d"],
                         emit_v=False)
            xs, v_s = _sgu(xs, gm, w["w_uv_d"], w["g_sgu"], w["wmix_s"], w["bmix_s"], w["w_out_d"],
                           emit_v=True)
            outs["sgu_v_s"] = v_s.reshape(bs, ss, -1)
        last = i == depth - 1
        gf = norm_final[None]
        xp = _ffn(xp, norm_ffn[i][None], w["w_ffn1"][i], w["w_ffn2"][i], gf, final_norm=last)
        xs = _ffn(xs, norm_ffn[i][None], w["w_ffn1"][i], w["w_ffn2"][i], gf, final_norm=last)
    return (xp.reshape(bp, sp, d), xs.reshape(bs, ss, d), outs["mla_rows_p"], outs["mla_rows_s"],
            outs["conv_p"], outs["conv_s"], outs["win1_p"], outs["win1_s"], outs["win2_p"],
            outs["win2_s"], outs["win3_p"], outs["win3_s"], outs["sgu_v_s"])
```

```python
import functools
import math

import jax
import jax.numpy as jnp
from jax import lax
from jax.experimental import pallas as pl
from jax.experimental.pallas import tpu as pltpu

EPS = 1e-6
ROPE_THETA = 10000.0
NEG = -1e30
BF = jnp.bfloat16
F32 = jnp.float32
LANES = 128
VMEM_LIMIT = 52 * 1024 * 1024

MLA_HEADS = 8
NOPE_DIM = 128
ROPE_DIM = 64
KV_RANK = 256
V_DIM = 128
WINDOWS = (128, 512, 2048)
DILATIONS = (1, 4, 16)
HPG = 4
HD = 128
N_BUCKETS = 32
MAX_DISTANCE = 2048
CHUNK = 128
SGU_GROUPS = 8
WIN_TILE = 128


def _cparams(*sem):
    return pltpu.CompilerParams(dimension_semantics=sem, vmem_limit_bytes=VMEM_LIMIT)


def _rms(x, g):
    return x * lax.rsqrt(jnp.mean(x * x, axis=-1, keepdims=True) + EPS) * g


def _dot(a, b):
    return jnp.dot(a, b, preferred_element_type=F32)


def _dot_nt(a, b):
    return lax.dot_general(a, b, (((1,), (1,)), ((), ())), preferred_element_type=F32)


def _dot_tn(a, b):
    return lax.dot_general(a, b, (((0,), (0,)), ((), ())), preferred_element_type=F32)


def _const_spec(shape):
    nd = len(shape)
    return pl.BlockSpec(shape, lambda *_: (0,) * nd)


def _ffn_kernel(x_ref, g_ref, w1_ref, w2_ref, gf_ref, o_ref, h_sc, acc_sc, *, nf, final_norm):
    f = pl.program_id(1)

    @pl.when(f == 0)
    def _():
        x = x_ref[...]
        h_sc[...] = _rms(x, g_ref[...]).astype(BF)
        acc_sc[...] = x

    a = _dot(h_sc[...], w1_ref[...])
    a = jnp.maximum(a, 0.0)
    acc_sc[...] += _dot((a * a).astype(BF), w2_ref[...])

    @pl.when(f == nf - 1)
    def _():
        y = acc_sc[...]
        if final_norm:
            y = _rms(y, gf_ref[...])
        o_ref[...] = y


def _ffn(x, g, w1, w2, gf, *, final_norm, tm=1024, tf=1024):
    t, d = x.shape
    dff = w1.shape[1]
    tm = min(tm, t)
    nf = dff // tf
    return pl.pallas_call(
        functools.partial(_ffn_kernel, nf=nf, final_norm=final_norm),
        grid=(t // tm, nf),
        in_specs=[
            pl.BlockSpec((tm, d), lambda i, f: (i, 0)),
            pl.BlockSpec((1, d), lambda i, f: (0, 0)),
            pl.BlockSpec((d, tf), lambda i, f: (0, f)),
            pl.BlockSpec((tf, d), lambda i, f: (f, 0)),
            pl.BlockSpec((1, d), lambda i, f: (0, 0)),
        ],
        out_specs=pl.BlockSpec((tm, d), lambda i, f: (i, 0)),
        out_shape=jax.ShapeDtypeStruct((t, d), F32),
        scratch_shapes=[pltpu.VMEM((tm, d), BF), pltpu.VMEM((tm, d), F32)],
        compiler_params=_cparams("parallel", "arbitrary"),
        name="ffn",
    )(x, g, w1, w2, gf)


def _proj_res_kernel(x_ref, a_ref, w_ref, o_ref):
    o_ref[...] = x_ref[...] + _dot(a_ref[...].astype(BF), w_ref[...])


def _proj_res(x, a, w, *, tm=512):
    t, d = x.shape
    k = a.shape[1]
    tm = min(tm, t)
    return pl.pallas_call(
        _proj_res_kernel,
        grid=(t // tm,),
        in_specs=[pl.BlockSpec((tm, d), lambda i: (i, 0)),
                  pl.BlockSpec((tm, k), lambda i: (i, 0)),
                  _const_spec(w.shape)],
        out_specs=pl.BlockSpec((tm, d), lambda i: (i, 0)),
        out_shape=jax.ShapeDtypeStruct((t, d), F32),
        compiler_params=_cparams("parallel"),
        name="proj_res",
    )(x, a, w)


def _mla_proj_kernel(x_ref, cos_ref, sin_ref, gm_ref, wdq_ref, gq_ref, wqn_ref, wqr_ref, wqt_ref,
                     wuk_ref, wkv_ref, wkt_ref, gkv_ref,
                     ql_ref, qp_ref, rows_ref, kc_ref, kp_ref, *, scale):
    h = _rms(x_ref[...], gm_ref[...]).astype(BF)
    cq = _rms(_dot(h, wdq_ref[...]), gq_ref[...]).astype(BF)
    cos = cos_ref[...]
    sin = sin_ref[...]
    q_nope = _dot(cq, wqn_ref[...])
    q_rope = _dot(cq, wqr_ref[...])
    q_rot = _dot(cq, wqt_ref[...])
    for hh in range(MLA_HEADS):
        sl = slice(hh * LANES, (hh + 1) * LANES)
        ql = _dot(q_nope[:, sl].astype(BF), wuk_ref[hh])
        ql_ref[hh] = (ql * scale).astype(ql_ref.dtype)
        qp = q_rope[:, sl] * cos + q_rot[:, sl] * sin
        qp_ref[hh] = (qp * scale).astype(qp_ref.dtype)
    ckv = _dot(h, wkv_ref[...])
    k_rot = _dot(h, wkt_ref[...])
    c = _rms(ckv[:, :KV_RANK], gkv_ref[...])
    kpe = ckv[:, KV_RANK:] * cos + k_rot * sin
    rows_ref[:, :KV_RANK] = c
    rows_ref[:, KV_RANK:] = kpe[:, :ROPE_DIM]
    kc_ref[...] = c.astype(BF)
    kp_ref[...] = kpe.astype(BF)


def _mla_proj(x, cos, sin, w, *, pos_blocks, q_dtype, tm=512):
    t, d = x.shape
    tm = min(tm, t)
    scale = 1.0 / math.sqrt(NOPE_DIM + ROPE_DIM)
    row = lambda i: (i, 0)
    consts = [w["g_mix0"], w["w_dq"], w["g_q"], w["w_uq_nope"], w["w_uq_rope"], w["w_uq_rot"],
              w["w_uk_t"], w["w_dkv"], w["w_dkv_rot"], w["g_kv"]]
    return pl.pallas_call(
        functools.partial(_mla_proj_kernel, scale=scale),
        grid=(t // tm,),
        in_specs=[pl.BlockSpec((tm, d), row),
                  pl.BlockSpec((tm, LANES), lambda i: (i % pos_blocks, 0)),
                  pl.BlockSpec((tm, LANES), lambda i: (i % pos_blocks, 0))]
                 + [_const_spec(c.shape) for c in consts],
        out_specs=[pl.BlockSpec((MLA_HEADS, tm, KV_RANK), lambda i: (0, i, 0)),
                   pl.BlockSpec((MLA_HEADS, tm, LANES), lambda i: (0, i, 0)),
                   pl.BlockSpec((tm, KV_RANK + ROPE_DIM), row),
                   pl.BlockSpec((tm, KV_RANK), row),
                   pl.BlockSpec((tm, LANES), row)],
        out_shape=[jax.ShapeDtypeStruct((MLA_HEADS, t, KV_RANK), q_dtype),
                   jax.ShapeDtypeStruct((MLA_HEADS, t, LANES), q_dtype),
                   jax.ShapeDtypeStruct((t, KV_RANK + ROPE_DIM), F32),
                   jax.ShapeDtypeStruct((t, KV_RANK), BF),
                   jax.ShapeDtypeStruct((t, LANES), BF)],
        compiler_params=_cparams("parallel"),
        name="mla_proj",
    )(x, cos, sin, *consts)


def _lane_rep(x, n):
    return x if n == LANES else jnp.concatenate([x] * (n // LANES), axis=1)


def _softmax_update(s, pv, m_ref, l_ref, acc_ref):
    m_old = m_ref[...]
    m_new = jnp.maximum(m_old, jnp.max(s, axis=-1, keepdims=True))
    alpha = jnp.exp(m_old - m_new)
    p = jnp.exp(s - _lane_rep(m_new, s.shape[1]))
    l_ref[...] = alpha * l_ref[...] + jnp.sum(p, axis=-1, keepdims=True)
    acc_ref[...] = _lane_rep(alpha, acc_ref.shape[1]) * acc_ref[...] + pv(p.astype(BF))
    m_ref[...] = m_new


def _mla_attn_p_kernel(ql_ref, qp_ref, kc_ref, kp_ref, o_ref, m_sc, l_sc, acc_sc, *, tq, tk, hc):
    i = pl.program_id(1)
    m_sc[...] = jnp.full(m_sc.shape, NEG, F32)
    l_sc[...] = jnp.zeros(l_sc.shape, F32)
    acc_sc[...] = jnp.zeros(acc_sc.shape, F32)
    n_full = (i * tq) // tk
    rc = hc * tq

    def tile(j, masked):
        start = pl.multiple_of(j * tk, tk)
        kc = kc_ref[pl.ds(start, tk), :]
        kp = kp_ref[pl.ds(start, tk), :]
        for c in range(MLA_HEADS // hc):
            ql = ql_ref[c * hc:(c + 1) * hc].reshape(rc, KV_RANK)
            qp = qp_ref[c * hc:(c + 1) * hc].reshape(rc, LANES)
            s = _dot_nt(ql, kc) + _dot_nt(qp, kp)
            if masked:
                q_pos = i * tq + lax.broadcasted_iota(jnp.int32, s.shape, 0) % tq
                k_pos = j * tk + lax.broadcasted_iota(jnp.int32, s.shape, 1)
                s = jnp.where(k_pos <= q_pos, s, NEG)
            rs = pl.ds(c * rc, rc)
            _softmax_update(s, lambda p: _dot(p, kc), m_sc.at[rs], l_sc.at[rs], acc_sc.at[rs])

    def body(j, carry):
        tile(j, False)
        return carry

    lax.fori_loop(0, n_full, body, 0)
    tile(n_full, True)
    o = acc_sc[...] / _lane_rep(l_sc[...], KV_RANK)
    o_ref[...] = o.reshape(MLA_HEADS, tq, KV_RANK).astype(o_ref.dtype)


def _mla_attn_prompt(ql, qp, kc, kp, *, batch, seq, tq=256, tk=512, hc=2):
    t = batch * seq
    nq = seq // tq
    rows = MLA_HEADS * tq
    qmap = lambda b, i: (0, b * nq + i, 0)
    return pl.pallas_call(
        functools.partial(_mla_attn_p_kernel, tq=tq, tk=tk, hc=hc),
        grid=(batch, nq),
        in_specs=[pl.BlockSpec((MLA_HEADS, tq, KV_RANK), qmap),
                  pl.BlockSpec((MLA_HEADS, tq, LANES), qmap),
                  pl.BlockSpec((seq, KV_RANK), lambda b, i: (b, 0)),
                  pl.BlockSpec((seq, LANES), lambda b, i: (b, 0))],
        out_specs=pl.BlockSpec((MLA_HEADS, tq, KV_RANK), qmap),
        out_shape=jax.ShapeDtypeStruct((MLA_HEADS, t, KV_RANK), BF),
        scratch_shapes=[pltpu.VMEM((rows, LANES), F32), pltpu.VMEM((rows, LANES), F32),
                        pltpu.VMEM((rows, KV_RANK), F32)],
        compiler_params=_cparams("parallel", "arbitrary"),
        name="mla_attn_prompt",
    )(ql, qp, kc, kp)


def _mla_attn_s_kernel(pt_ref, ql_ref, qp_ref, new_ref, *rest, pages, n_steps, page_size, s_new):
    del pt_ref
    cache_refs = rest[:pages]
    o_ref = rest[pages]
    m_sc, l_sc, acc_sc, kc_sc, kp_sc = rest[pages + 1:]
    step = pl.program_id(1)
    rows = MLA_HEADS * s_new
    ql = ql_ref[...].reshape(rows, KV_RANK).astype(BF)
    qp = qp_ref[...].reshape(rows, LANES).astype(BF)

    @pl.when(step == 0)
    def _():
        m_sc[...] = jnp.full(m_sc.shape, NEG, F32)
        l_sc[...] = jnp.zeros(l_sc.shape, F32)
        acc_sc[...] = jnp.zeros(acc_sc.shape, F32)

        kp_sc[ROPE_DIM:, :] = jnp.zeros((LANES - ROPE_DIM, kp_sc.shape[1]), BF)

    for k in range(pages):
        kb = cache_refs[k][...]
        sl = slice(k * page_size, (k + 1) * page_size)
        kc_sc[:, sl] = kb[:KV_RANK].astype(BF)
        kp_sc[:ROPE_DIM, sl] = kb[KV_RANK:].astype(BF)
    kc_t = kc_sc[...]
    s = _dot(ql, kc_t) + _dot(qp, kp_sc[...])
    _softmax_update(s, lambda p: _dot_nt(p, kc_t), m_sc, l_sc, acc_sc)

    @pl.when(step == n_steps - 1)
    def _():
        nr = new_ref[...]
        pad_r = jnp.zeros((LANES - s_new, KV_RANK + ROPE_DIM), F32)
        nb = jnp.concatenate([nr, pad_r], axis=0)
        nc = nb[:, :KV_RANK].astype(BF)
        npe = jnp.concatenate([nb[:, KV_RANK:], jnp.zeros((LANES, LANES - ROPE_DIM), F32)],
                              axis=1).astype(BF)
        sn = _dot_nt(ql, nc) + _dot_nt(qp, npe)
        q_pos = lax.broadcasted_iota(jnp.int32, sn.shape, 0) % s_new
        k_pos = lax.broadcasted_iota(jnp.int32, sn.shape, 1)
        sn = jnp.where(k_pos <= q_pos, sn, NEG)
        _softmax_update(sn, lambda p: _dot(p, nc), m_sc, l_sc, acc_sc)
        o = acc_sc[...] / _lane_rep(l_sc[...], KV_RANK)
        o_ref[...] = o.reshape(MLA_HEADS, s_new, KV_RANK)


def _mla_attn_sample(ql, qp, rows_new, cache, page_table, *, pages=64):
    _, bs, s_new, _ = ql.shape
    n_pages = page_table.shape[1]
    page_size = cache.shape[1]
    row_w = cache.shape[2]
    pages = min(pages, n_pages)
    n_steps = n_pages // pages
    rows = MLA_HEADS * s_new
    pt = page_table.reshape(-1)
    qmap = lambda b, s, pt: (0, b, 0, 0)
    cache_t = jnp.transpose(cache, (0, 2, 1))

    def cache_spec(k):
        return pl.BlockSpec((None, row_w, page_size),
                            lambda b, s, pt: (pt[b * n_pages + s * pages + k], 0, 0))

    grid_spec = pltpu.PrefetchScalarGridSpec(
        num_scalar_prefetch=1,
        grid=(bs, n_steps),
        in_specs=[pl.BlockSpec((MLA_HEADS, None, s_new, KV_RANK), qmap),
                  pl.BlockSpec((MLA_HEADS, None, s_new, LANES), qmap),
                  pl.BlockSpec((None, s_new, row_w), lambda b, s, pt: (b, 0, 0))]
                 + [cache_spec(k) for k in range(pages)],
        out_specs=pl.BlockSpec((MLA_HEADS, None, s_new, KV_RANK), qmap),
        scratch_shapes=[pltpu.VMEM((rows, LANES), F32), pltpu.VMEM((rows, LANES), F32),
                        pltpu.VMEM((rows, KV_RANK), F32),
                        pltpu.VMEM((KV_RANK, pages * page_size), BF),
                        pltpu.VMEM((LANES, pages * page_size), BF)],
    )
    return pl.pallas_call(
        functools.partial(_mla_attn_s_kernel, pages=pages, n_steps=n_steps,
                          page_size=page_size, s_new=s_new),
        grid_spec=grid_spec,
        out_shape=jax.ShapeDtypeStruct((MLA_HEADS, bs, s_new, KV_RANK), F32),
        compiler_params=_cparams("parallel", "arbitrary"),
        name="mla_attn_sample",
    )(pt, ql, qp, rows_new, *([cache_t] * pages))


def _mla_out_kernel(x_ref, o_ref, wuv_ref, wo_ref, y_ref):
    vs = [_dot(o_ref[hh].astype(BF), wuv_ref[hh]).astype(BF) for hh in range(MLA_HEADS)]
    v = jnp.concatenate(vs, axis=1)
    y_ref[...] = x_ref[...] + _dot(v, wo_ref[...])


def _mla_out(x, o, w_uv_h, w_o, *, tm=512):
    t, d = x.shape
    tm = min(tm, t)
    return pl.pallas_call(
        _mla_out_kernel,
        grid=(t // tm,),
        in_specs=[pl.BlockSpec((tm, d), lambda i: (i, 0)),
                  pl.BlockSpec((MLA_HEADS, tm, KV_RANK), lambda i: (0, i, 0)),
                  _const_spec(w_uv_h.shape), _const_spec(w_o.shape)],
        out_specs=pl.BlockSpec((tm, d), lambda i: (i, 0)),
        out_shape=jax.ShapeDtypeStruct((t, d), F32),
        compiler_params=_cparams("parallel"),
        name="mla_out",
    )(x, o, w_uv_h, w_o)


def _conv_kernel(x_ref, st_ref, gm_ref, win_ref, wc_ref, wout_ref, y_ref, cs_ref, carry_sc,
                 *, tiles_per_seq, seq_rows):
    x = x_ref[...]
    tm, d = x.shape
    h = _rms(x, gm_ref[...]).astype(BF)
    bch = _dot(h, win_ref[...])
    gate_b = bch[:, :d]
    z = bch[:, d:2 * d] * bch[:, 2 * d:]
    z1 = pltpu.roll(z, 1, axis=0)
    z2 = pltpu.roll(z, 2, axis=0)
    row = lax.broadcasted_iota(jnp.int32, z.shape, 0)
    if tiles_per_seq is not None:
        i = pl.program_id(0)
        first = (i % tiles_per_seq) == 0
        st = st_ref[...]
        prev2 = jnp.where(first, st[0:1], carry_sc[6:7])
        prev1 = jnp.where(first, st[1:2], carry_sc[7:8])
        z1 = jnp.where(row == 0, prev1, z1)
        z2 = jnp.where(row == 0, prev2, jnp.where(row == 1, prev1, z2))
        carry_sc[...] = z[tm - 8:tm]
        cs_ref[...] = z[tm - 2:tm]
    else:
        nb = tm // seq_rows
        st = st_ref[...]
        j = lax.broadcasted_iota(jnp.int32, (nb, seq_rows, d), 1)
        z3 = z.reshape(nb, seq_rows, d)
        p1 = st[:, 1:2, :]
        p2 = st[:, 0:1, :]
        z1 = jnp.where(j == 0, p1, z1.reshape(nb, seq_rows, d)).reshape(tm, d)
        z2 = jnp.where(j == 0, p2, jnp.where(j == 1, p1, z2.reshape(nb, seq_rows, d))).reshape(tm, d)
        cs_ref[...] = z3[:, seq_rows - 2:, :]
    wc = wc_ref[...]
    y = wc[0:1] * z2 + wc[1:2] * z1 + wc[2:3] * z
    y_ref[...] = x + _dot((gate_b * y).astype(BF), wout_ref[...])


def _conv_prompt(x, state, gm, w_in, w_conv, w_out, *, batch, seq, tm=512):
    t, d = x.shape
    tps = seq // tm
    return pl.pallas_call(
        functools.partial(_conv_kernel, tiles_per_seq=tps, seq_rows=seq),
        grid=(t // tm,),
        in_specs=[pl.BlockSpec((tm, d), lambda i: (i, 0)),
                  pl.BlockSpec((None, 2, d), lambda i: (i // tps, 0, 0)),
                  _const_spec(gm.shape), _const_spec(w_in.shape), _const_spec(w_conv.shape),
                  _const_spec(w_out.shape)],
        out_specs=[pl.BlockSpec((tm, d), lambda i: (i, 0)),
                   pl.BlockSpec((None, 2, d), lambda i: (i // tps, 0, 0))],
        out_shape=[jax.ShapeDtypeStruct((t, d), F32), jax.ShapeDtypeStruct((batch, 2, d), F32)],
        scratch_shapes=[pltpu.VMEM((8, d), F32)],
        compiler_params=_cparams("arbitrary"),
        name="conv_prompt",
    )(x, state, gm, w_in, w_conv, w_out)


def _conv_sample(x, state, gm, w_in, w_conv, w_out, *, batch, seq, tm=512):
    t, d = x.shape
    tm = min(tm, t)
    nb = tm // seq
    return pl.pallas_call(
        functools.partial(_conv_kernel, tiles_per_seq=None, seq_rows=seq),
        grid=(t // tm,),
        in_specs=[pl.BlockSpec((tm, d), lambda i: (i, 0)),
                  pl.BlockSpec((nb, 2, d), lambda i: (i, 0, 0)),
                  _const_spec(gm.shape), _const_spec(w_in.shape), _const_spec(w_conv.shape),
                  _const_spec(w_out.shape)],
        out_specs=[pl.BlockSpec((tm, d), lambda i: (i, 0)),
                   pl.BlockSpec((nb, 2, d), lambda i: (i, 0, 0))],
        out_shape=[jax.ShapeDtypeStruct((t, d), F32), jax.ShapeDtypeStruct((batch, 2, d), F32)],
        scratch_shapes=[pltpu.VMEM((8, d), F32)],
        compiler_params=_cparams("parallel"),
        name="conv_sample",
    )(x, state, gm, w_in, w_conv, w_out)


GW = HPG * HD


def _dil_proj_kernel(x_ref, gm_ref, w_ref, q_ref, kv0_ref, kv1_ref, kv2_ref, *, scale):
    h = _rms(x_ref[...], gm_ref[...]).astype(BF)
    nq = 3 * GW
    q_ref[...] = (_dot(h, w_ref[:, :nq]) * scale).astype(q_ref.dtype)
    for g, kv_ref in enumerate((kv0_ref, kv1_ref, kv2_ref)):
        kv_ref[...] = _dot(h, w_ref[:, nq + g * 2 * GW: nq + (g + 1) * 2 * GW])


def _dil_proj(x, gm, w_qkv, *, q_dtype, tm=256):
    t, d = x.shape
    tm = min(tm, t)
    row = lambda i: (i, 0)
    return pl.pallas_call(
        functools.partial(_dil_proj_kernel, scale=1.0 / math.sqrt(HD)),
        grid=(t // tm,),
        in_specs=[pl.BlockSpec((tm, d), row), _const_spec(gm.shape), _const_spec(w_qkv.shape)],
        out_specs=[pl.BlockSpec((tm, 3 * GW), row)] + [pl.BlockSpec((tm, 2 * GW), row)] * 3,
        out_shape=[jax.ShapeDtypeStruct((t, 3 * GW), q_dtype)]
                  + [jax.ShapeDtypeStruct((t, 2 * GW), F32)] * 3,
        compiler_params=_cparams("parallel"),
        name="dil_proj",
    )(x, gm, w_qkv)


def _dil_proj_p_kernel(x_ref, gm_ref, w_ref, q0_ref, q1_ref, q2_ref, kv0_ref, kv1_ref, kv2_ref, kvt_ref,
                       *, scale):
    tm = x_ref.shape[0]
    h = _rms(x_ref[...], gm_ref[...]).astype(BF)
    nq = 3 * GW
    q = _dot(h, w_ref[:, :nq]) * scale
    for g, q_ref in enumerate((q0_ref, q1_ref, q2_ref)):
        for s in range(HPG):
            q_ref[s] = q[:, g * GW + s * HD: g * GW + (s + 1) * HD]
    for g, kv_ref in enumerate((kv0_ref, kv1_ref, kv2_ref)):
        kv = _dot(h, w_ref[:, nq + g * 2 * GW: nq + (g + 1) * 2 * GW])
        for s in range(2 * HPG):
            kv_ref[s] = kv[:, s * HD:(s + 1) * HD]
            if g == 2:
                kvt_ref[pl.ds(s, tm, stride=2 * HPG), :] = kv[:, s * HD:(s + 1) * HD]


def _dil_proj_prompt(x, gm, w_qkv, *, batch, seq, tm=256):
    t, d = x.shape
    tps = seq // tm
    slab = lambda i: (i // tps, 0, i % tps, 0)
    return pl.pallas_call(
        functools.partial(_dil_proj_p_kernel, scale=1.0 / math.sqrt(HD)),
        grid=(t // tm,),
        in_specs=[pl.BlockSpec((tm, d), lambda i: (i, 0)), _const_spec(gm.shape), _const_spec(w_qkv.shape)],
        out_specs=[pl.BlockSpec((None, HPG, tm, HD), slab)] * 3
                  + [pl.BlockSpec((None, 2 * HPG, tm, HD), slab)] * 3
                  + [pl.BlockSpec((tm * 2 * HPG, HD), lambda i: (i, 0))],
        out_shape=[jax.ShapeDtypeStruct((batch, HPG, seq, HD), F32)] * 3
                  + [jax.ShapeDtypeStruct((batch, 2 * HPG, seq, HD), F32)] * 3
                  + [jax.ShapeDtypeStruct((t * 2 * HPG, HD), F32)],
        compiler_params=_cparams("parallel"),
        name="dil_proj_prompt",
    )(x, gm, w_qkv)


def _dil_attn_p_kernel(q_ref, kv_ref, bias_ref, o_ref, lse_ref, *, dil, n_i):
    lane_grp = lax.broadcasted_iota(jnp.int32, (WIN_TILE, LANES), 1) // (LANES // HPG)
    col = lax.broadcasted_iota(jnp.int32, (WIN_TILE, 2 * WIN_TILE), 1)
    span = WIN_TILE * dil

    def rows_at(start):
        return pl.ds(start, WIN_TILE) if dil == 1 else pl.ds(start, WIN_TILE, stride=dil)

    def unit(u, carry):
        r = u // n_i
        i = u % n_i
        start = r + i * span
        cur = rows_at(start)
        prev = rows_at(jnp.maximum(start - span, r))
        prev_mask = jnp.where((col < WIN_TILE) & (i == 0), NEG, 0.0)
        lse_tile = jnp.zeros((WIN_TILE, LANES), F32)
        for hh in range(HPG):
            q = q_ref[hh, cur, :].astype(BF)
            k = jnp.concatenate([kv_ref[hh, prev, :], kv_ref[hh, cur, :]], axis=0).astype(BF)
            v = jnp.concatenate([kv_ref[HPG + hh, prev, :], kv_ref[HPG + hh, cur, :]], axis=0).astype(BF)
            s = _dot_nt(q, k) + bias_ref[hh] + prev_mask
            m = jnp.max(s, axis=-1, keepdims=True)
            e = jnp.exp(s - m)
            l = jnp.sum(e, axis=-1, keepdims=True)
            p = (e / l).astype(BF)
            o_ref[hh, cur, :] = _dot(p, v)
            lse_tile = jnp.where(lane_grp == hh, m + jnp.log(l), lse_tile)
        lse_ref[cur, :] = lse_tile
        return carry

    lax.fori_loop(0, dil * n_i, unit, 0, unroll=4)


def _dil_attn_prompt(q, kv, bias, *, g, batch, seq):
    dil = DILATIONS[g]
    n_i = seq // dil // WIN_TILE
    return pl.pallas_call(
        functools.partial(_dil_attn_p_kernel, dil=dil, n_i=n_i),
        grid=(batch,),
        in_specs=[pl.BlockSpec((None, HPG, seq, HD), lambda b: (b, 0, 0, 0)),
                  pl.BlockSpec((None, 2 * HPG, seq, HD), lambda b: (b, 0, 0, 0)),
                  _const_spec(bias.shape)],
        out_specs=[pl.BlockSpec((None, HPG, seq, HD), lambda b: (b, 0, 0, 0)),
                   pl.BlockSpec((None, seq, LANES), lambda b: (b, 0, 0))],
        out_shape=[jax.ShapeDtypeStruct((batch, HPG, seq, HD), F32),
                   jax.ShapeDtypeStruct((batch, seq, LANES), F32)],
        compiler_params=_cparams("parallel"),
        name=f"dil_attn_prompt_g{g}",
    )(q, kv, bias)


def _dil_comb_kernel(x_ref, o0_ref, o1_ref, o2_ref, l0_ref, l1_ref, l2_ref, w_ref, y_ref):
    l0, l1, l2 = l0_ref[...], l1_ref[...], l2_ref[...]
    m = jnp.maximum(jnp.maximum(l0, l1), l2)
    e0, e1, e2 = jnp.exp(l0 - m), jnp.exp(l1 - m), jnp.exp(l2 - m)
    den = e0 + e1 + e2
    ws = (e0 / den, e1 / den, e2 / den)
    tm = x_ref.shape[0]
    parts = []
    for hh in range(HPG):
        acc = jnp.zeros((tm, HD), F32)
        for wg, o_ref in zip(ws, (o0_ref, o1_ref, o2_ref)):
            wcol = wg[:, hh * (LANES // HPG): hh * (LANES // HPG) + 1]
            acc = acc + wcol * o_ref[hh]
        parts.append(acc.astype(BF))
    y_ref[...] = x_ref[...] + _dot(jnp.concatenate(parts, axis=1), w_ref[...])


def _dil_comb(x, os_, lses, w_o, *, batch, seq, tm=512):
    t, d = x.shape
    tps = seq // tm
    row = lambda i: (i, 0)
    return pl.pallas_call(
        _dil_comb_kernel,
        grid=(t // tm,),
        in_specs=[pl.BlockSpec((tm, d), row)]
                 + [pl.BlockSpec((None, HPG, tm, HD), lambda i: (i // tps, 0, i % tps, 0))] * 3
                 + [pl.BlockSpec((None, tm, LANES), lambda i: (i // tps, i % tps, 0))] * 3
                 + [_const_spec(w_o.shape)],
        out_specs=pl.BlockSpec((tm, d), row),
        out_shape=jax.ShapeDtypeStruct((t, d), F32),
        compiler_params=_cparams("parallel"),
        name="dil_comb",
    )(x, *os_, *lses, w_o)


def _dil_attn_s_kernel(q_ref, kn0_ref, kn1_ref, kn2_ref, b0_ref, b1_ref, b2_ref,
                       t0_ref, t1_ref, t2_ref, n0_ref, n1_ref, n2_ref, o_ref, *, s_new):
    q = q_ref[...]
    rows_w = lax.broadcasted_iota(jnp.int32, (LANES, GW), 0)
    lanes_w = lax.broadcasted_iota(jnp.int32, (LANES, GW), 1)
    head_sel = (lanes_w // HD) == (rows_w // s_new)
    logits = []
    vals = []
    for g, (kn_ref, b_ref, t_ref, n_ref) in enumerate((
            (kn0_ref, b0_ref, t0_ref, n0_ref), (kn1_ref, b1_ref, t1_ref, n1_ref),
            (kn2_ref, b2_ref, t2_ref, n2_ref))):
        qg = q[:, g * GW:(g + 1) * GW]
        wq = jnp.where(head_sel, jnp.tile(qg, (LANES // s_new, 1)), 0.0).astype(BF)
        n_rows = b_ref.shape[0] // (2 * HPG)
        buf_k = jnp.concatenate([b_ref[pl.ds(s, n_rows, stride=2 * HPG), :] for s in range(HPG)], axis=1)
        buf_v = jnp.concatenate([b_ref[pl.ds(HPG + s, n_rows, stride=2 * HPG), :] for s in range(HPG)],
                                axis=1)
        new = jnp.concatenate([kn_ref[...], jnp.zeros((LANES - s_new, 2 * GW), F32)], axis=0)
        logits.append(_dot_nt(buf_k.astype(BF), wq) + t_ref[...])
        vals.append(buf_v.astype(BF))
        logits.append(_dot_nt(new[:, :GW].astype(BF), wq) + n_ref[...])
        vals.append(new[:, GW:].astype(BF))
    m = logits[0].max(axis=0, keepdims=True)
    for lg in logits[1:]:
        m = jnp.maximum(m, lg.max(axis=0, keepdims=True))
    ps = [jnp.exp(lg - m) for lg in logits]
    den = ps[0].sum(axis=0, keepdims=True)
    for p in ps[1:]:
        den = den + p.sum(axis=0, keepdims=True)
    inv = 1.0 / den
    out = jnp.zeros((LANES, GW), F32)
    for p, v in zip(ps, vals):
        out = out + _dot_tn((p * inv).astype(BF), v)
    for hh in range(HPG):
        o_ref[:, hh * HD:(hh + 1) * HD] = out[hh * s_new:(hh + 1) * s_new, hh * HD:(hh + 1) * HD]


def _dil_attn_sample(q, kv_new, bufs, tabs, ntabs, *, batch, s_new):
    t = batch * s_new
    bufs2 = [b.reshape(batch, b.shape[1] * 2 * HPG, HD) for b in bufs]
    return pl.pallas_call(
        functools.partial(_dil_attn_s_kernel, s_new=s_new),
        grid=(batch,),
        in_specs=[pl.BlockSpec((s_new, 3 * GW), lambda b: (b, 0))]
                 + [pl.BlockSpec((s_new, 2 * GW), lambda b: (b, 0))] * 3
                 + [pl.BlockSpec((None, bb.shape[1], HD), lambda b: (b, 0, 0)) for bb in bufs2]
                 + [_const_spec(tb.shape) for tb in tabs]
                 + [_const_spec(tb.shape) for tb in ntabs],
        out_specs=pl.BlockSpec((s_new, GW), lambda b: (b, 0)),
        out_shape=jax.ShapeDtypeStruct((t, GW), F32),
        compiler_params=_cparams("parallel"),
        name="dil_attn_sample",
    )(q, *kv_new, *bufs2, *tabs, *ntabs)


def _sgu_kernel(x_ref, gm_ref, wuv_ref, gs_ref, wmix_ref, bmix_ref, wout_ref, y_ref, *rest, emit_v):
    if emit_v:
        v_ref, mixed_sc = rest
    else:
        (mixed_sc,) = rest
    x = x_ref[...]
    tm, d = x.shape
    h = _rms(x, gm_ref[...]).astype(BF)
    uv = jax.nn.gelu(_dot(h, wuv_ref[...]))
    w = uv.shape[1] // 2
    u = uv[:, :w]
    v = _rms(uv[:, w:], gs_ref[...])
    if emit_v:
        v_ref[...] = v
    vb = v.astype(BF)
    gd = w // SGU_GROUPS
    for c in range(tm // CHUNK):
        rs = slice(c * CHUNK, (c + 1) * CHUNK)
        for g in range(SGU_GROUPS):
            cs = slice(g * gd, (g + 1) * gd)
            mixed_sc[rs, cs] = _dot(wmix_ref[g], vb[rs, cs]) + bmix_ref[:, cs]
    y_ref[...] = x + _dot((u * mixed_sc[...]).astype(BF), wout_ref[...])


def _sgu(x, gm, w_uv, g_sgu, wmix, bmix, w_out, *, emit_v, tm=512):
    t, d = x.shape
    tm = min(tm, t)
    w = w_uv.shape[1] // 2
    row = lambda i: (i, 0)
    out_specs = [pl.BlockSpec((tm, d), row)]
    out_shape = [jax.ShapeDtypeStruct((t, d), F32)]
    if emit_v:
        out_specs.append(pl.BlockSpec((tm, w), row))
        out_shape.append(jax.ShapeDtypeStruct((t, w), F32))
    consts = [gm, w_uv, g_sgu, wmix, bmix, w_out]
    return pl.pallas_call(
        functools.partial(_sgu_kernel, emit_v=emit_v),
        grid=(t // tm,),
        in_specs=[pl.BlockSpec((tm, d), row)] + [_const_spec(c.shape) for c in consts],
        out_specs=out_specs,
        out_shape=out_shape,
        scratch_shapes=[pltpu.VMEM((tm, w), F32)],
        compiler_params=_cparams("parallel"),
        name="sgu",
    )(x, *consts)


def _rot_half_cols(w):
    half = w.shape[-1] // 2
    return jnp.concatenate([-w[..., half:], w[..., :half]], axis=-1)


def _pad_last(w, n):
    return jnp.pad(w, [(0, 0)] * (w.ndim - 1) + [(0, n - w.shape[-1])])


def _rope_tables(pos):
    half = ROPE_DIM // 2
    inv = ROPE_THETA ** (-jnp.arange(half, dtype=F32) / half)
    ang = pos.astype(F32)[:, None] * inv[None, :]
    cos, sin = jnp.cos(ang), jnp.sin(ang)
    return (_pad_last(jnp.concatenate([cos, cos], axis=1), LANES),
            _pad_last(jnp.concatenate([sin, sin], axis=1), LANES))


def _t5_bucket(dist):
    max_exact = N_BUCKETS // 2
    n = jnp.maximum(dist, 1).astype(F32)
    large = max_exact + (jnp.log(n / max_exact) / math.log(MAX_DISTANCE / max_exact)
                         * (N_BUCKETS - max_exact)).astype(jnp.int32)
    return jnp.where(dist < max_exact, dist, jnp.minimum(large, N_BUCKETS - 1))


def _group_bias(rel_bias, g):
    n_keys = WINDOWS[g] // DILATIONS[g] + 1
    dist = DILATIONS[g] * jnp.arange(n_keys, dtype=jnp.int32)
    return rel_bias[_t5_bucket(dist)][:, g * HPG:(g + 1) * HPG].T


def _prompt_bias_table(bias_g):
    period = 3 * WIN_TILE
    line = jnp.concatenate([bias_g[:, ::-1], jnp.full((HPG, period - WIN_TILE - 1), NEG, F32)], axis=1)
    flat = jnp.tile(line, (1, WIN_TILE))[:, :WIN_TILE * (period - 1)]
    return flat.reshape(HPG, WIN_TILE, period - 1)[:, :, :2 * WIN_TILE]


def _sample_bias_tables(bias_g, g, s_new):
    win, dil = WINDOWS[g], DILATIONS[g]
    n_keys = win // dil + 1
    line = jnp.concatenate([bias_g[:, :, None], jnp.full((HPG, n_keys, dil - 1), NEG, F32)], axis=2)
    line = line.reshape(HPG, n_keys * dil)
    line = jnp.concatenate([line[:, :win + 1], jnp.full((HPG, s_new), NEG, F32)], axis=1)
    buf = jnp.stack([line[:, j + 1: win + j + 1][:, ::-1] for j in range(s_new)])
    buf = jnp.transpose(buf, (2, 1, 0)).reshape(win, HPG * s_new)
    lpad = jnp.concatenate([jnp.full((HPG, s_new - 1), NEG, F32), line[:, :s_new]], axis=1)
    new = jnp.stack([lpad[:, j: j + s_new][:, ::-1] for j in range(s_new)])
    new = jnp.transpose(new, (2, 1, 0)).reshape(s_new, HPG * s_new)
    pad = lambda t, r: jnp.pad(t, ((0, r - t.shape[0]), (0, LANES - t.shape[1])), constant_values=NEG)
    return pad(buf, win), pad(new, LANES)


def _prep_weights(p, s_new):
    w = {}
    w["g_mix0"] = p["norm_mix"][0][None]
    w["w_dq"] = p["w_dq"].astype(BF)
    w["g_q"] = p["g_q"][None]
    q_rank = p["w_uq"].shape[0]
    wuq = p["w_uq"].reshape(q_rank, MLA_HEADS, NOPE_DIM + ROPE_DIM)
    w["w_uq_nope"] = wuq[:, :, :NOPE_DIM].reshape(q_rank, -1).astype(BF)
    wr = wuq[:, :, NOPE_DIM:]
    w["w_uq_rope"] = _pad_last(wr, LANES).reshape(q_rank, -1).astype(BF)
    w["w_uq_rot"] = _pad_last(_rot_half_cols(wr), LANES).reshape(q_rank, -1).astype(BF)
    w["w_uk_t"] = jnp.transpose(p["w_uk"], (1, 2, 0)).astype(BF)
    w["w_dkv"] = _pad_last(p["w_dkv"], KV_RANK + LANES).astype(BF)
    w["w_dkv_rot"] = _pad_last(_rot_half_cols(p["w_dkv"][:, KV_RANK:]), LANES).astype(BF)
    w["g_kv"] = p["g_kv"][None]
    w["w_uv_h"] = jnp.transpose(p["w_uv"], (1, 0, 2)).astype(BF)
    w["w_o_mla"] = p["w_o_mla"].astype(BF)
    w["w_in_conv"] = p["w_in_conv"].astype(BF)
    w["w_out_conv"] = p["w_out_conv"].astype(BF)
    d = p["w_qkv_c"].shape[0]
    wqkv = p["w_qkv_c"].reshape(d, 3, 3, GW)
    cols = [wqkv[:, 0].reshape(d, 3 * GW)]
    for g in range(3):
        cols += [wqkv[:, 1, g], wqkv[:, 2, g]]
    w["w_qkv"] = jnp.concatenate(cols, axis=1).astype(BF)
    w["w_o_c"] = p["w_o_c"].astype(BF)
    biases = [_group_bias(p["rel_bias"], g) for g in range(3)]
    w["bias_p"] = [_prompt_bias_table(b) for b in biases]
    tabs = [_sample_bias_tables(b, g, s_new) for g, b in enumerate(biases)]
    w["tab_s"] = [tb[0] for tb in tabs]
    w["ntab_s"] = [tb[1] for tb in tabs]
    w["w_uv_d"] = p["w_uv_d"].astype(BF)
    w["g_sgu"] = p["g_sgu"][None]
    w["w_out_d"] = p["w_out_d"].astype(BF)
    gd = p["w_uv_d"].shape[1] // 2 // SGU_GROUPS
    tril = jnp.tril(jnp.ones((CHUNK, CHUNK), F32))
    w["wmix_p"] = (p["w_s"] * tril).astype(BF)
    w["bmix_p"] = jnp.repeat(p["b_s"].T, gd, axis=1)
    ws_s = p["w_s"][:, :s_new, :s_new] * jnp.tril(jnp.ones((s_new, s_new), F32))
    eye = jnp.eye(CHUNK // s_new, dtype=F32)
    w["wmix_s"] = jnp.einsum("ab,gij->gaibj", eye, ws_s).reshape(SGU_GROUPS, CHUNK, CHUNK).astype(BF)
    w["bmix_s"] = jnp.repeat(jnp.tile(p["b_s"][:, :s_new].T, (CHUNK // s_new, 1)), gd, axis=1)
    w["w_ffn1"] = p["w_ffn1"].astype(BF)
    w["w_ffn2"] = p["w_ffn2"].astype(BF)
    return w


def kernel(x_prompt, x_sample, cache_mla, page_table, state_conv, state_win1, state_win2, state_win3, norm_mix, norm_ffn, norm_final, w_dq, g_q, w_uq, w_dkv, g_kv, w_uk, w_uv, w_o_mla, w_in_conv, w_conv, w_out_conv, w_qkv_c, w_o_c, rel_bias, w_uv_d, g_sgu, w_s, b_s, w_out_d, w_ffn1, w_ffn2):
    bp, sp, d = x_prompt.shape
    bs, ss, _ = x_sample.shape
    depth = norm_mix.shape[0]
    past_len = page_table.shape[1] * cache_mla.shape[1]
    params = dict(norm_mix=norm_mix, w_dq=w_dq, g_q=g_q, w_uq=w_uq, w_dkv=w_dkv, g_kv=g_kv, w_uk=w_uk,
                  w_uv=w_uv, w_o_mla=w_o_mla, w_in_conv=w_in_conv, w_out_conv=w_out_conv,
                  w_qkv_c=w_qkv_c, w_o_c=w_o_c, rel_bias=rel_bias, w_uv_d=w_uv_d, g_sgu=g_sgu,
                  w_s=w_s, b_s=b_s, w_out_d=w_out_d, w_ffn1=w_ffn1, w_ffn2=w_ffn2)
    w = _prep_weights(params, ss)
    xp = x_prompt.reshape(bp * sp, d)
    xs = x_sample.reshape(bs * ss, d)
    outs = {}
    for i in range(depth):
        gm = norm_mix[i][None]
        kind = i % 4
        if kind == 0:
            w["g_mix0"] = gm
            tm_p = 512
            cos_p, sin_p = _rope_tables(jnp.arange(sp, dtype=jnp.int32))
            ql, qp, rows_p, kc, kp = _mla_proj(xp, cos_p, sin_p, w, pos_blocks=sp // tm_p,
                                               q_dtype=BF, tm=tm_p)
            o = _mla_attn_prompt(ql, qp, kc, kp, batch=bp, seq=sp)
            xp = _mla_out(xp, o, w["w_uv_h"], w["w_o_mla"])
            pos_s = past_len + jnp.arange(ss, dtype=jnp.int32)
            cos_s, sin_s = _rope_tables(jnp.tile(pos_s, bs))
            tm_s = min(512, bs * ss)
            ql, qp, rows_s, _, _ = _mla_proj(xs, cos_s, sin_s, w, pos_blocks=bs * ss // tm_s,
                                             q_dtype=F32, tm=tm_s)
            o = _mla_attn_sample(ql.reshape(MLA_HEADS, bs, ss, KV_RANK),
                                 qp.reshape(MLA_HEADS, bs, ss, LANES),
                                 rows_s.reshape(bs, ss, -1), cache_mla, page_table)
            xs = _mla_out(xs, o.reshape(MLA_HEADS, bs * ss, KV_RANK), w["w_uv_h"], w["w_o_mla"])
            outs["mla_rows_p"] = rows_p.reshape(bp, sp, -1)
            outs["mla_rows_s"] = rows_s.reshape(bs, ss, -1)
        elif kind == 1:
            zeros_p = jnp.zeros((bp, 2, d), F32)
            xp, outs["conv_p"] = _conv_prompt(xp, zeros_p, gm, w["w_in_conv"], w_conv, w["w_out_conv"],
                                              batch=bp, seq=sp)
            xs, outs["conv_s"] = _conv_sample(xs, state_conv, gm, w["w_in_conv"], w_conv,
                                              w["w_out_conv"], batch=bs, seq=ss)
        elif kind == 2:
            q0, q1, q2, kv0, kv1, kv2, kvt2 = _dil_proj_prompt(xp, gm, w["w_qkv"], batch=bp, seq=sp)
            qs_p, kvs = (q0, q1, q2), (kv0, kv1, kv2)
            res = [_dil_attn_prompt(qs_p[g], kvs[g], w["bias_p"][g], g=g, batch=bp, seq=sp)
                   for g in range(3)]
            xp = _dil_comb(xp, [r[0] for r in res], [r[1] for r in res], w["w_o_c"], batch=bp, seq=sp)
            for g in range(2):
                n_last = min(WINDOWS[g], sp)
                last = kvs[g][:, :, sp - n_last:, :]
                outs[f"win{g + 1}_p"] = jnp.transpose(last, (0, 2, 1, 3)).reshape(bp, n_last, 2, HPG, HD)
            n_last = min(WINDOWS[2], sp)
            outs["win3_p"] = kvt2.reshape(bp, sp, 2, HPG, HD)[:, sp - n_last:]
            qs, kn0, kn1, kn2 = _dil_proj(xs, gm, w["w_qkv"], q_dtype=F32)
            kns = (kn0, kn1, kn2)
            o = _dil_attn_sample(qs, kns, (state_win1, state_win2, state_win3), w["tab_s"], w["ntab_s"],
                                 batch=bs, s_new=ss)
            xs = _proj_res(xs, o, w["w_o_c"])
            for g in range(3):
                outs[f"win{g + 1}_s"] = kns[g].reshape(bs, ss, 2, HPG, HD)
        else:
            (xp,) = _sgu(xp, gm, w["w_uv_d"], w["g_sgu"], w["wmix_p"], w["bmix_p"], w["w_out_d"],
                         emit_v=False)
            xs, v_s = _sgu(xs, gm, w["w_uv_d"], w["g_sgu"], w["wmix_s"], w["bmix_s"], w["w_out_d"],
                           emit_v=True)
            outs["sgu_v_s"] = v_s.reshape(bs, ss, -1)
        last = i == depth - 1
        gf = norm_final[None]
        xp = _ffn(xp, norm_ffn[i][None], w["w_ffn1"][i], w["w_ffn2"][i], gf, final_norm=last)
        xs = _ffn(xs, norm_ffn[i][None], w["w_ffn1"][i], w["w_ffn2"][i], gf, final_norm=last)
    return (xp.reshape(bp, sp, d), xs.reshape(bs, ss, d), outs["mla_rows_p"], outs["mla_rows_s"],
            outs["conv_p"], outs["conv_s"], outs["win1_p"], outs["win1_s"], outs["win2_p"],
            outs["win2_s"], outs["win3_p"], outs["win3_s"], outs["sgu_v_s"])
```

```python
import functools
import math

import jax
import jax.numpy as jnp
from jax import lax
from jax.experimental import pallas as pl
from jax.experimental.pallas import tpu as pltpu

EPS = 1e-6
ROPE_THETA = 10000.0
NEG = -1e30
BF = jnp.bfloat16
F32 = jnp.float32
LANES = 128
VMEM_LIMIT = 52 * 1024 * 1024

MLA_HEADS = 8
NOPE_DIM = 128
ROPE_DIM = 64
KV_RANK = 256
V_DIM = 128
WINDOWS = (128, 512, 2048)
DILATIONS = (1, 4, 16)
HPG = 4
HD = 128
N_BUCKETS = 32
MAX_DISTANCE = 2048
CHUNK = 128
SGU_GROUPS = 8
WIN_TILE = 128


def _cparams(*sem):
    return pltpu.CompilerParams(dimension_semantics=sem, vmem_limit_bytes=VMEM_LIMIT)


def _rms(x, g):
    return x * lax.rsqrt(jnp.mean(x * x, axis=-1, keepdims=True) + EPS) * g


def _dot(a, b):
    return jnp.dot(a, b, preferred_element_type=F32)


def _dot_nt(a, b):
    return lax.dot_general(a, b, (((1,), (1,)), ((), ())), preferred_element_type=F32)


def _dot_tn(a, b):
    return lax.dot_general(a, b, (((0,), (0,)), ((), ())), preferred_element_type=F32)


def _const_spec(shape):
    nd = len(shape)
    return pl.BlockSpec(shape, lambda *_: (0,) * nd)


def _ffn_kernel(x_ref, g_ref, w1_ref, w2_ref, gf_ref, o_ref, h_sc, acc_sc, *, nf, final_norm):
    _ffn_step(pl.program_id(1), x_ref, g_ref, w1_ref, w2_ref, gf_ref, o_ref, h_sc, acc_sc,
              nf=nf, final_norm=final_norm)


def _ffn_step(f, x_ref, g_ref, w1_ref, w2_ref, gf_ref, o_ref, h_sc, acc_sc, *, nf, final_norm, between=None):
    @pl.when(f == 0)
    def _():
        x = x_ref[...]
        h_sc[...] = _rms(x, g_ref[...]).astype(BF)
        acc_sc[...] = x

    a = _dot(h_sc[...], w1_ref[...])
    a = jnp.maximum(a, 0.0)
    acc_sc[...] += _dot((a * a).astype(BF), w2_ref[...])
    if between is not None:
        between()

    @pl.when(f == nf - 1)
    def _():
        y = acc_sc[...]
        if final_norm:
            y = _rms(y, gf_ref[...])
        o_ref[...] = y


def _ffn(x, g, w1, w2, gf, *, final_norm, tm=1024, tf=1024):
    t, d = x.shape
    dff = w1.shape[1]
    tm = min(tm, t)
    nf = dff // tf
    return pl.pallas_call(
        functools.partial(_ffn_kernel, nf=nf, final_norm=final_norm),
        grid=(t // tm, nf),
        in_specs=[
            pl.BlockSpec((tm, d), lambda i, f: (i, 0)),
            pl.BlockSpec((1, d), lambda i, f: (0, 0)),
            pl.BlockSpec((d, tf), lambda i, f: (0, f)),
            pl.BlockSpec((tf, d), lambda i, f: (f, 0)),
            pl.BlockSpec((1, d), lambda i, f: (0, 0)),
        ],
        out_specs=pl.BlockSpec((tm, d), lambda i, f: (i, 0)),
        out_shape=jax.ShapeDtypeStruct((t, d), F32),
        scratch_shapes=[pltpu.VMEM((tm, d), BF), pltpu.VMEM((tm, d), F32)],
        compiler_params=_cparams("parallel", "arbitrary"),
        name="ffn",
    )(x, g, w1, w2, gf)


def _proj_res_kernel(x_ref, a_ref, w_ref, o_ref):
    o_ref[...] = x_ref[...] + _dot(a_ref[...].astype(BF), w_ref[...])


def _proj_res(x, a, w, *, tm=512):
    t, d = x.shape
    k = a.shape[1]
    tm = min(tm, t)
    return pl.pallas_call(
        _proj_res_kernel,
        grid=(t // tm,),
        in_specs=[pl.BlockSpec((tm, d), lambda i: (i, 0)),
                  pl.BlockSpec((tm, k), lambda i: (i, 0)),
                  _const_spec(w.shape)],
        out_specs=pl.BlockSpec((tm, d), lambda i: (i, 0)),
        out_shape=jax.ShapeDtypeStruct((t, d), F32),
        compiler_params=_cparams("parallel"),
        name="proj_res",
    )(x, a, w)


def _mla_proj_kernel(x_ref, cos_ref, sin_ref, gm_ref, wdq_ref, gq_ref, wqn_ref, wqr_ref, wqt_ref,
                     wuk_ref, wkv_ref, wkt_ref, gkv_ref,
                     ql_ref, qp_ref, rows_ref, kc_ref, kp_ref, *, scale):
    h = _rms(x_ref[...], gm_ref[...]).astype(BF)
    cq = _rms(_dot(h, wdq_ref[...]), gq_ref[...]).astype(BF)
    cos = cos_ref[...]
    sin = sin_ref[...]
    q_nope = _dot(cq, wqn_ref[...])
    q_rope = _dot(cq, wqr_ref[...])
    q_rot = _dot(cq, wqt_ref[...])
    for hh in range(MLA_HEADS):
        sl = slice(hh * LANES, (hh + 1) * LANES)
        ql = _dot(q_nope[:, sl].astype(BF), wuk_ref[hh])
        ql_ref[hh] = (ql * scale).astype(ql_ref.dtype)
        qp = q_rope[:, sl] * cos + q_rot[:, sl] * sin
        qp_ref[hh] = (qp * scale).astype(qp_ref.dtype)
    ckv = _dot(h, wkv_ref[...])
    k_rot = _dot(h, wkt_ref[...])
    c = _rms(ckv[:, :KV_RANK], gkv_ref[...])
    kpe = ckv[:, KV_RANK:] * cos + k_rot * sin
    rows_ref[:, :KV_RANK] = c
    rows_ref[:, KV_RANK:] = kpe[:, :ROPE_DIM]
    kc_ref[...] = c.astype(BF)
    kp_ref[...] = kpe.astype(BF)


def _mla_proj(x, cos, sin, w, *, pos_blocks, q_dtype, tm=512):
    t, d = x.shape
    tm = min(tm, t)
    scale = 1.0 / math.sqrt(NOPE_DIM + ROPE_DIM)
    row = lambda i: (i, 0)
    consts = [w["g_mix0"], w["w_dq"], w["g_q"], w["w_uq_nope"], w["w_uq_rope"], w["w_uq_rot"],
              w["w_uk_t"], w["w_dkv"], w["w_dkv_rot"], w["g_kv"]]
    return pl.pallas_call(
        functools.partial(_mla_proj_kernel, scale=scale),
        grid=(t // tm,),
        in_specs=[pl.BlockSpec((tm, d), row),
                  pl.BlockSpec((tm, LANES), lambda i: (i % pos_blocks, 0)),
                  pl.BlockSpec((tm, LANES), lambda i: (i % pos_blocks, 0))]
                 + [_const_spec(c.shape) for c in consts],
        out_specs=[pl.BlockSpec((MLA_HEADS, tm, KV_RANK), lambda i: (0, i, 0)),
                   pl.BlockSpec((MLA_HEADS, tm, LANES), lambda i: (0, i, 0)),
                   pl.BlockSpec((tm, KV_RANK + ROPE_DIM), row),
                   pl.BlockSpec((tm, KV_RANK), row),
                   pl.BlockSpec((tm, LANES), row)],
        out_shape=[jax.ShapeDtypeStruct((MLA_HEADS, t, KV_RANK), q_dtype),
                   jax.ShapeDtypeStruct((MLA_HEADS, t, LANES), q_dtype),
                   jax.ShapeDtypeStruct((t, KV_RANK + ROPE_DIM), F32),
                   jax.ShapeDtypeStruct((t, KV_RANK), BF),
                   jax.ShapeDtypeStruct((t, LANES), BF)],
        compiler_params=_cparams("parallel"),
        name="mla_proj",
    )(x, cos, sin, *consts)


def _lane_rep(x, n):
    return x if n == LANES else jnp.concatenate([x] * (n // LANES), axis=1)


def _softmax_update(s, pv, m_ref, l_ref, acc_ref):
    m_old = m_ref[...]
    m_new = jnp.maximum(m_old, jnp.max(s, axis=-1, keepdims=True))
    alpha = jnp.exp(m_old - m_new)
    p = jnp.exp(s - _lane_rep(m_new, s.shape[1]))
    l_ref[...] = alpha * l_ref[...] + jnp.sum(p, axis=-1, keepdims=True)
    acc_ref[...] = _lane_rep(alpha, acc_ref.shape[1]) * acc_ref[...] + pv(p.astype(BF))
    m_ref[...] = m_new


def _mla_attn_p_kernel(ql_ref, qp_ref, kc_ref, kp_ref, o_ref, m_sc, l_sc, acc_sc, *, tq, tk, hc):
    i = pl.program_id(1)
    m_sc[...] = jnp.full(m_sc.shape, NEG, F32)
    l_sc[...] = jnp.zeros(l_sc.shape, F32)
    acc_sc[...] = jnp.zeros(acc_sc.shape, F32)
    n_full = (i * tq) // tk
    rc = hc * tq

    def tile(j, masked):
        start = pl.multiple_of(j * tk, tk)
        kc = kc_ref[pl.ds(start, tk), :]
        kp = kp_ref[pl.ds(start, tk), :]
        for c in range(MLA_HEADS // hc):
            ql = ql_ref[c * hc:(c + 1) * hc].reshape(rc, KV_RANK)
            qp = qp_ref[c * hc:(c + 1) * hc].reshape(rc, LANES)
            s = _dot_nt(ql, kc) + _dot_nt(qp, kp)
            if masked:
                q_pos = i * tq + lax.broadcasted_iota(jnp.int32, s.shape, 0) % tq
                k_pos = j * tk + lax.broadcasted_iota(jnp.int32, s.shape, 1)
                s = jnp.where(k_pos <= q_pos, s, NEG)
            rs = pl.ds(c * rc, rc)
            _softmax_update(s, lambda p: _dot(p, kc), m_sc.at[rs], l_sc.at[rs], acc_sc.at[rs])

    def body(j, carry):
        tile(j, False)
        return carry

    lax.fori_loop(0, n_full, body, 0)
    tile(n_full, True)
    o = acc_sc[...] / _lane_rep(l_sc[...], KV_RANK)
    o_ref[...] = o.reshape(MLA_HEADS, tq, KV_RANK).astype(o_ref.dtype)


def _mla_attn_prompt(ql, qp, kc, kp, *, batch, seq, tq=256, tk=512, hc=2):
    t = batch * seq
    nq = seq // tq
    rows = MLA_HEADS * tq
    qmap = lambda b, i: (0, b * nq + i, 0)
    return pl.pallas_call(
        functools.partial(_mla_attn_p_kernel, tq=tq, tk=tk, hc=hc),
        grid=(batch, nq),
        in_specs=[pl.BlockSpec((MLA_HEADS, tq, KV_RANK), qmap),
                  pl.BlockSpec((MLA_HEADS, tq, LANES), qmap),
                  pl.BlockSpec((seq, KV_RANK), lambda b, i: (b, 0)),
                  pl.BlockSpec((seq, LANES), lambda b, i: (b, 0))],
        out_specs=pl.BlockSpec((MLA_HEADS, tq, KV_RANK), qmap),
        out_shape=jax.ShapeDtypeStruct((MLA_HEADS, t, KV_RANK), BF),
        scratch_shapes=[pltpu.VMEM((rows, LANES), F32), pltpu.VMEM((rows, LANES), F32),
                        pltpu.VMEM((rows, KV_RANK), F32)],
        compiler_params=_cparams("parallel", "arbitrary"),
        name="mla_attn_prompt",
    )(ql, qp, kc, kp)


def _mla_attn_s_kernel(pt_ref, ql_ref, qp_ref, new_ref, *rest, pages, **kw):
    del pt_ref
    _mla_decode_step(pl.program_id(1), ql_ref, qp_ref, new_ref, rest[:pages], rest[pages],
                     *rest[pages + 1:], **kw)


def _mla_decode_step(step, ql_ref, qp_ref, new_ref, cache_refs, o_ref, m_sc, l_sc, acc_sc, kc_sc, kp_sc,
                     *, chunk, n_steps, page_size, s_new):
    pages = len(cache_refs)
    rows = MLA_HEADS * s_new
    ql = ql_ref[...].reshape(rows, KV_RANK).astype(BF)
    qp = qp_ref[...].reshape(rows, LANES).astype(BF)

    @pl.when(step == 0)
    def _():
        m_sc[...] = jnp.full(m_sc.shape, NEG, F32)
        l_sc[...] = jnp.zeros(l_sc.shape, F32)
        acc_sc[...] = jnp.zeros(acc_sc.shape, F32)

        kp_sc[ROPE_DIM:, :] = jnp.zeros((LANES - ROPE_DIM, kp_sc.shape[1]), BF)

    for c in range(pages // chunk):
        for k in range(c * chunk, (c + 1) * chunk):
            kb = cache_refs[k][...]
            sl = slice(k * page_size, (k + 1) * page_size)
            kc_sc[:, sl] = kb[:KV_RANK].astype(BF)
            kp_sc[:ROPE_DIM, sl] = kb[KV_RANK:].astype(BF)
        csl = slice(c * chunk * page_size, (c + 1) * chunk * page_size)
        kc_t = kc_sc[:, csl]
        s = _dot(ql, kc_t) + _dot(qp, kp_sc[:, csl])
        _softmax_update(s, lambda p, kc_t=kc_t: _dot_nt(p, kc_t), m_sc, l_sc, acc_sc)

    @pl.when(step == n_steps - 1)
    def _():
        nr = new_ref[...]
        pad_r = jnp.zeros((LANES - s_new, KV_RANK + ROPE_DIM), F32)
        nb = jnp.concatenate([nr, pad_r], axis=0)
        nc = nb[:, :KV_RANK].astype(BF)
        npe = jnp.concatenate([nb[:, KV_RANK:], jnp.zeros((LANES, LANES - ROPE_DIM), F32)],
                              axis=1).astype(BF)
        sn = _dot_nt(ql, nc) + _dot_nt(qp, npe)
        q_pos = lax.broadcasted_iota(jnp.int32, sn.shape, 0) % s_new
        k_pos = lax.broadcasted_iota(jnp.int32, sn.shape, 1)
        sn = jnp.where(k_pos <= q_pos, sn, NEG)
        _softmax_update(sn, lambda p: _dot(p, nc), m_sc, l_sc, acc_sc)
        o = acc_sc[...] / _lane_rep(l_sc[...], KV_RANK)
        o_ref[...] = o.reshape(MLA_HEADS, s_new, KV_RANK)


def _mla_attn_sample(ql, qp, rows_new, cache, page_table, *, pages=64, chunk=16):
    _, bs, s_new, _ = ql.shape
    n_pages = page_table.shape[1]
    page_size = cache.shape[1]
    row_w = cache.shape[2]
    pages = min(pages, n_pages)
    n_steps = n_pages // pages
    rows = MLA_HEADS * s_new
    pt = page_table.reshape(-1)
    qmap = lambda b, s, pt: (0, b, 0, 0)
    cache_t = jnp.transpose(cache, (0, 2, 1))

    def cache_spec(k):
        return pl.BlockSpec((None, row_w, page_size),
                            lambda b, s, pt: (pt[b * n_pages + s * pages + k], 0, 0))

    grid_spec = pltpu.PrefetchScalarGridSpec(
        num_scalar_prefetch=1,
        grid=(bs, n_steps),
        in_specs=[pl.BlockSpec((MLA_HEADS, None, s_new, KV_RANK), qmap),
                  pl.BlockSpec((MLA_HEADS, None, s_new, LANES), qmap),
                  pl.BlockSpec((None, s_new, row_w), lambda b, s, pt: (b, 0, 0))]
                 + [cache_spec(k) for k in range(pages)],
        out_specs=pl.BlockSpec((MLA_HEADS, None, s_new, KV_RANK), qmap),
        scratch_shapes=[pltpu.VMEM((rows, LANES), F32), pltpu.VMEM((rows, LANES), F32),
                        pltpu.VMEM((rows, KV_RANK), F32),
                        pltpu.VMEM((KV_RANK, pages * page_size), BF),
                        pltpu.VMEM((LANES, pages * page_size), BF)],
    )
    return pl.pallas_call(
        functools.partial(_mla_attn_s_kernel, pages=pages, chunk=min(chunk, pages), n_steps=n_steps,
                          page_size=page_size, s_new=s_new),
        grid_spec=grid_spec,
        out_shape=jax.ShapeDtypeStruct((MLA_HEADS, bs, s_new, KV_RANK), F32),
        compiler_params=_cparams("parallel", "arbitrary"),
        name="mla_attn_sample",
    )(pt, ql, qp, rows_new, *([cache_t] * pages))


def _ffn_mla_kernel(pt_ref, x_ref, g_ref, w1_ref, w2_ref, gf_ref, ql_ref, qp_ref, new_ref, *rest,
                    nf, final_norm, pages, spb, **kw):
    del pt_ref
    cache_refs = rest[:pages]
    o_ref, og_ref, h_sc, acc_sc = rest[pages:pages + 4]
    f = pl.program_id(1)
    _ffn_step(f, x_ref, g_ref, w1_ref, w2_ref, gf_ref, o_ref, h_sc, acc_sc, nf=nf, final_norm=final_norm,
              between=lambda: _mla_decode_step((pl.program_id(0) * nf + f) % spb, ql_ref, qp_ref, new_ref,
                                               cache_refs, og_ref, *rest[pages + 4:], n_steps=spb, **kw))


def _ffn_mla(x, g, w1, w2, gf, ql, qp, rows_new, cache, page_table, *, final_norm, batch_offset,
             tm=1024, tf=512, pages=32, chunk=16):
    t, d = x.shape
    dff = w1.shape[1]
    nf = dff // tf
    _, bs, s_new, _ = ql.shape
    n_pages = page_table.shape[1]
    page_size, row_w = cache.shape[1], cache.shape[2]
    pages = min(pages, n_pages)
    spb = n_pages // pages
    n_seq = (t // tm) * nf // spb
    rows = MLA_HEADS * s_new
    pt = page_table.reshape(-1)
    cache_t = jnp.transpose(cache, (0, 2, 1))
    seq_of = lambda i, f: batch_offset + (i * nf + f) // spb
    qmap = lambda i, f, pt: (0, seq_of(i, f), 0, 0)

    def cache_spec(k):
        return pl.BlockSpec(
            (None, row_w, page_size),
            lambda i, f, pt: (pt[seq_of(i, f) * n_pages + ((i * nf + f) % spb) * pages + k], 0, 0))

    grid_spec = pltpu.PrefetchScalarGridSpec(
        num_scalar_prefetch=1,
        grid=(t // tm, nf),
        in_specs=[pl.BlockSpec((tm, d), lambda i, f, pt: (i, 0)),
                  pl.BlockSpec((1, d), lambda i, f, pt: (0, 0)),
                  pl.BlockSpec((d, tf), lambda i, f, pt: (0, f)),
                  pl.BlockSpec((tf, d), lambda i, f, pt: (f, 0)),
                  pl.BlockSpec((1, d), lambda i, f, pt: (0, 0)),
                  pl.BlockSpec((MLA_HEADS, None, s_new, KV_RANK), qmap),
                  pl.BlockSpec((MLA_HEADS, None, s_new, LANES), qmap),
                  pl.BlockSpec((None, s_new, row_w), lambda i, f, pt: (seq_of(i, f), 0, 0))]
                 + [cache_spec(k) for k in range(pages)],
        out_specs=[pl.BlockSpec((tm, d), lambda i, f, pt: (i, 0)),
                   pl.BlockSpec((MLA_HEADS, None, s_new, KV_RANK),
                                lambda i, f, pt: (0, (i * nf + f) // spb, 0, 0))],
        scratch_shapes=[pltpu.VMEM((tm, d), BF), pltpu.VMEM((tm, d), F32),
                        pltpu.VMEM((rows, LANES), F32), pltpu.VMEM((rows, LANES), F32),
                        pltpu.VMEM((rows, KV_RANK), F32),
                        pltpu.VMEM((KV_RANK, pages * page_size), BF),
                        pltpu.VMEM((LANES, pages * page_size), BF)],
    )
    return pl.pallas_call(
        functools.partial(_ffn_mla_kernel, nf=nf, final_norm=final_norm, pages=pages, spb=spb,
                          chunk=min(chunk, pages), page_size=page_size, s_new=s_new),
        grid_spec=grid_spec,
        out_shape=[jax.ShapeDtypeStruct((t, d), F32),
                   jax.ShapeDtypeStruct((MLA_HEADS, n_seq, s_new, KV_RANK), F32)],
        compiler_params=_cparams("arbitrary", "arbitrary"),
        name="ffn_mla",
    )(pt, x, g, w1, w2, gf, ql, qp, rows_new, *([cache_t] * pages))


def _mla_out_kernel(x_ref, o_ref, wuv_ref, wo_ref, y_ref):
    vs = [_dot(o_ref[hh].astype(BF), wuv_ref[hh]).astype(BF) for hh in range(MLA_HEADS)]
    v = jnp.concatenate(vs, axis=1)
    y_ref[...] = x_ref[...] + _dot(v, wo_ref[...])


def _mla_out(x, o, w_uv_h, w_o, *, tm=512):
    t, d = x.shape
    tm = min(tm, t)
    return pl.pallas_call(
        _mla_out_kernel,
        grid=(t // tm,),
        in_specs=[pl.BlockSpec((tm, d), lambda i: (i, 0)),
                  pl.BlockSpec((MLA_HEADS, tm, KV_RANK), lambda i: (0, i, 0)),
                  _const_spec(w_uv_h.shape), _const_spec(w_o.shape)],
        out_specs=pl.BlockSpec((tm, d), lambda i: (i, 0)),
        out_shape=jax.ShapeDtypeStruct((t, d), F32),
        compiler_params=_cparams("parallel"),
        name="mla_out",
    )(x, o, w_uv_h, w_o)


def _conv_kernel(x_ref, st_ref, gm_ref, win_ref, wc_ref, wout_ref, y_ref, cs_ref, carry_sc,
                 *, tiles_per_seq, seq_rows):
    x = x_ref[...]
    tm, d = x.shape
    h = _rms(x, gm_ref[...]).astype(BF)
    bch = _dot(h, win_ref[...])
    gate_b = bch[:, :d]
    z = bch[:, d:2 * d] * bch[:, 2 * d:]
    z1 = pltpu.roll(z, 1, axis=0)
    z2 = pltpu.roll(z, 2, axis=0)
    row = lax.broadcasted_iota(jnp.int32, z.shape, 0)
    if tiles_per_seq is not None:
        i = pl.program_id(0)
        first = (i % tiles_per_seq) == 0
        st = st_ref[...]
        prev2 = jnp.where(first, st[0:1], carry_sc[6:7])
        prev1 = jnp.where(first, st[1:2], carry_sc[7:8])
        z1 = jnp.where(row == 0, prev1, z1)
        z2 = jnp.where(row == 0, prev2, jnp.where(row == 1, prev1, z2))
        carry_sc[...] = z[tm - 8:tm]
        cs_ref[...] = z[tm - 2:tm]
    else:
        nb = tm // seq_rows
        st = st_ref[...]
        j = lax.broadcasted_iota(jnp.int32, (nb, seq_rows, d), 1)
        z3 = z.reshape(nb, seq_rows, d)
        p1 = st[:, 1:2, :]
        p2 = st[:, 0:1, :]
        z1 = jnp.where(j == 0, p1, z1.reshape(nb, seq_rows, d)).reshape(tm, d)
        z2 = jnp.where(j == 0, p2, jnp.where(j == 1, p1, z2.reshape(nb, seq_rows, d))).reshape(tm, d)
        cs_ref[...] = z3[:, seq_rows - 2:, :]
    wc = wc_ref[...]
    y = wc[0:1] * z2 + wc[1:2] * z1 + wc[2:3] * z
    y_ref[...] = x + _dot((gate_b * y).astype(BF), wout_ref[...])


def _conv_prompt(x, state, gm, w_in, w_conv, w_out, *, batch, seq, tm=512):
    t, d = x.shape
    tps = seq // tm
    return pl.pallas_call(
        functools.partial(_conv_kernel, tiles_per_seq=tps, seq_rows=seq),
        grid=(t // tm,),
        in_specs=[pl.BlockSpec((tm, d), lambda i: (i, 0)),
                  pl.BlockSpec((None, 2, d), lambda i: (i // tps, 0, 0)),
                  _const_spec(gm.shape), _const_spec(w_in.shape), _const_spec(w_conv.shape),
                  _const_spec(w_out.shape)],
        out_specs=[pl.BlockSpec((tm, d), lambda i: (i, 0)),
                   pl.BlockSpec((None, 2, d), lambda i: (i // tps, 0, 0))],
        out_shape=[jax.ShapeDtypeStruct((t, d), F32), jax.ShapeDtypeStruct((batch, 2, d), F32)],
        scratch_shapes=[pltpu.VMEM((8, d), F32)],
        compiler_params=_cparams("arbitrary"),
        name="conv_prompt",
    )(x, state, gm, w_in, w_conv, w_out)


def _conv_sample(x, state, gm, w_in, w_conv, w_out, *, batch, seq, tm=512):
    t, d = x.shape
    tm = min(tm, t)
    nb = tm // seq
    return pl.pallas_call(
        functools.partial(_conv_kernel, tiles_per_seq=None, seq_rows=seq),
        grid=(t // tm,),
        in_specs=[pl.BlockSpec((tm, d), lambda i: (i, 0)),
                  pl.BlockSpec((nb, 2, d), lambda i: (i, 0, 0)),
                  _const_spec(gm.shape), _const_spec(w_in.shape), _const_spec(w_conv.shape),
                  _const_spec(w_out.shape)],
        out_specs=[pl.BlockSpec((tm, d), lambda i: (i, 0)),
                   pl.BlockSpec((nb, 2, d), lambda i: (i, 0, 0))],
        out_shape=[jax.ShapeDtypeStruct((t, d), F32), jax.ShapeDtypeStruct((batch, 2, d), F32)],
        scratch_shapes=[pltpu.VMEM((8, d), F32)],
        compiler_params=_cparams("parallel"),
        name="conv_sample",
    )(x, state, gm, w_in, w_conv, w_out)


GW = HPG * HD


def _dil_proj_kernel(x_ref, gm_ref, w_ref, q_ref, kv0_ref, kv1_ref, kv2_ref, *, scale):
    h = _rms(x_ref[...], gm_ref[...]).astype(BF)
    nq = 3 * GW
    q_ref[...] = (_dot(h, w_ref[:, :nq]) * scale).astype(q_ref.dtype)
    for g, kv_ref in enumerate((kv0_ref, kv1_ref, kv2_ref)):
        kv_ref[...] = _dot(h, w_ref[:, nq + g * 2 * GW: nq + (g + 1) * 2 * GW])


def _dil_proj(x, gm, w_qkv, *, q_dtype, tm=256):
    t, d = x.shape
    tm = min(tm, t)
    row = lambda i: (i, 0)
    return pl.pallas_call(
        functools.partial(_dil_proj_kernel, scale=1.0 / math.sqrt(HD)),
        grid=(t // tm,),
        in_specs=[pl.BlockSpec((tm, d), row), _const_spec(gm.shape), _const_spec(w_qkv.shape)],
        out_specs=[pl.BlockSpec((tm, 3 * GW), row)] + [pl.BlockSpec((tm, 2 * GW), row)] * 3,
        out_shape=[jax.ShapeDtypeStruct((t, 3 * GW), q_dtype)]
                  + [jax.ShapeDtypeStruct((t, 2 * GW), F32)] * 3,
        compiler_params=_cparams("parallel"),
        name="dil_proj",
    )(x, gm, w_qkv)


def _dil_proj_p_kernel(x_ref, gm_ref, w_ref, q0_ref, q1_ref, q2_ref, kv0_ref, kv1_ref, kv2_ref, kvt_ref,
                       *, scale):
    tm = x_ref.shape[0]
    h = _rms(x_ref[...], gm_ref[...]).astype(BF)
    nq = 3 * GW
    q = _dot(h, w_ref[:, :nq]) * scale
    for g, q_ref in enumerate((q0_ref, q1_ref, q2_ref)):
        for s in range(HPG):
            q_ref[s] = q[:, g * GW + s * HD: g * GW + (s + 1) * HD]
    for g, kv_ref in enumerate((kv0_ref, kv1_ref, kv2_ref)):
        kv = _dot(h, w_ref[:, nq + g * 2 * GW: nq + (g + 1) * 2 * GW])
        for s in range(2 * HPG):
            kv_ref[s] = kv[:, s * HD:(s + 1) * HD]
            if g == 2:
                kvt_ref[pl.ds(s, tm, stride=2 * HPG), :] = kv[:, s * HD:(s + 1) * HD]


def _dil_proj_prompt(x, gm, w_qkv, *, batch, seq, tm=256):
    t, d = x.shape
    tps = seq // tm
    slab = lambda i: (i // tps, 0, i % tps, 0)
    return pl.pallas_call(
        functools.partial(_dil_proj_p_kernel, scale=1.0 / math.sqrt(HD)),
        grid=(t // tm,),
        in_specs=[pl.BlockSpec((tm, d), lambda i: (i, 0)), _const_spec(gm.shape), _const_spec(w_qkv.shape)],
        out_specs=[pl.BlockSpec((None, HPG, tm, HD), slab)] * 3
                  + [pl.BlockSpec((None, 2 * HPG, tm, HD), slab)] * 3
                  + [pl.BlockSpec((tm * 2 * HPG, HD), lambda i: (i, 0))],
        out_shape=[jax.ShapeDtypeStruct((batch, HPG, seq, HD), F32)] * 3
                  + [jax.ShapeDtypeStruct((batch, 2 * HPG, seq, HD), F32)] * 3
                  + [jax.ShapeDtypeStruct((t * 2 * HPG, HD), F32)],
        compiler_params=_cparams("parallel"),
        name="dil_proj_prompt",
    )(x, gm, w_qkv)


def _dil_attn_p_kernel(q_ref, kv_ref, bias_ref, o_ref, lse_ref, *, dil, n_i):
    lane_grp = lax.broadcasted_iota(jnp.int32, (WIN_TILE, LANES), 1) // (LANES // HPG)
    col = lax.broadcasted_iota(jnp.int32, (WIN_TILE, 2 * WIN_TILE), 1)
    span = WIN_TILE * dil

    def rows_at(start):
        return pl.ds(start, WIN_TILE) if dil == 1 else pl.ds(start, WIN_TILE, stride=dil)

    def unit(u, carry):
        r = u // n_i
        i = u % n_i
        start = r + i * span
        cur = rows_at(start)
        prev = rows_at(jnp.maximum(start - span, r))
        prev_mask = jnp.where((col < WIN_TILE) & (i == 0), NEG, 0.0)
        lse_tile = jnp.zeros((WIN_TILE, LANES), F32)
        for hh in range(HPG):
            q = q_ref[hh, cur, :].astype(BF)
            k = jnp.concatenate([kv_ref[hh, prev, :], kv_ref[hh, cur, :]], axis=0).astype(BF)
            v = jnp.concatenate([kv_ref[HPG + hh, prev, :], kv_ref[HPG + hh, cur, :]], axis=0).astype(BF)
            s = _dot_nt(q, k) + bias_ref[hh] + prev_mask
            m = jnp.max(s, axis=-1, keepdims=True)
            e = jnp.exp(s - m)
            l = jnp.sum(e, axis=-1, keepdims=True)
            p = (e / l).astype(BF)
            o_ref[hh, cur, :] = _dot(p, v)
            lse_tile = jnp.where(lane_grp == hh, m + jnp.log(l), lse_tile)
        lse_ref[cur, :] = lse_tile
        return carry

    lax.fori_loop(0, dil * n_i, unit, 0, unroll=4)


def _dil_attn_prompt(q, kv, bias, *, g, batch, seq):
    dil = DILATIONS[g]
    n_i = seq // dil // WIN_TILE
    return pl.pallas_call(
        functools.partial(_dil_attn_p_kernel, dil=dil, n_i=n_i),
        grid=(batch,),
        in_specs=[pl.BlockSpec((None, HPG, seq, HD), lambda b: (b, 0, 0, 0)),
                  pl.BlockSpec((None, 2 * HPG, seq, HD), lambda b: (b, 0, 0, 0)),
                  _const_spec(bias.shape)],
        out_specs=[pl.BlockSpec((None, HPG, seq, HD), lambda b: (b, 0, 0, 0)),
                   pl.BlockSpec((None, seq, LANES), lambda b: (b, 0, 0))],
        out_shape=[jax.ShapeDtypeStruct((batch, HPG, seq, HD), F32),
                   jax.ShapeDtypeStruct((batch, seq, LANES), F32)],
        compiler_params=_cparams("parallel"),
        name=f"dil_attn_prompt_g{g}",
    )(q, kv, bias)


def _dil_comb_kernel(x_ref, o0_ref, o1_ref, o2_ref, l0_ref, l1_ref, l2_ref, w_ref, y_ref):
    l0, l1, l2 = l0_ref[...], l1_ref[...], l2_ref[...]
    m = jnp.maximum(jnp.maximum(l0, l1), l2)
    e0, e1, e2 = jnp.exp(l0 - m), jnp.exp(l1 - m), jnp.exp(l2 - m)
    den = e0 + e1 + e2
    ws = (e0 / den, e1 / den, e2 / den)
    tm = x_ref.shape[0]
    parts = []
    for hh in range(HPG):
        acc = jnp.zeros((tm, HD), F32)
        for wg, o_ref in zip(ws, (o0_ref, o1_ref, o2_ref)):
            wcol = wg[:, hh * (LANES // HPG): hh * (LANES // HPG) + 1]
            acc = acc + wcol * o_ref[hh]
        parts.append(acc.astype(BF))
    y_ref[...] = x_ref[...] + _dot(jnp.concatenate(parts, axis=1), w_ref[...])


def _dil_comb(x, os_, lses, w_o, *, batch, seq, tm=512):
    t, d = x.shape
    tps = seq // tm
    row = lambda i: (i, 0)
    return pl.pallas_call(
        _dil_comb_kernel,
        grid=(t // tm,),
        in_specs=[pl.BlockSpec((tm, d), row)]
                 + [pl.BlockSpec((None, HPG, tm, HD), lambda i: (i // tps, 0, i % tps, 0))] * 3
                 + [pl.BlockSpec((None, tm, LANES), lambda i: (i // tps, i % tps, 0))] * 3
                 + [_const_spec(w_o.shape)],
        out_specs=pl.BlockSpec((tm, d), row),
        out_shape=jax.ShapeDtypeStruct((t, d), F32),
        compiler_params=_cparams("parallel"),
        name="dil_comb",
    )(x, *os_, *lses, w_o)


def _dil_attn_s_kernel(q_ref, kn0_ref, kn1_ref, kn2_ref, b0_ref, b1_ref, b2_ref,
                       t0_ref, t1_ref, t2_ref, n0_ref, n1_ref, n2_ref, o_ref, *, s_new):
    q = q_ref[...]
    rows_w = lax.broadcasted_iota(jnp.int32, (LANES, GW), 0)
    lanes_w = lax.broadcasted_iota(jnp.int32, (LANES, GW), 1)
    head_sel = (lanes_w // HD) == (rows_w // s_new)
    logits = []
    vals = []
    for g, (kn_ref, b_ref, t_ref, n_ref) in enumerate((
            (kn0_ref, b0_ref, t0_ref, n0_ref), (kn1_ref, b1_ref, t1_ref, n1_ref),
            (kn2_ref, b2_ref, t2_ref, n2_ref))):
        qg = q[:, g * GW:(g + 1) * GW]
        wq = jnp.where(head_sel, jnp.tile(qg, (LANES // s_new, 1)), 0.0).astype(BF)
        n_rows = b_ref.shape[0] // (2 * HPG)
        if len(b_ref.shape) == 2:
            slab = lambda s: b_ref[pl.ds(s, n_rows, stride=2 * HPG), :]
        else:
            slab = lambda s: b_ref[:, pl.ds(s, s_new, stride=2 * HPG), :].reshape(-1, HD)
        buf_k = jnp.concatenate([slab(s) for s in range(HPG)], axis=1)
        buf_v = jnp.concatenate([slab(HPG + s) for s in range(HPG)], axis=1)
        new = jnp.concatenate([kn_ref[...], jnp.zeros((LANES - s_new, 2 * GW), F32)], axis=0)
        logits.append(_dot_nt(buf_k.astype(BF), wq) + t_ref[...])
        vals.append(buf_v.astype(BF))
        logits.append(_dot_nt(new[:, :GW].astype(BF), wq) + n_ref[...])
        vals.append(new[:, GW:].astype(BF))
    m = logits[0].max(axis=0, keepdims=True)
    for lg in logits[1:]:
        m = jnp.maximum(m, lg.max(axis=0, keepdims=True))
    ps = [jnp.exp(lg - m) for lg in logits]
    den = ps[0].sum(axis=0, keepdims=True)
    for p in ps[1:]:
        den = den + p.sum(axis=0, keepdims=True)
    inv = 1.0 / den
    out = jnp.zeros((LANES, GW), F32)
    for p, v in zip(ps, vals):
        out = out + _dot_tn((p * inv).astype(BF), v)
    for hh in range(HPG):
        o_ref[:, hh * HD:(hh + 1) * HD] = out[hh * s_new:(hh + 1) * s_new, hh * HD:(hh + 1) * HD]


def _dil_attn_sample(q, kv_new, bufs, tabs, ntabs, *, batch, s_new):
    t = batch * s_new
    bufs2, buf_specs, tabs = [], [], list(tabs)
    for g, b in enumerate(bufs):
        r, dil = b.shape[1], DILATIONS[g]
        if dil > s_new:
            bufs2.append(b.reshape(batch, r // dil, dil * 2 * HPG, HD))
            buf_specs.append(pl.BlockSpec((None, r // dil, s_new * 2 * HPG, HD), lambda b: (b, 0, 0, 0)))
            tabs[g] = tabs[g].reshape(r // dil, dil, LANES)[:, :s_new].reshape(-1, LANES)
        else:
            bufs2.append(b.reshape(batch, r * 2 * HPG, HD))
            buf_specs.append(pl.BlockSpec((None, r * 2 * HPG, HD), lambda b: (b, 0, 0)))
    return pl.pallas_call(
        functools.partial(_dil_attn_s_kernel, s_new=s_new),
        grid=(batch,),
        in_specs=[pl.BlockSpec((s_new, 3 * GW), lambda b: (b, 0))]
                 + [pl.BlockSpec((s_new, 2 * GW), lambda b: (b, 0))] * 3
                 + buf_specs
                 + [_const_spec(tb.shape) for tb in tabs]
                 + [_const_spec(tb.shape) for tb in ntabs],
        out_specs=pl.BlockSpec((s_new, GW), lambda b: (b, 0)),
        out_shape=jax.ShapeDtypeStruct((t, GW), F32),
        compiler_params=_cparams("parallel"),
        name="dil_attn_sample",
    )(q, *kv_new, *bufs2, *tabs, *ntabs)


def _sgu_kernel(x_ref, gm_ref, wuv_ref, gs_ref, wmix_ref, bmix_ref, wout_ref, y_ref, *rest, emit_v):
    if emit_v:
        v_ref, mixed_sc = rest
    else:
        (mixed_sc,) = rest
    x = x_ref[...]
    tm, d = x.shape
    h = _rms(x, gm_ref[...]).astype(BF)
    uv = jax.nn.gelu(_dot(h, wuv_ref[...]))
    w = uv.shape[1] // 2
    u = uv[:, :w]
    v = _rms(uv[:, w:], gs_ref[...])
    if emit_v:
        v_ref[...] = v
    vb = v.astype(BF)
    gd = w // SGU_GROUPS
    for c in range(tm // CHUNK):
        rs = slice(c * CHUNK, (c + 1) * CHUNK)
        for g in range(SGU_GROUPS):
            cs = slice(g * gd, (g + 1) * gd)
            mixed_sc[rs, cs] = _dot(wmix_ref[g], vb[rs, cs]) + bmix_ref[:, cs]
    y_ref[...] = x + _dot((u * mixed_sc[...]).astype(BF), wout_ref[...])


def _sgu(x, gm, w_uv, g_sgu, wmix, bmix, w_out, *, emit_v, tm=512):
    t, d = x.shape
    tm = min(tm, t)
    w = w_uv.shape[1] // 2
    row = lambda i: (i, 0)
    out_specs = [pl.BlockSpec((tm, d), row)]
    out_shape = [jax.ShapeDtypeStruct((t, d), F32)]
    if emit_v:
        out_specs.append(pl.BlockSpec((tm, w), row))
        out_shape.append(jax.ShapeDtypeStruct((t, w), F32))
    consts = [gm, w_uv, g_sgu, wmix, bmix, w_out]
    return pl.pallas_call(
        functools.partial(_sgu_kernel, emit_v=emit_v),
        grid=(t // tm,),
        in_specs=[pl.BlockSpec((tm, d), row)] + [_const_spec(c.shape) for c in consts],
        out_specs=out_specs,
        out_shape=out_shape,
        scratch_shapes=[pltpu.VMEM((tm, w), F32)],
        compiler_params=_cparams("parallel"),
        name="sgu",
    )(x, *consts)


def _rot_half_cols(w):
    half = w.shape[-1] // 2
    return jnp.concatenate([-w[..., half:], w[..., :half]], axis=-1)


def _pad_last(w, n):
    return jnp.pad(w, [(0, 0)] * (w.ndim - 1) + [(0, n - w.shape[-1])])


def _rope_tables(pos):
    half = ROPE_DIM // 2
    inv = ROPE_THETA ** (-jnp.arange(half, dtype=F32) / half)
    ang = pos.astype(F32)[:, None] * inv[None, :]
    cos, sin = jnp.cos(ang), jnp.sin(ang)
    return (_pad_last(jnp.concatenate([cos, cos], axis=1), LANES),
            _pad_last(jnp.concatenate([sin, sin], axis=1), LANES))


def _t5_bucket(dist):
    max_exact = N_BUCKETS // 2
    n = jnp.maximum(dist, 1).astype(F32)
    large = max_exact + (jnp.log(n / max_exact) / math.log(MAX_DISTANCE / max_exact)
                         * (N_BUCKETS - max_exact)).astype(jnp.int32)
    return jnp.where(dist < max_exact, dist, jnp.minimum(large, N_BUCKETS - 1))


def _group_bias(rel_bias, g):
    n_keys = WINDOWS[g] // DILATIONS[g] + 1
    dist = DILATIONS[g] * jnp.arange(n_keys, dtype=jnp.int32)
    return rel_bias[_t5_bucket(dist)][:, g * HPG:(g + 1) * HPG].T


def _prompt_bias_table(bias_g):
    period = 3 * WIN_TILE
    line = jnp.concatenate([bias_g[:, ::-1], jnp.full((HPG, period - WIN_TILE - 1), NEG, F32)], axis=1)
    flat = jnp.tile(line, (1, WIN_TILE))[:, :WIN_TILE * (period - 1)]
    return flat.reshape(HPG, WIN_TILE, period - 1)[:, :, :2 * WIN_TILE]


def _sample_bias_tables(bias_g, g, s_new):
    win, dil = WINDOWS[g], DILATIONS[g]
    n_keys = win // dil + 1
    line = jnp.concatenate([bias_g[:, :, None], jnp.full((HPG, n_keys, dil - 1), NEG, F32)], axis=2)
    line = line.reshape(HPG, n_keys * dil)
    line = jnp.concatenate([line[:, :win + 1], jnp.full((HPG, s_new), NEG, F32)], axis=1)
    buf = jnp.stack([line[:, j + 1: win + j + 1][:, ::-1] for j in range(s_new)])
    buf = jnp.transpose(buf, (2, 1, 0)).reshape(win, HPG * s_new)
    lpad = jnp.concatenate([jnp.full((HPG, s_new - 1), NEG, F32), line[:, :s_new]], axis=1)
    new = jnp.stack([lpad[:, j: j + s_new][:, ::-1] for j in range(s_new)])
    new = jnp.transpose(new, (2, 1, 0)).reshape(s_new, HPG * s_new)
    pad = lambda t, r: jnp.pad(t, ((0, r - t.shape[0]), (0, LANES - t.shape[1])), constant_values=NEG)
    return pad(buf, win), pad(new, LANES)


def _prep_weights(p, s_new):
    w = {}
    w["g_mix0"] = p["norm_mix"][0][None]
    w["w_dq"] = p["w_dq"].astype(BF)
    w["g_q"] = p["g_q"][None]
    q_rank = p["w_uq"].shape[0]
    wuq = p["w_uq"].reshape(q_rank, MLA_HEADS, NOPE_DIM + ROPE_DIM)
    w["w_uq_nope"] = wuq[:, :, :NOPE_DIM].reshape(q_rank, -1).astype(BF)
    wr = wuq[:, :, NOPE_DIM:]
    w["w_uq_rope"] = _pad_last(wr, LANES).reshape(q_rank, -1).astype(BF)
    w["w_uq_rot"] = _pad_last(_rot_half_cols(wr), LANES).reshape(q_rank, -1).astype(BF)
    w["w_uk_t"] = jnp.transpose(p["w_uk"], (1, 2, 0)).astype(BF)
    w["w_dkv"] = _pad_last(p["w_dkv"], KV_RANK + LANES).astype(BF)
    w["w_dkv_rot"] = _pad_last(_rot_half_cols(p["w_dkv"][:, KV_RANK:]), LANES).astype(BF)
    w["g_kv"] = p["g_kv"][None]
    w["w_uv_h"] = jnp.transpose(p["w_uv"], (1, 0, 2)).astype(BF)
    w["w_o_mla"] = p["w_o_mla"].astype(BF)
    w["w_in_conv"] = p["w_in_conv"].astype(BF)
    w["w_out_conv"] = p["w_out_conv"].astype(BF)
    d = p["w_qkv_c"].shape[0]
    wqkv = p["w_qkv_c"].reshape(d, 3, 3, GW)
    cols = [wqkv[:, 0].reshape(d, 3 * GW)]
    for g in range(3):
        cols += [wqkv[:, 1, g], wqkv[:, 2, g]]
    w["w_qkv"] = jnp.concatenate(cols, axis=1).astype(BF)
    w["w_o_c"] = p["w_o_c"].astype(BF)
    biases = [_group_bias(p["rel_bias"], g) for g in range(3)]
    w["bias_p"] = [_prompt_bias_table(b) for b in biases]
    tabs = [_sample_bias_tables(b, g, s_new) for g, b in enumerate(biases)]
    w["tab_s"] = [tb[0] for tb in tabs]
    w["ntab_s"] = [tb[1] for tb in tabs]
    w["w_uv_d"] = p["w_uv_d"].astype(BF)
    w["g_sgu"] = p["g_sgu"][None]
    w["w_out_d"] = p["w_out_d"].astype(BF)
    gd = p["w_uv_d"].shape[1] // 2 // SGU_GROUPS
    tril = jnp.tril(jnp.ones((CHUNK, CHUNK), F32))
    w["wmix_p"] = (p["w_s"] * tril).astype(BF)
    w["bmix_p"] = jnp.repeat(p["b_s"].T, gd, axis=1)
    ws_s = p["w_s"][:, :s_new, :s_new] * jnp.tril(jnp.ones((s_new, s_new), F32))
    eye = jnp.eye(CHUNK // s_new, dtype=F32)
    w["wmix_s"] = jnp.einsum("ab,gij->gaibj", eye, ws_s).reshape(SGU_GROUPS, CHUNK, CHUNK).astype(BF)
    w["bmix_s"] = jnp.repeat(jnp.tile(p["b_s"][:, :s_new].T, (CHUNK // s_new, 1)), gd, axis=1)
    w["w_ffn1"] = p["w_ffn1"].astype(BF)
    w["w_ffn2"] = p["w_ffn2"].astype(BF)
    return w


def kernel(x_prompt, x_sample, cache_mla, page_table, state_conv, state_win1, state_win2, state_win3, norm_mix, norm_ffn, norm_final, w_dq, g_q, w_uq, w_dkv, g_kv, w_uk, w_uv, w_o_mla, w_in_conv, w_conv, w_out_conv, w_qkv_c, w_o_c, rel_bias, w_uv_d, g_sgu, w_s, b_s, w_out_d, w_ffn1, w_ffn2):
    bp, sp, d = x_prompt.shape
    bs, ss, _ = x_sample.shape
    depth = norm_mix.shape[0]
    past_len = page_table.shape[1] * cache_mla.shape[1]
    params = dict(norm_mix=norm_mix, w_dq=w_dq, g_q=g_q, w_uq=w_uq, w_dkv=w_dkv, g_kv=g_kv, w_uk=w_uk,
                  w_uv=w_uv, w_o_mla=w_o_mla, w_in_conv=w_in_conv, w_out_conv=w_out_conv,
                  w_qkv_c=w_qkv_c, w_o_c=w_o_c, rel_bias=rel_bias, w_uv_d=w_uv_d, g_sgu=g_sgu,
                  w_s=w_s, b_s=b_s, w_out_d=w_out_d, w_ffn1=w_ffn1, w_ffn2=w_ffn2)
    w = _prep_weights(params, ss)
    xp = x_prompt.reshape(bp * sp, d)
    xs = x_sample.reshape(bs * ss, d)
    outs = {}
    assert depth == 4, "one layer of each mixer kind"
    gf = norm_final[None]
    gms = [norm_mix[i][None] for i in range(depth)]
    ffn_args = lambda i: (norm_ffn[i][None], w["w_ffn1"][i], w["w_ffn2"][i], gf)
    ffn = lambda x, i: _ffn(x, *ffn_args(i), final_norm=(i == depth - 1))

    tm_p = 512
    cos_p, sin_p = _rope_tables(jnp.arange(sp, dtype=jnp.int32))
    ql, qp, rows_p, kc, kp = _mla_proj(xp, cos_p, sin_p, w, pos_blocks=sp // tm_p, q_dtype=BF, tm=tm_p)
    o = _mla_attn_prompt(ql, qp, kc, kp, batch=bp, seq=sp)
    xp = _mla_out(xp, o, w["w_uv_h"], w["w_o_mla"])
    outs["mla_rows_p"] = rows_p.reshape(bp, sp, -1)
    pos_s = past_len + jnp.arange(ss, dtype=jnp.int32)
    cos_s, sin_s = _rope_tables(jnp.tile(pos_s, bs))
    tm_s = min(512, bs * ss)
    ql, qp, rows_s, _, _ = _mla_proj(xs, cos_s, sin_s, w, pos_blocks=bs * ss // tm_s, q_dtype=F32, tm=tm_s)
    outs["mla_rows_s"] = rows_s.reshape(bs, ss, -1)
    dec_args = (ql.reshape(MLA_HEADS, bs, ss, KV_RANK), qp.reshape(MLA_HEADS, bs, ss, LANES),
                outs["mla_rows_s"], cache_mla, page_table)
    xp, o_a = _ffn_mla(xp, *ffn_args(0), *dec_args, final_norm=False, batch_offset=0)
    assert 2 * o_a.shape[1] == bs, "the two hosting calls must cover every sample sequence"
    zeros_p = jnp.zeros((bp, 2, d), F32)
    xp, outs["conv_p"] = _conv_prompt(xp, zeros_p, gms[1], w["w_in_conv"], w_conv, w["w_out_conv"],
                                      batch=bp, seq=sp)
    xp, o_b = _ffn_mla(xp, *ffn_args(1), *dec_args, final_norm=False, batch_offset=o_a.shape[1])
    o = jnp.concatenate([o_a, o_b], axis=1)
    xs = _mla_out(xs, o.reshape(MLA_HEADS, bs * ss, KV_RANK), w["w_uv_h"], w["w_o_mla"])
    xs = ffn(xs, 0)
    xs, outs["conv_s"] = _conv_sample(xs, state_conv, gms[1], w["w_in_conv"], w_conv, w["w_out_conv"],
                                      batch=bs, seq=ss)
    xs = ffn(xs, 1)
    q0, q1, q2, kv0, kv1, kv2, kvt2 = _dil_proj_prompt(xp, gms[2], w["w_qkv"], batch=bp, seq=sp)
    qs_p, kvs = (q0, q1, q2), (kv0, kv1, kv2)
    res = [_dil_attn_prompt(qs_p[g], kvs[g], w["bias_p"][g], g=g, batch=bp, seq=sp) for g in range(3)]
    xp = _dil_comb(xp, [r[0] for r in res], [r[1] for r in res], w["w_o_c"], batch=bp, seq=sp)
    for g in range(2):
        n_last = min(WINDOWS[g], sp)
        last = kvs[g][:, :, sp - n_last:, :]
        outs[f"win{g + 1}_p"] = jnp.transpose(last, (0, 2, 1, 3)).reshape(bp, n_last, 2, HPG, HD)
    n_last = min(WINDOWS[2], sp)
    outs["win3_p"] = kvt2.reshape(bp, sp, 2, HPG, HD)[:, sp - n_last:]
    qs, kn0, kn1, kn2 = _dil_proj(xs, gms[2], w["w_qkv"], q_dtype=F32)
    kns = (kn0, kn1, kn2)
    o = _dil_attn_sample(qs, kns, (state_win1, state_win2, state_win3), w["tab_s"], w["ntab_s"],
                         batch=bs, s_new=ss)
    xs = _proj_res(xs, o, w["w_o_c"])
    for g in range(3):
        outs[f"win{g + 1}_s"] = kns[g].reshape(bs, ss, 2, HPG, HD)
    xp = ffn(xp, 2)
    xs = ffn(xs, 2)
    (xp,) = _sgu(xp, gms[3], w["w_uv_d"], w["g_sgu"], w["wmix_p"], w["bmix_p"], w["w_out_d"], emit_v=False)
    xs, v_s = _sgu(xs, gms[3], w["w_uv_d"], w["g_sgu"], w["wmix_s"], w["bmix_s"], w["w_out_d"], emit_v=True)
    outs["sgu_v_s"] = v_s.reshape(bs, ss, -1)
    xp = ffn(xp, 3)
    xs = ffn(xs, 3)
    return (xp.reshape(bp, sp, d), xs.reshape(bs, ss, d), outs["mla_rows_p"], outs["mla_rows_s"],
            outs["conv_p"], outs["conv_s"], outs["win1_p"], outs["win1_s"], outs["win2_p"],
            outs["win2_s"], outs["win3_p"], outs["win3_s"], outs["sgu_v_s"])
```

```python
import functools
import math

import jax
import jax.numpy as jnp
from jax import lax
from jax.experimental import pallas as pl
from jax.experimental.pallas import tpu as pltpu

EPS = 1e-6
ROPE_THETA = 10000.0
NEG = -1e30
BF = jnp.bfloat16
F32 = jnp.float32
LANES = 128
VMEM_LIMIT = 52 * 1024 * 1024

MLA_HEADS = 8
NOPE_DIM = 128
ROPE_DIM = 64
KV_RANK = 256
V_DIM = 128
WINDOWS = (128, 512, 2048)
DILATIONS = (1, 4, 16)
HPG = 4
HD = 128
N_BUCKETS = 32
MAX_DISTANCE = 2048
CHUNK = 128
SGU_GROUPS = 8
WIN_TILE = 128


def _cparams(*sem):
    return pltpu.CompilerParams(dimension_semantics=sem, vmem_limit_bytes=VMEM_LIMIT)


def _rms(x, g):
    return x * lax.rsqrt(jnp.mean(x * x, axis=-1, keepdims=True) + EPS) * g


def _dot(a, b):
    return jnp.dot(a, b, preferred_element_type=F32)


def _dot_nt(a, b):
    return lax.dot_general(a, b, (((1,), (1,)), ((), ())), preferred_element_type=F32)


def _dot_tn(a, b):
    return lax.dot_general(a, b, (((0,), (0,)), ((), ())), preferred_element_type=F32)


def _const_spec(shape):
    nd = len(shape)
    return pl.BlockSpec(shape, lambda *_: (0,) * nd)


def _ffn_kernel(x_ref, g_ref, w1_ref, w2_ref, gf_ref, o_ref, h_sc, acc_sc, *, nf, final_norm):
    f = pl.program_id(1)

    @pl.when(f == 0)
    def _():
        x = x_ref[...]
        h_sc[...] = _rms(x, g_ref[...]).astype(BF)
        acc_sc[...] = x

    a = _dot(h_sc[...], w1_ref[...])
    a = jnp.maximum(a, 0.0)
    acc_sc[...] += _dot((a * a).astype(BF), w2_ref[...])

    @pl.when(f == nf - 1)
    def _():
        y = acc_sc[...]
        if final_norm:
            y = _rms(y, gf_ref[...])
        o_ref[...] = y


def _ffn(x, g, w1, w2, gf, *, layer, final_norm, tm=1024, tf=1024):
    t, d = x.shape
    dff = w1.shape[2]
    tm = min(tm, t)
    nf = dff // tf
    return pl.pallas_call(
        functools.partial(_ffn_kernel, nf=nf, final_norm=final_norm),
        grid=(t // tm, nf),
        in_specs=[
            pl.BlockSpec((tm, d), lambda i, f: (i, 0)),
            pl.BlockSpec((1, d), lambda i, f: (0, 0)),
            pl.BlockSpec((None, d, tf), lambda i, f: (layer, 0, f)),
            pl.BlockSpec((None, tf, d), lambda i, f: (layer, f, 0)),
            pl.BlockSpec((1, d), lambda i, f: (0, 0)),
        ],
        out_specs=pl.BlockSpec((tm, d), lambda i, f: (i, 0)),
        out_shape=jax.ShapeDtypeStruct((t, d), F32),
        scratch_shapes=[pltpu.VMEM((tm, d), BF), pltpu.VMEM((tm, d), F32)],
        compiler_params=_cparams("parallel", "arbitrary"),
        name="ffn",
    )(x, g, w1, w2, gf)


def _proj_res_kernel(x_ref, a_ref, w_ref, o_ref):
    o_ref[...] = x_ref[...] + _dot(a_ref[...].astype(BF), w_ref[...])


def _proj_res(x, a, w, *, tm=512):
    t, d = x.shape
    k = a.shape[1]
    tm = min(tm, t)
    return pl.pallas_call(
        _proj_res_kernel,
        grid=(t // tm,),
        in_specs=[pl.BlockSpec((tm, d), lambda i: (i, 0)),
                  pl.BlockSpec((tm, k), lambda i: (i, 0)),
                  _const_spec(w.shape)],
        out_specs=pl.BlockSpec((tm, d), lambda i: (i, 0)),
        out_shape=jax.ShapeDtypeStruct((t, d), F32),
        compiler_params=_cparams("parallel"),
        name="proj_res",
    )(x, a, w)


def _mla_proj_kernel(x_ref, cos_ref, sin_ref, gm_ref, wdq_ref, gq_ref, wqn_ref, wqr_ref, wqt_ref,
                     wuk_ref, wkv_ref, wkt_ref, gkv_ref,
                     ql_ref, qp_ref, rows_ref, kc_ref, kp_ref, *, scale):
    h = _rms(x_ref[...], gm_ref[...]).astype(BF)
    cq = _rms(_dot(h, wdq_ref[...]), gq_ref[...]).astype(BF)
    cos = cos_ref[...]
    sin = sin_ref[...]
    q_nope = _dot(cq, wqn_ref[...])
    q_rope = _dot(cq, wqr_ref[...])
    q_rot = _dot(cq, wqt_ref[...])
    for hh in range(MLA_HEADS):
        sl = slice(hh * LANES, (hh + 1) * LANES)
        ql = _dot(q_nope[:, sl].astype(BF), wuk_ref[hh])
        ql_ref[hh] = (ql * scale).astype(ql_ref.dtype)
        qp = q_rope[:, sl] * cos + q_rot[:, sl] * sin
        qp_ref[hh] = (qp * scale).astype(qp_ref.dtype)
    ckv = _dot(h, wkv_ref[...])
    k_rot = _dot(h, wkt_ref[...])
    c = _rms(ckv[:, :KV_RANK], gkv_ref[...])
    kpe = ckv[:, KV_RANK:] * cos + k_rot * sin
    rows_ref[:, :KV_RANK] = c
    rows_ref[:, KV_RANK:] = kpe[:, :ROPE_DIM]
    kc_ref[...] = c.astype(BF)
    kp_ref[...] = kpe.astype(BF)


def _mla_proj(x, cos, sin, w, *, pos_blocks, q_dtype, tm=512):
    t, d = x.shape
    tm = min(tm, t)
    scale = 1.0 / math.sqrt(NOPE_DIM + ROPE_DIM)
    row = lambda i: (i, 0)
    consts = [w["g_mix0"], w["w_dq"], w["g_q"], w["w_uq_nope"], w["w_uq_rope"], w["w_uq_rot"],
              w["w_uk_t"], w["w_dkv"], w["w_dkv_rot"], w["g_kv"]]
    return pl.pallas_call(
        functools.partial(_mla_proj_kernel, scale=scale),
        grid=(t // tm,),
        in_specs=[pl.BlockSpec((tm, d), row),
                  pl.BlockSpec((tm, LANES), lambda i: (i % pos_blocks, 0)),
                  pl.BlockSpec((tm, LANES), lambda i: (i % pos_blocks, 0))]
                 + [_const_spec(c.shape) for c in consts],
        out_specs=[pl.BlockSpec((MLA_HEADS, tm, KV_RANK), lambda i: (0, i, 0)),
                   pl.BlockSpec((MLA_HEADS, tm, LANES), lambda i: (0, i, 0)),
                   pl.BlockSpec((tm, KV_RANK + ROPE_DIM), row),
                   pl.BlockSpec((tm, KV_RANK), row),
                   pl.BlockSpec((tm, LANES), row)],
        out_shape=[jax.ShapeDtypeStruct((MLA_HEADS, t, KV_RANK), q_dtype),
                   jax.ShapeDtypeStruct((MLA_HEADS, t, LANES), q_dtype),
                   jax.ShapeDtypeStruct((t, KV_RANK + ROPE_DIM), F32),
                   jax.ShapeDtypeStruct((t, KV_RANK), BF),
                   jax.ShapeDtypeStruct((t, LANES), BF)],
        compiler_params=_cparams("parallel"),
        name="mla_proj",
    )(x, cos, sin, *consts)


def _lane_rep(x, n):
    return x if n == LANES else jnp.concatenate([x] * (n // LANES), axis=1)


def _softmax_update(s, pv, m_ref, l_ref, acc_ref):
    m_old = m_ref[...]
    m_new = jnp.maximum(m_old, jnp.max(s, axis=-1, keepdims=True))
    alpha = jnp.exp(m_old - m_new)
    p = jnp.exp(s - _lane_rep(m_new, s.shape[1]))
    l_ref[...] = alpha * l_ref[...] + jnp.sum(p, axis=-1, keepdims=True)
    acc_ref[...] = _lane_rep(alpha, acc_ref.shape[1]) * acc_ref[...] + pv(p.astype(BF))
    m_ref[...] = m_new


def _mla_attn_p_kernel(ql_ref, qp_ref, kc_ref, kp_ref, o_ref, m_sc, l_sc, acc_sc, *, tq, tk, hc):
    i = pl.program_id(1)
    m_sc[...] = jnp.full(m_sc.shape, NEG, F32)
    l_sc[...] = jnp.zeros(l_sc.shape, F32)
    acc_sc[...] = jnp.zeros(acc_sc.shape, F32)
    n_full = (i * tq) // tk
    rc = hc * tq

    def tile(start, width, masked):
        start = pl.multiple_of(start, width)
        kc = kc_ref[pl.ds(start, width), :]
        kp = kp_ref[pl.ds(start, width), :]
        for c in range(MLA_HEADS // hc):
            ql = ql_ref[c * hc:(c + 1) * hc].reshape(rc, KV_RANK)
            qp = qp_ref[c * hc:(c + 1) * hc].reshape(rc, LANES)
            s = _dot_nt(ql, kc) + _dot_nt(qp, kp)
            if masked:
                q_pos = i * tq + lax.broadcasted_iota(jnp.int32, s.shape, 0) % tq
                k_pos = start + lax.broadcasted_iota(jnp.int32, s.shape, 1)
                s = jnp.where(k_pos <= q_pos, s, NEG)
            rs = pl.ds(c * rc, rc)
            _softmax_update(s, lambda p: _dot(p, kc), m_sc.at[rs], l_sc.at[rs], acc_sc.at[rs])

    def body(j, carry):
        tile(j * tk, tk, False)
        return carry

    lax.fori_loop(0, n_full, body, 0)
    rem = i * tq - n_full * tk
    if tk == 2 * tq:
        @pl.when(rem == 0)
        def _():
            tile(n_full * tk, tq, True)

        @pl.when(rem != 0)
        def _():
            tile(n_full * tk, tk, True)
    else:
        tile(n_full * tk, tk, True)
    o = acc_sc[...] / _lane_rep(l_sc[...], KV_RANK)
    o_ref[...] = o.reshape(MLA_HEADS, tq, KV_RANK).astype(o_ref.dtype)


def _mla_attn_prompt(ql, qp, kc, kp, *, batch, seq, tq=256, tk=512, hc=2):
    t = batch * seq
    nq = seq // tq
    rows = MLA_HEADS * tq
    qmap = lambda b, i: (0, b * nq + i, 0)
    return pl.pallas_call(
        functools.partial(_mla_attn_p_kernel, tq=tq, tk=tk, hc=hc),
        grid=(batch, nq),
        in_specs=[pl.BlockSpec((MLA_HEADS, tq, KV_RANK), qmap),
                  pl.BlockSpec((MLA_HEADS, tq, LANES), qmap),
                  pl.BlockSpec((seq, KV_RANK), lambda b, i: (b, 0)),
                  pl.BlockSpec((seq, LANES), lambda b, i: (b, 0))],
        out_specs=pl.BlockSpec((MLA_HEADS, tq, KV_RANK), qmap),
        out_shape=jax.ShapeDtypeStruct((MLA_HEADS, t, KV_RANK), BF),
        scratch_shapes=[pltpu.VMEM((rows, LANES), F32), pltpu.VMEM((rows, LANES), F32),
                        pltpu.VMEM((rows, KV_RANK), F32)],
        compiler_params=_cparams("parallel", "arbitrary"),
        name="mla_attn_prompt",
    )(ql, qp, kc, kp)


def _mla_attn_s_kernel(pt_ref, ql_ref, qp_ref, new_ref, *rest, pages, chunk, n_steps, page_size, s_new):
    del pt_ref
    cache_refs = rest[:pages]
    o_ref = rest[pages]
    m_sc, l_sc, acc_sc, kc_sc, kp_sc = rest[pages + 1:]
    step = pl.program_id(1)
    rows = MLA_HEADS * s_new
    ql = ql_ref[...].reshape(rows, KV_RANK).astype(BF)
    qp = qp_ref[...].reshape(rows, LANES).astype(BF)

    @pl.when(step == 0)
    def _():
        m_sc[...] = jnp.full(m_sc.shape, NEG, F32)
        l_sc[...] = jnp.zeros(l_sc.shape, F32)
        acc_sc[...] = jnp.zeros(acc_sc.shape, F32)

        kp_sc[ROPE_DIM:, :] = jnp.zeros((LANES - ROPE_DIM, kp_sc.shape[1]), BF)

    for c in range(pages // chunk):
        for k in range(c * chunk, (c + 1) * chunk):
            kb = cache_refs[k][...]
            sl = slice(k * page_size, (k + 1) * page_size)
            kc_sc[:, sl] = kb[:KV_RANK].astype(BF)
            kp_sc[:ROPE_DIM, sl] = kb[KV_RANK:].astype(BF)
        csl = slice(c * chunk * page_size, (c + 1) * chunk * page_size)
        kc_t = kc_sc[:, csl]
        s = _dot(ql, kc_t) + _dot(qp, kp_sc[:, csl])
        _softmax_update(s, lambda p, kc_t=kc_t: _dot_nt(p, kc_t), m_sc, l_sc, acc_sc)

    @pl.when(step == n_steps - 1)
    def _():
        nr = new_ref[...]
        pad_r = jnp.zeros((LANES - s_new, KV_RANK + ROPE_DIM), F32)
        nb = jnp.concatenate([nr, pad_r], axis=0)
        nc = nb[:, :KV_RANK].astype(BF)
        npe = jnp.concatenate([nb[:, KV_RANK:], jnp.zeros((LANES, LANES - ROPE_DIM), F32)],
                              axis=1).astype(BF)
        sn = _dot_nt(ql, nc) + _dot_nt(qp, npe)
        q_pos = lax.broadcasted_iota(jnp.int32, sn.shape, 0) % s_new
        k_pos = lax.broadcasted_iota(jnp.int32, sn.shape, 1)
        sn = jnp.where(k_pos <= q_pos, sn, NEG)
        _softmax_update(sn, lambda p: _dot(p, nc), m_sc, l_sc, acc_sc)
        o = acc_sc[...] / _lane_rep(l_sc[...], KV_RANK)
        o_ref[...] = o.reshape(MLA_HEADS, s_new, KV_RANK)


def _mla_attn_sample(ql, qp, rows_new, cache, page_table, *, pages=64, chunk=16):
    _, bs, s_new, _ = ql.shape
    n_pages = page_table.shape[1]
    page_size = cache.shape[1]
    row_w = cache.shape[2]
    pages = min(pages, n_pages)
    n_steps = n_pages // pages
    rows = MLA_HEADS * s_new
    pt = page_table.reshape(-1)
    qmap = lambda b, s, pt: (0, b, 0, 0)
    cache_t = jnp.transpose(cache, (0, 2, 1))

    def cache_spec(k):
        return pl.BlockSpec((None, row_w, page_size),
                            lambda b, s, pt: (pt[b * n_pages + s * pages + k], 0, 0))

    grid_spec = pltpu.PrefetchScalarGridSpec(
        num_scalar_prefetch=1,
        grid=(bs, n_steps),
        in_specs=[pl.BlockSpec((MLA_HEADS, None, s_new, KV_RANK), qmap),
                  pl.BlockSpec((MLA_HEADS, None, s_new, LANES), qmap),
                  pl.BlockSpec((None, s_new, row_w), lambda b, s, pt: (b, 0, 0))]
                 + [cache_spec(k) for k in range(pages)],
        out_specs=pl.BlockSpec((MLA_HEADS, None, s_new, KV_RANK), qmap),
        scratch_shapes=[pltpu.VMEM((rows, LANES), F32), pltpu.VMEM((rows, LANES), F32),
                        pltpu.VMEM((rows, KV_RANK), F32),
                        pltpu.VMEM((KV_RANK, pages * page_size), BF),
                        pltpu.VMEM((LANES, pages * page_size), BF)],
    )
    return pl.pallas_call(
        functools.partial(_mla_attn_s_kernel, pages=pages, chunk=min(chunk, pages), n_steps=n_steps,
                          page_size=page_size, s_new=s_new),
        grid_spec=grid_spec,
        out_shape=jax.ShapeDtypeStruct((MLA_HEADS, bs, s_new, KV_RANK), F32),
        compiler_params=_cparams("parallel", "arbitrary"),
        name="mla_attn_sample",
    )(pt, ql, qp, rows_new, *([cache_t] * pages))


def _mla_out_kernel(x_ref, o_ref, wuv_ref, wo_ref, y_ref):
    vs = [_dot(o_ref[hh].astype(BF), wuv_ref[hh]).astype(BF) for hh in range(MLA_HEADS)]
    v = jnp.concatenate(vs, axis=1)
    y_ref[...] = x_ref[...] + _dot(v, wo_ref[...])


def _mla_out(x, o, w_uv_h, w_o, *, tm=1024):
    t, d = x.shape
    tm = min(tm, t)
    return pl.pallas_call(
        _mla_out_kernel,
        grid=(t // tm,),
        in_specs=[pl.BlockSpec((tm, d), lambda i: (i, 0)),
                  pl.BlockSpec((MLA_HEADS, tm, KV_RANK), lambda i: (0, i, 0)),
                  _const_spec(w_uv_h.shape), _const_spec(w_o.shape)],
        out_specs=pl.BlockSpec((tm, d), lambda i: (i, 0)),
        out_shape=jax.ShapeDtypeStruct((t, d), F32),
        compiler_params=_cparams("parallel"),
        name="mla_out",
    )(x, o, w_uv_h, w_o)


def _conv_kernel(x_ref, st_ref, gm_ref, win_ref, wc_ref, wout_ref, y_ref, cs_ref, carry_sc,
                 *, tiles_per_seq, seq_rows):
    x = x_ref[...]
    tm, d = x.shape
    h = _rms(x, gm_ref[...]).astype(BF)
    bch = _dot(h, win_ref[...])
    gate_b = bch[:, :d]
    z = bch[:, d:2 * d] * bch[:, 2 * d:]
    z1 = pltpu.roll(z, 1, axis=0)
    z2 = pltpu.roll(z, 2, axis=0)
    row = lax.broadcasted_iota(jnp.int32, z.shape, 0)
    if tiles_per_seq is not None:
        i = pl.program_id(0)
        first = (i % tiles_per_seq) == 0
        st = st_ref[...]
        prev2 = jnp.where(first, st[0:1], carry_sc[6:7])
        prev1 = jnp.where(first, st[1:2], carry_sc[7:8])
        z1 = jnp.where(row == 0, prev1, z1)
        z2 = jnp.where(row == 0, prev2, jnp.where(row == 1, prev1, z2))
        carry_sc[...] = z[tm - 8:tm]
        cs_ref[...] = z[tm - 2:tm]
    else:
        nb = tm // seq_rows
        st = st_ref[...]
        j = lax.broadcasted_iota(jnp.int32, (nb, seq_rows, d), 1)
        z3 = z.reshape(nb, seq_rows, d)
        p1 = st[:, 1:2, :]
        p2 = st[:, 0:1, :]
        z1 = jnp.where(j == 0, p1, z1.reshape(nb, seq_rows, d)).reshape(tm, d)
        z2 = jnp.where(j == 0, p2, jnp.where(j == 1, p1, z2.reshape(nb, seq_rows, d))).reshape(tm, d)
        cs_ref[...] = z3[:, seq_rows - 2:, :]
    wc = wc_ref[...]
    y = wc[0:1] * z2 + wc[1:2] * z1 + wc[2:3] * z
    y_ref[...] = x + _dot((gate_b * y).astype(BF), wout_ref[...])


def _conv_prompt(x, state, gm, w_in, w_conv, w_out, *, batch, seq, tm=1024):
    t, d = x.shape
    tps = seq // tm
    return pl.pallas_call(
        functools.partial(_conv_kernel, tiles_per_seq=tps, seq_rows=seq),
        grid=(t // tm,),
        in_specs=[pl.BlockSpec((tm, d), lambda i: (i, 0)),
                  pl.BlockSpec((None, 2, d), lambda i: (i // tps, 0, 0)),
                  _const_spec(gm.shape), _const_spec(w_in.shape), _const_spec(w_conv.shape),
                  _const_spec(w_out.shape)],
        out_specs=[pl.BlockSpec((tm, d), lambda i: (i, 0)),
                   pl.BlockSpec((None, 2, d), lambda i: (i // tps, 0, 0))],
        out_shape=[jax.ShapeDtypeStruct((t, d), F32), jax.ShapeDtypeStruct((batch, 2, d), F32)],
        scratch_shapes=[pltpu.VMEM((8, d), F32)],
        compiler_params=_cparams("arbitrary"),
        name="conv_prompt",
    )(x, state, gm, w_in, w_conv, w_out)


def _conv_sample(x, state, gm, w_in, w_conv, w_out, *, batch, seq, tm=512):
    t, d = x.shape
    tm = min(tm, t)
    nb = tm // seq
    return pl.pallas_call(
        functools.partial(_conv_kernel, tiles_per_seq=None, seq_rows=seq),
        grid=(t // tm,),
        in_specs=[pl.BlockSpec((tm, d), lambda i: (i, 0)),
                  pl.BlockSpec((nb, 2, d), lambda i: (i, 0, 0)),
                  _const_spec(gm.shape), _const_spec(w_in.shape), _const_spec(w_conv.shape),
                  _const_spec(w_out.shape)],
        out_specs=[pl.BlockSpec((tm, d), lambda i: (i, 0)),
                   pl.BlockSpec((nb, 2, d), lambda i: (i, 0, 0))],
        out_shape=[jax.ShapeDtypeStruct((t, d), F32), jax.ShapeDtypeStruct((batch, 2, d), F32)],
        scratch_shapes=[pltpu.VMEM((8, d), F32)],
        compiler_params=_cparams("parallel"),
        name="conv_sample",
    )(x, state, gm, w_in, w_conv, w_out)


GW = HPG * HD


def _dil_proj_kernel(x_ref, gm_ref, w_ref, q_ref, kv0_ref, kv1_ref, kv2_ref, *, scale):
    h = _rms(x_ref[...], gm_ref[...]).astype(BF)
    nq = 3 * GW
    q_ref[...] = (_dot(h, w_ref[:, :nq]) * scale).astype(q_ref.dtype)
    for g, kv_ref in enumerate((kv0_ref, kv1_ref, kv2_ref)):
        kv_ref[...] = _dot(h, w_ref[:, nq + g * 2 * GW: nq + (g + 1) * 2 * GW])


def _dil_proj(x, gm, w_qkv, *, q_dtype, tm=256):
    t, d = x.shape
    tm = min(tm, t)
    row = lambda i: (i, 0)
    return pl.pallas_call(
        functools.partial(_dil_proj_kernel, scale=1.0 / math.sqrt(HD)),
        grid=(t // tm,),
        in_specs=[pl.BlockSpec((tm, d), row), _const_spec(gm.shape), _const_spec(w_qkv.shape)],
        out_specs=[pl.BlockSpec((tm, 3 * GW), row)] + [pl.BlockSpec((tm, 2 * GW), row)] * 3,
        out_shape=[jax.ShapeDtypeStruct((t, 3 * GW), q_dtype)]
                  + [jax.ShapeDtypeStruct((t, 2 * GW), F32)] * 3,
        compiler_params=_cparams("parallel"),
        name="dil_proj",
    )(x, gm, w_qkv)


def _dil_proj_p_kernel(x_ref, gm_ref, w_ref, q0_ref, q1_ref, q2_ref, kv0_ref, kv1_ref, kv2_ref, kvt_ref,
                       *, scale):
    tm = x_ref.shape[0]
    h = _rms(x_ref[...], gm_ref[...]).astype(BF)
    nq = 3 * GW
    q = _dot(h, w_ref[:, :nq]) * scale
    for g, q_ref in enumerate((q0_ref, q1_ref, q2_ref)):
        for s in range(HPG):
            q_ref[s] = q[:, g * GW + s * HD: g * GW + (s + 1) * HD]
    for g, kv_ref in enumerate((kv0_ref, kv1_ref, kv2_ref)):
        kv = _dot(h, w_ref[:, nq + g * 2 * GW: nq + (g + 1) * 2 * GW])
        for s in range(2 * HPG):
            kv_ref[s] = kv[:, s * HD:(s + 1) * HD]
            if g == 2:
                kvt_ref[pl.ds(s, tm, stride=2 * HPG), :] = kv[:, s * HD:(s + 1) * HD]


def _dil_proj_prompt(x, gm, w_qkv, *, batch, seq, tm=512):
    t, d = x.shape
    tps = seq // tm
    slab = lambda i: (i // tps, 0, i % tps, 0)
    return pl.pallas_call(
        functools.partial(_dil_proj_p_kernel, scale=1.0 / math.sqrt(HD)),
        grid=(t // tm,),
        in_specs=[pl.BlockSpec((tm, d), lambda i: (i, 0)), _const_spec(gm.shape), _const_spec(w_qkv.shape)],
        out_specs=[pl.BlockSpec((None, HPG, tm, HD), slab)] * 3
                  + [pl.BlockSpec((None, 2 * HPG, tm, HD), slab)] * 3
                  + [pl.BlockSpec((tm * 2 * HPG, HD), lambda i: (i, 0))],
        out_shape=[jax.ShapeDtypeStruct((batch, HPG, seq, HD), F32)] * 3
                  + [jax.ShapeDtypeStruct((batch, 2 * HPG, seq, HD), F32)] * 3
                  + [jax.ShapeDtypeStruct((t * 2 * HPG, HD), F32)],
        compiler_params=_cparams("parallel"),
        name="dil_proj_prompt",
    )(x, gm, w_qkv)


def _dil_attn_p_kernel(q_ref, kv_ref, bias_ref, o_ref, lse_ref, *, dil, n_i):
    lane_grp = lax.broadcasted_iota(jnp.int32, (WIN_TILE, LANES), 1) // (LANES // HPG)
    col = lax.broadcasted_iota(jnp.int32, (WIN_TILE, 2 * WIN_TILE), 1)
    span = WIN_TILE * dil

    def rows_at(start):
        return pl.ds(start, WIN_TILE) if dil == 1 else pl.ds(start, WIN_TILE, stride=dil)

    def unit(u, carry):
        r = u // n_i
        i = u % n_i
        start = r + i * span
        cur = rows_at(start)
        prev = rows_at(jnp.maximum(start - span, r))
        prev_mask = jnp.where((col < WIN_TILE) & (i == 0), NEG, 0.0)
        lse_tile = jnp.zeros((WIN_TILE, LANES), F32)
        for hh in range(HPG):
            q = q_ref[hh, cur, :].astype(BF)
            k = jnp.concatenate([kv_ref[hh, prev, :], kv_ref[hh, cur, :]], axis=0).astype(BF)
            v = jnp.concatenate([kv_ref[HPG + hh, prev, :], kv_ref[HPG + hh, cur, :]], axis=0).astype(BF)
            s = _dot_nt(q, k) + bias_ref[hh] + prev_mask
            m = jnp.max(s, axis=-1, keepdims=True)
            e = jnp.exp(s - m)
            l = jnp.sum(e, axis=-1, keepdims=True)
            p = (e / l).astype(BF)
            o_ref[hh, cur, :] = _dot(p, v)
            lse_tile = jnp.where(lane_grp == hh, m + jnp.log(l), lse_tile)
        lse_ref[cur, :] = lse_tile
        return carry

    lax.fori_loop(0, dil * n_i, unit, 0, unroll=4)


def _dil_attn_prompt(q, kv, bias, *, g, batch, seq):
    dil = DILATIONS[g]
    n_i = seq // dil // WIN_TILE
    return pl.pallas_call(
        functools.partial(_dil_attn_p_kernel, dil=dil, n_i=n_i),
        grid=(batch,),
        in_specs=[pl.BlockSpec((None, HPG, seq, HD), lambda b: (b, 0, 0, 0)),
                  pl.BlockSpec((None, 2 * HPG, seq, HD), lambda b: (b, 0, 0, 0)),
                  _const_spec(bias.shape)],
        out_specs=[pl.BlockSpec((None, HPG, seq, HD), lambda b: (b, 0, 0, 0)),
                   pl.BlockSpec((None, seq, LANES), lambda b: (b, 0, 0))],
        out_shape=[jax.ShapeDtypeStruct((batch, HPG, seq, HD), F32),
                   jax.ShapeDtypeStruct((batch, seq, LANES), F32)],
        compiler_params=_cparams("parallel"),
        name=f"dil_attn_prompt_g{g}",
    )(q, kv, bias)


def _dil_comb_kernel(x_ref, o0_ref, o1_ref, o2_ref, l0_ref, l1_ref, l2_ref, w_ref, y_ref):
    l0, l1, l2 = l0_ref[...], l1_ref[...], l2_ref[...]
    m = jnp.maximum(jnp.maximum(l0, l1), l2)
    e0, e1, e2 = jnp.exp(l0 - m), jnp.exp(l1 - m), jnp.exp(l2 - m)
    den = e0 + e1 + e2
    ws = (e0 / den, e1 / den, e2 / den)
    tm = x_ref.shape[0]
    parts = []
    for hh in range(HPG):
        acc = jnp.zeros((tm, HD), F32)
        for wg, o_ref in zip(ws, (o0_ref, o1_ref, o2_ref)):
            wcol = wg[:, hh * (LANES // HPG): hh * (LANES // HPG) + 1]
            acc = acc + wcol * o_ref[hh]
        parts.append(acc.astype(BF))
    y_ref[...] = x_ref[...] + _dot(jnp.concatenate(parts, axis=1), w_ref[...])


def _dil_comb(x, os_, lses, w_o, *, batch, seq, tm=512):
    t, d = x.shape
    tps = seq // tm
    row = lambda i: (i, 0)
    return pl.pallas_call(
        _dil_comb_kernel,
        grid=(t // tm,),
        in_specs=[pl.BlockSpec((tm, d), row)]
                 + [pl.BlockSpec((None, HPG, tm, HD), lambda i: (i // tps, 0, i % tps, 0))] * 3
                 + [pl.BlockSpec((None, tm, LANES), lambda i: (i // tps, i % tps, 0))] * 3
                 + [_const_spec(w_o.shape)],
        out_specs=pl.BlockSpec((tm, d), row),
        out_shape=jax.ShapeDtypeStruct((t, d), F32),
        compiler_params=_cparams("parallel"),
        name="dil_comb",
    )(x, *os_, *lses, w_o)


def _dil_attn_s_kernel(q_ref, kn0_ref, kn1_ref, kn2_ref, b0_ref, b1_ref, b2_ref,
                       t0_ref, t1_ref, t2_ref, n0_ref, n1_ref, n2_ref, o_ref, *, s_new):
    q = q_ref[...]
    rows_w = lax.broadcasted_iota(jnp.int32, (LANES, GW), 0)
    lanes_w = lax.broadcasted_iota(jnp.int32, (LANES, GW), 1)
    head_sel = (lanes_w // HD) == (rows_w // s_new)
    logits = []
    vals = []
    for g, (kn_ref, b_ref, t_ref, n_ref) in enumerate((
            (kn0_ref, b0_ref, t0_ref, n0_ref), (kn1_ref, b1_ref, t1_ref, n1_ref),
            (kn2_ref, b2_ref, t2_ref, n2_ref))):
        qg = q[:, g * GW:(g + 1) * GW]
        wq = jnp.where(head_sel, jnp.tile(qg, (LANES // s_new, 1)), 0.0).astype(BF)
        n_rows = b_ref.shape[0] // (2 * HPG)
        if len(b_ref.shape) == 2:
            slab = lambda s: b_ref[pl.ds(s, n_rows, stride=2 * HPG), :]
        else:
            slab = lambda s: b_ref[:, pl.ds(s, s_new, stride=2 * HPG), :].reshape(-1, HD)
        buf_k = jnp.concatenate([slab(s) for s in range(HPG)], axis=1)
        buf_v = jnp.concatenate([slab(HPG + s) for s in range(HPG)], axis=1)
        new = jnp.concatenate([kn_ref[...], jnp.zeros((LANES - s_new, 2 * GW), F32)], axis=0)
        logits.append(_dot_nt(buf_k.astype(BF), wq) + t_ref[...])
        vals.append(buf_v.astype(BF))
        logits.append(_dot_nt(new[:, :GW].astype(BF), wq) + n_ref[...])
        vals.append(new[:, GW:].astype(BF))
    m = logits[0].max(axis=0, keepdims=True)
    for lg in logits[1:]:
        m = jnp.maximum(m, lg.max(axis=0, keepdims=True))
    ps = [jnp.exp(lg - m) for lg in logits]
    den = ps[0].sum(axis=0, keepdims=True)
    for p in ps[1:]:
        den = den + p.sum(axis=0, keepdims=True)
    inv = 1.0 / den
    out = jnp.zeros((LANES, GW), F32)
    for p, v in zip(ps, vals):
        out = out + _dot_tn((p * inv).astype(BF), v)
    for hh in range(HPG):
        o_ref[:, hh * HD:(hh + 1) * HD] = out[hh * s_new:(hh + 1) * s_new, hh * HD:(hh + 1) * HD]


def _dil_attn_sample(q, kv_new, bufs, tabs, ntabs, *, batch, s_new):
    t = batch * s_new
    bufs2, buf_specs, tabs = [], [], list(tabs)
    for g, b in enumerate(bufs):
        r, dil = b.shape[1], DILATIONS[g]
        if dil > s_new:
            bufs2.append(b.reshape(batch, r // dil, dil * 2 * HPG, HD))
            buf_specs.append(pl.BlockSpec((None, r // dil, s_new * 2 * HPG, HD), lambda b: (b, 0, 0, 0)))
            tabs[g] = tabs[g].reshape(r // dil, dil, LANES)[:, :s_new].reshape(-1, LANES)
        else:
            bufs2.append(b.reshape(batch, r * 2 * HPG, HD))
            buf_specs.append(pl.BlockSpec((None, r * 2 * HPG, HD), lambda b: (b, 0, 0)))
    return pl.pallas_call(
        functools.partial(_dil_attn_s_kernel, s_new=s_new),
        grid=(batch,),
        in_specs=[pl.BlockSpec((s_new, 3 * GW), lambda b: (b, 0))]
                 + [pl.BlockSpec((s_new, 2 * GW), lambda b: (b, 0))] * 3
                 + buf_specs
                 + [_const_spec(tb.shape) for tb in tabs]
                 + [_const_spec(tb.shape) for tb in ntabs],
        out_specs=pl.BlockSpec((s_new, GW), lambda b: (b, 0)),
        out_shape=jax.ShapeDtypeStruct((t, GW), F32),
        compiler_params=_cparams("parallel"),
        name="dil_attn_sample",
    )(q, *kv_new, *bufs2, *tabs, *ntabs)


def _sgu_kernel(x_ref, gm_ref, wuv_ref, gs_ref, wmix_ref, bmix_ref, wout_ref, y_ref, *rest, emit_v):
    if emit_v:
        v_ref, mixed_sc = rest
    else:
        (mixed_sc,) = rest
    x = x_ref[...]
    tm, d = x.shape
    h = _rms(x, gm_ref[...]).astype(BF)
    uv = jax.nn.gelu(_dot(h, wuv_ref[...]))
    w = uv.shape[1] // 2
    u = uv[:, :w]
    v = _rms(uv[:, w:], gs_ref[...])
    if emit_v:
        v_ref[...] = v
    vb = v.astype(BF)
    gd = w // SGU_GROUPS
    for c in range(tm // CHUNK):
        rs = slice(c * CHUNK, (c + 1) * CHUNK)
        for g in range(SGU_GROUPS):
            cs = slice(g * gd, (g + 1) * gd)
            mixed_sc[rs, cs] = _dot(wmix_ref[g], vb[rs, cs]) + bmix_ref[:, cs]
    y_ref[...] = x + _dot((u * mixed_sc[...]).astype(BF), wout_ref[...])


def _sgu(x, gm, w_uv, g_sgu, wmix, bmix, w_out, *, emit_v, tm=1024):
    t, d = x.shape
    tm = min(tm, t)
    w = w_uv.shape[1] // 2
    row = lambda i: (i, 0)
    out_specs = [pl.BlockSpec((tm, d), row)]
    out_shape = [jax.ShapeDtypeStruct((t, d), F32)]
    if emit_v:
        out_specs.append(pl.BlockSpec((tm, w), row))
        out_shape.append(jax.ShapeDtypeStruct((t, w), F32))
    consts = [gm, w_uv, g_sgu, wmix, bmix, w_out]
    return pl.pallas_call(
        functools.partial(_sgu_kernel, emit_v=emit_v),
        grid=(t // tm,),
        in_specs=[pl.BlockSpec((tm, d), row)] + [_const_spec(c.shape) for c in consts],
        out_specs=out_specs,
        out_shape=out_shape,
        scratch_shapes=[pltpu.VMEM((tm, w), F32)],
        compiler_params=_cparams("parallel"),
        name="sgu",
    )(x, *consts)


def _rot_half_cols(w):
    half = w.shape[-1] // 2
    return jnp.concatenate([-w[..., half:], w[..., :half]], axis=-1)


def _pad_last(w, n):
    return jnp.pad(w, [(0, 0)] * (w.ndim - 1) + [(0, n - w.shape[-1])])


def _rope_tables(pos):
    half = ROPE_DIM // 2
    inv = ROPE_THETA ** (-jnp.arange(half, dtype=F32) / half)
    ang = pos.astype(F32)[:, None] * inv[None, :]
    cos, sin = jnp.cos(ang), jnp.sin(ang)
    return (_pad_last(jnp.concatenate([cos, cos], axis=1), LANES),
            _pad_last(jnp.concatenate([sin, sin], axis=1), LANES))


def _t5_bucket(dist):
    max_exact = N_BUCKETS // 2
    n = jnp.maximum(dist, 1).astype(F32)
    large = max_exact + (jnp.log(n / max_exact) / math.log(MAX_DISTANCE / max_exact)
                         * (N_BUCKETS - max_exact)).astype(jnp.int32)
    return jnp.where(dist < max_exact, dist, jnp.minimum(large, N_BUCKETS - 1))


def _group_bias(rel_bias, g):
    n_keys = WINDOWS[g] // DILATIONS[g] + 1
    dist = DILATIONS[g] * jnp.arange(n_keys, dtype=jnp.int32)
    return rel_bias[_t5_bucket(dist)][:, g * HPG:(g + 1) * HPG].T


def _prompt_bias_table(bias_g):
    period = 3 * WIN_TILE
    line = jnp.concatenate([bias_g[:, ::-1], jnp.full((HPG, period - WIN_TILE - 1), NEG, F32)], axis=1)
    flat = jnp.tile(line, (1, WIN_TILE))[:, :WIN_TILE * (period - 1)]
    return flat.reshape(HPG, WIN_TILE, period - 1)[:, :, :2 * WIN_TILE]


def _sample_bias_tables(bias_g, g, s_new):
    win, dil = WINDOWS[g], DILATIONS[g]
    n_keys = win // dil + 1
    line = jnp.concatenate([bias_g[:, :, None], jnp.full((HPG, n_keys, dil - 1), NEG, F32)], axis=2)
    line = line.reshape(HPG, n_keys * dil)
    line = jnp.concatenate([line[:, :win + 1], jnp.full((HPG, s_new), NEG, F32)], axis=1)
    buf = jnp.stack([line[:, j + 1: win + j + 1][:, ::-1] for j in range(s_new)])
    buf = jnp.transpose(buf, (2, 1, 0)).reshape(win, HPG * s_new)
    lpad = jnp.concatenate([jnp.full((HPG, s_new - 1), NEG, F32), line[:, :s_new]], axis=1)
    new = jnp.stack([lpad[:, j: j + s_new][:, ::-1] for j in range(s_new)])
    new = jnp.transpose(new, (2, 1, 0)).reshape(s_new, HPG * s_new)
    pad = lambda t, r: jnp.pad(t, ((0, r - t.shape[0]), (0, LANES - t.shape[1])), constant_values=NEG)
    return pad(buf, win), pad(new, LANES)


def _prep_weights(p, s_new):
    w = {}
    w["g_mix0"] = p["norm_mix"][0][None]
    w["w_dq"] = p["w_dq"].astype(BF)
    w["g_q"] = p["g_q"][None]
    q_rank = p["w_uq"].shape[0]
    wuq = p["w_uq"].reshape(q_rank, MLA_HEADS, NOPE_DIM + ROPE_DIM)
    w["w_uq_nope"] = wuq[:, :, :NOPE_DIM].reshape(q_rank, -1).astype(BF)
    wr = wuq[:, :, NOPE_DIM:]
    w["w_uq_rope"] = _pad_last(wr, LANES).reshape(q_rank, -1).astype(BF)
    w["w_uq_rot"] = _pad_last(_rot_half_cols(wr), LANES).reshape(q_rank, -1).astype(BF)
    w["w_uk_t"] = jnp.transpose(p["w_uk"], (1, 2, 0)).astype(BF)
    w["w_dkv"] = _pad_last(p["w_dkv"], KV_RANK + LANES).astype(BF)
    w["w_dkv_rot"] = _pad_last(_rot_half_cols(p["w_dkv"][:, KV_RANK:]), LANES).astype(BF)
    w["g_kv"] = p["g_kv"][None]
    w["w_uv_h"] = jnp.transpose(p["w_uv"], (1, 0, 2)).astype(BF)
    w["w_o_mla"] = p["w_o_mla"].astype(BF)
    w["w_in_conv"] = p["w_in_conv"].astype(BF)
    w["w_out_conv"] = p["w_out_conv"].astype(BF)
    d = p["w_qkv_c"].shape[0]
    wqkv = p["w_qkv_c"].reshape(d, 3, 3, GW)
    cols = [wqkv[:, 0].reshape(d, 3 * GW)]
    for g in range(3):
        cols += [wqkv[:, 1, g], wqkv[:, 2, g]]
    w["w_qkv"] = jnp.concatenate(cols, axis=1).astype(BF)
    w["w_o_c"] = p["w_o_c"].astype(BF)
    biases = [_group_bias(p["rel_bias"], g) for g in range(3)]
    w["bias_p"] = [_prompt_bias_table(b) for b in biases]
    tabs = [_sample_bias_tables(b, g, s_new) for g, b in enumerate(biases)]
    w["tab_s"] = [tb[0] for tb in tabs]
    w["ntab_s"] = [tb[1] for tb in tabs]
    w["w_uv_d"] = p["w_uv_d"].astype(BF)
    w["g_sgu"] = p["g_sgu"][None]
    w["w_out_d"] = p["w_out_d"].astype(BF)
    gd = p["w_uv_d"].shape[1] // 2 // SGU_GROUPS
    tril = jnp.tril(jnp.ones((CHUNK, CHUNK), F32))
    w["wmix_p"] = (p["w_s"] * tril).astype(BF)
    w["bmix_p"] = jnp.repeat(p["b_s"].T, gd, axis=1)
    ws_s = p["w_s"][:, :s_new, :s_new] * jnp.tril(jnp.ones((s_new, s_new), F32))
    eye = jnp.eye(CHUNK // s_new, dtype=F32)
    w["wmix_s"] = jnp.einsum("ab,gij->gaibj", eye, ws_s).reshape(SGU_GROUPS, CHUNK, CHUNK).astype(BF)
    w["bmix_s"] = jnp.repeat(jnp.tile(p["b_s"][:, :s_new].T, (CHUNK // s_new, 1)), gd, axis=1)
    w["w_ffn1"] = p["w_ffn1"].astype(BF)
    w["w_ffn2"] = p["w_ffn2"].astype(BF)
    return w


def kernel(x_prompt, x_sample, cache_mla, page_table, state_conv, state_win1, state_win2, state_win3, norm_mix, norm_ffn, norm_final, w_dq, g_q, w_uq, w_dkv, g_kv, w_uk, w_uv, w_o_mla, w_in_conv, w_conv, w_out_conv, w_qkv_c, w_o_c, rel_bias, w_uv_d, g_sgu, w_s, b_s, w_out_d, w_ffn1, w_ffn2):
    bp, sp, d = x_prompt.shape
    bs, ss, _ = x_sample.shape
    depth = norm_mix.shape[0]
    past_len = page_table.shape[1] * cache_mla.shape[1]
    params = dict(norm_mix=norm_mix, w_dq=w_dq, g_q=g_q, w_uq=w_uq, w_dkv=w_dkv, g_kv=g_kv, w_uk=w_uk,
                  w_uv=w_uv, w_o_mla=w_o_mla, w_in_conv=w_in_conv, w_out_conv=w_out_conv,
                  w_qkv_c=w_qkv_c, w_o_c=w_o_c, rel_bias=rel_bias, w_uv_d=w_uv_d, g_sgu=g_sgu,
                  w_s=w_s, b_s=b_s, w_out_d=w_out_d, w_ffn1=w_ffn1, w_ffn2=w_ffn2)
    w = _prep_weights(params, ss)
    xp = x_prompt.reshape(bp * sp, d)
    xs = x_sample.reshape(bs * ss, d)
    outs = {}
    assert depth == 4, "one layer of each mixer kind"
    gf = norm_final[None]
    gms = [norm_mix[i][None] for i in range(depth)]
    ffn = lambda x, i: _ffn(x, norm_ffn[i][None], w["w_ffn1"], w["w_ffn2"], gf, layer=i,
                            final_norm=(i == depth - 1))

    tm_p = 512
    cos_p, sin_p = _rope_tables(jnp.arange(sp, dtype=jnp.int32))
    ql, qp, rows_p, kc, kp = _mla_proj(xp, cos_p, sin_p, w, pos_blocks=sp // tm_p, q_dtype=BF, tm=tm_p)
    o = _mla_attn_prompt(ql, qp, kc, kp, batch=bp, seq=sp)
    xp = _mla_out(xp, o, w["w_uv_h"], w["w_o_mla"])
    outs["mla_rows_p"] = rows_p.reshape(bp, sp, -1)
    pos_s = past_len + jnp.arange(ss, dtype=jnp.int32)
    cos_s, sin_s = _rope_tables(jnp.tile(pos_s, bs))
    tm_s = min(512, bs * ss)
    ql, qp, rows_s, _, _ = _mla_proj(xs, cos_s, sin_s, w, pos_blocks=bs * ss // tm_s, q_dtype=F32, tm=tm_s)
    outs["mla_rows_s"] = rows_s.reshape(bs, ss, -1)
    o = _mla_attn_sample(ql.reshape(MLA_HEADS, bs, ss, KV_RANK), qp.reshape(MLA_HEADS, bs, ss, LANES),
                         outs["mla_rows_s"], cache_mla, page_table)
    xs = _mla_out(xs, o.reshape(MLA_HEADS, bs * ss, KV_RANK), w["w_uv_h"], w["w_o_mla"])
    xp = ffn(xp, 0)
    xs = ffn(xs, 0)
    zeros_p = jnp.zeros((bp, 2, d), F32)
    xp, outs["conv_p"] = _conv_prompt(xp, zeros_p, gms[1], w["w_in_conv"], w_conv, w["w_out_conv"],
                                      batch=bp, seq=sp)
    xs, outs["conv_s"] = _conv_sample(xs, state_conv, gms[1], w["w_in_conv"], w_conv, w["w_out_conv"],
                                      batch=bs, seq=ss)
    xp = ffn(xp, 1)
    xs = ffn(xs, 1)
    q0, q1, q2, kv0, kv1, kv2, kvt2 = _dil_proj_prompt(xp, gms[2], w["w_qkv"], batch=bp, seq=sp)
    qs_p, kvs = (q0, q1, q2), (kv0, kv1, kv2)
    res = [_dil_attn_prompt(qs_p[g], kvs[g], w["bias_p"][g], g=g, batch=bp, seq=sp) for g in range(3)]
    xp = _dil_comb(xp, [r[0] for r in res], [r[1] for r in res], w["w_o_c"], batch=bp, seq=sp)
    for g in range(2):
        n_last = min(WINDOWS[g], sp)
        last = kvs[g][:, :, sp - n_last:, :]
        outs[f"win{g + 1}_p"] = jnp.transpose(last, (0, 2, 1, 3)).reshape(bp, n_last, 2, HPG, HD)
    n_last = min(WINDOWS[2], sp)
    outs["win3_p"] = kvt2.reshape(bp, sp, 2, HPG, HD)[:, sp - n_last:]
    qs, kn0, kn1, kn2 = _dil_proj(xs, gms[2], w["w_qkv"], q_dtype=F32)
    kns = (kn0, kn1, kn2)
    o = _dil_attn_sample(qs, kns, (state_win1, state_win2, state_win3), w["tab_s"], w["ntab_s"],
                         batch=bs, s_new=ss)
    xs = _proj_res(xs, o, w["w_o_c"])
    for g in range(3):
        outs[f"win{g + 1}_s"] = kns[g].reshape(bs, ss, 2, HPG, HD)
    xp = ffn(xp, 2)
    xs = ffn(xs, 2)
    (xp,) = _sgu(xp, gms[3], w["w_uv_d"], w["g_sgu"], w["wmix_p"], w["bmix_p"], w["w_out_d"], emit_v=False)
    xs, v_s = _sgu(xs, gms[3], w["w_uv_d"], w["g_sgu"], w["wmix_s"], w["bmix_s"], w["w_out_d"], emit_v=True)
    outs["sgu_v_s"] = v_s.reshape(bs, ss, -1)
    xp = ffn(xp, 3)
    xs = ffn(xs, 3)
    return (xp.reshape(bp, sp, d), xs.reshape(bs, ss, d), outs["mla_rows_p"], outs["mla_rows_s"],
            outs["conv_p"], outs["conv_s"], outs["win1_p"], outs["win1_s"], outs["win2_p"],
            outs["win2_s"], outs["win3_p"], outs["win3_s"], outs["sgu_v_s"])
```

```python
import functools
import math

import jax
import jax.numpy as jnp
from jax import lax
from jax.experimental import pallas as pl
from jax.experimental.pallas import tpu as pltpu

EPS = 1e-6
ROPE_THETA = 10000.0
NEG = -1e30
BF = jnp.bfloat16
F32 = jnp.float32
LANES = 128
VMEM_LIMIT = 52 * 1024 * 1024

MLA_HEADS = 8
NOPE_DIM = 128
ROPE_DIM = 64
KV_RANK = 256
V_DIM = 128
WINDOWS = (128, 512, 2048)
DILATIONS = (1, 4, 16)
HPG = 4
HD = 128
N_BUCKETS = 32
MAX_DISTANCE = 2048
CHUNK = 128
SGU_GROUPS = 8
WIN_TILE = 128


def _cparams(*sem):
    return pltpu.CompilerParams(dimension_semantics=sem, vmem_limit_bytes=VMEM_LIMIT)


def _rms(x, g):
    return x * lax.rsqrt(jnp.mean(x * x, axis=-1, keepdims=True) + EPS) * g


def _dot(a, b):
    return jnp.dot(a, b, preferred_element_type=F32)


def _dot_nt(a, b):
    return lax.dot_general(a, b, (((1,), (1,)), ((), ())), preferred_element_type=F32)


def _dot_tn(a, b):
    return lax.dot_general(a, b, (((0,), (0,)), ((), ())), preferred_element_type=F32)


def _const_spec(shape):
    nd = len(shape)
    return pl.BlockSpec(shape, lambda *_: (0,) * nd)


def _ffn_kernel(x_ref, g_ref, w1_ref, w2_ref, gf_ref, o_ref, h_sc, acc_sc, *, nf, final_norm):
    f = pl.program_id(1)

    @pl.when(f == 0)
    def _():
        x = x_ref[...]
        h_sc[...] = _rms(x, g_ref[...]).astype(BF)
        acc_sc[...] = x

    a = _dot(h_sc[...], w1_ref[...])
    a = jnp.maximum(a, 0.0)
    acc_sc[...] += _dot((a * a).astype(BF), w2_ref[...])

    @pl.when(f == nf - 1)
    def _():
        y = acc_sc[...]
        if final_norm:
            y = _rms(y, gf_ref[...])
        o_ref[...] = y


def _ffn(x, g, w1, w2, gf, *, layer, final_norm, tm=1024, tf=2048):
    t, d = x.shape
    dff = w1.shape[2]
    tm = min(tm, t)
    nf = dff // tf
    return pl.pallas_call(
        functools.partial(_ffn_kernel, nf=nf, final_norm=final_norm),
        grid=(t // tm, nf),
        in_specs=[
            pl.BlockSpec((tm, d), lambda i, f: (i, 0)),
            pl.BlockSpec((1, d), lambda i, f: (0, 0)),
            pl.BlockSpec((None, d, tf), lambda i, f: (layer, 0, f)),
            pl.BlockSpec((None, tf, d), lambda i, f: (layer, f, 0)),
            pl.BlockSpec((1, d), lambda i, f: (0, 0)),
        ],
        out_specs=pl.BlockSpec((tm, d), lambda i, f: (i, 0)),
        out_shape=jax.ShapeDtypeStruct((t, d), F32),
        scratch_shapes=[pltpu.VMEM((tm, d), BF), pltpu.VMEM((tm, d), F32)],
        compiler_params=_cparams("parallel", "arbitrary"),
        name="ffn",
    )(x, g, w1, w2, gf)


def _proj_res_kernel(x_ref, a_ref, w_ref, o_ref):
    o_ref[...] = x_ref[...] + _dot(a_ref[...].astype(BF), w_ref[...])


def _proj_res(x, a, w, *, tm=512):
    t, d = x.shape
    k = a.shape[1]
    tm = min(tm, t)
    return pl.pallas_call(
        _proj_res_kernel,
        grid=(t // tm,),
        in_specs=[pl.BlockSpec((tm, d), lambda i: (i, 0)),
                  pl.BlockSpec((tm, k), lambda i: (i, 0)),
                  _const_spec(w.shape)],
        out_specs=pl.BlockSpec((tm, d), lambda i: (i, 0)),
        out_shape=jax.ShapeDtypeStruct((t, d), F32),
        compiler_params=_cparams("parallel"),
        name="proj_res",
    )(x, a, w)


def _mla_proj_kernel(x_ref, cos_ref, sin_ref, gm_ref, wdq_ref, gq_ref, wqn_ref, wqr_ref, wqt_ref,
                     wuk_ref, wkv_ref, wkt_ref, gkv_ref,
                     ql_ref, qp_ref, rows_ref, kc_ref, kp_ref, *, scale):
    h = _rms(x_ref[...], gm_ref[...]).astype(BF)
    cq = _rms(_dot(h, wdq_ref[...]), gq_ref[...]).astype(BF)
    cos = cos_ref[...]
    sin = sin_ref[...]
    q_nope = _dot(cq, wqn_ref[...])
    q_rope = _dot(cq, wqr_ref[...])
    q_rot = _dot(cq, wqt_ref[...])
    for hh in range(MLA_HEADS):
        sl = slice(hh * LANES, (hh + 1) * LANES)
        ql = _dot(q_nope[:, sl].astype(BF), wuk_ref[hh])
        ql_ref[hh] = (ql * scale).astype(ql_ref.dtype)
        qp = q_rope[:, sl] * cos + q_rot[:, sl] * sin
        qp_ref[hh] = (qp * scale).astype(qp_ref.dtype)
    ckv = _dot(h, wkv_ref[...])
    k_rot = _dot(h, wkt_ref[...])
    c = _rms(ckv[:, :KV_RANK], gkv_ref[...])
    kpe = ckv[:, KV_RANK:] * cos + k_rot * sin
    rows_ref[:, :KV_RANK] = c
    rows_ref[:, KV_RANK:] = kpe[:, :ROPE_DIM]
    kc_ref[...] = c.astype(BF)
    kp_ref[...] = kpe.astype(BF)


def _mla_proj(x, cos, sin, w, *, pos_blocks, q_dtype, tm=512):
    t, d = x.shape
    tm = min(tm, t)
    scale = 1.0 / math.sqrt(NOPE_DIM + ROPE_DIM)
    row = lambda i: (i, 0)
    consts = [w["g_mix0"], w["w_dq"], w["g_q"], w["w_uq_nope"], w["w_uq_rope"], w["w_uq_rot"],
              w["w_uk_t"], w["w_dkv"], w["w_dkv_rot"], w["g_kv"]]
    return pl.pallas_call(
        functools.partial(_mla_proj_kernel, scale=scale),
        grid=(t // tm,),
        in_specs=[pl.BlockSpec((tm, d), row),
                  pl.BlockSpec((tm, LANES), lambda i: (i % pos_blocks, 0)),
                  pl.BlockSpec((tm, LANES), lambda i: (i % pos_blocks, 0))]
                 + [_const_spec(c.shape) for c in consts],
        out_specs=[pl.BlockSpec((MLA_HEADS, tm, KV_RANK), lambda i: (0, i, 0)),
                   pl.BlockSpec((MLA_HEADS, tm, LANES), lambda i: (0, i, 0)),
                   pl.BlockSpec((tm, KV_RANK + ROPE_DIM), row),
                   pl.BlockSpec((tm, KV_RANK), row),
                   pl.BlockSpec((tm, LANES), row)],
        out_shape=[jax.ShapeDtypeStruct((MLA_HEADS, t, KV_RANK), q_dtype),
                   jax.ShapeDtypeStruct((MLA_HEADS, t, LANES), q_dtype),
                   jax.ShapeDtypeStruct((t, KV_RANK + ROPE_DIM), F32),
                   jax.ShapeDtypeStruct((t, KV_RANK), BF),
                   jax.ShapeDtypeStruct((t, LANES), BF)],
        compiler_params=_cparams("parallel"),
        name="mla_proj",
    )(x, cos, sin, *consts)


def _lane_rep(x, n):
    return x if n == LANES else jnp.concatenate([x] * (n // LANES), axis=1)


def _softmax_update(s, pv, m_ref, l_ref, acc_ref):
    m_old = m_ref[...]
    m_new = jnp.maximum(m_old, jnp.max(s, axis=-1, keepdims=True))
    alpha = jnp.exp(m_old - m_new)
    p = jnp.exp(s - _lane_rep(m_new, s.shape[1]))
    l_ref[...] = alpha * l_ref[...] + jnp.sum(p, axis=-1, keepdims=True)
    acc_ref[...] = _lane_rep(alpha, acc_ref.shape[1]) * acc_ref[...] + pv(p.astype(BF))
    m_ref[...] = m_new


def _mla_attn_p_kernel(ql_ref, qp_ref, kc_ref, kp_ref, o_ref, m_sc, l_sc, acc_sc, *, tq, tk, hc):
    i = pl.program_id(1)
    m_sc[...] = jnp.full(m_sc.shape, NEG, F32)
    l_sc[...] = jnp.zeros(l_sc.shape, F32)
    acc_sc[...] = jnp.zeros(acc_sc.shape, F32)
    n_full = (i * tq) // tk
    rc = hc * tq

    def tile(start, width, masked):
        start = pl.multiple_of(start, width)
        kc = kc_ref[pl.ds(start, width), :]
        kp = kp_ref[pl.ds(start, width), :]
        for c in range(MLA_HEADS // hc):
            ql = ql_ref[c * hc:(c + 1) * hc].reshape(rc, KV_RANK)
            qp = qp_ref[c * hc:(c + 1) * hc].reshape(rc, LANES)
            s = _dot_nt(ql, kc) + _dot_nt(qp, kp)
            if masked:
                q_pos = i * tq + lax.broadcasted_iota(jnp.int32, s.shape, 0) % tq
                k_pos = start + lax.broadcasted_iota(jnp.int32, s.shape, 1)
                s = jnp.where(k_pos <= q_pos, s, NEG)
            rs = pl.ds(c * rc, rc)
            _softmax_update(s, lambda p: _dot(p, kc), m_sc.at[rs], l_sc.at[rs], acc_sc.at[rs])

    def body(j, carry):
        tile(j * tk, tk, False)
        return carry

    lax.fori_loop(0, n_full, body, 0)
    rem = i * tq - n_full * tk
    if tk == 2 * tq:
        @pl.when(rem == 0)
        def _():
            tile(n_full * tk, tq, True)

        @pl.when(rem != 0)
        def _():
            tile(n_full * tk, tk, True)
    else:
        tile(n_full * tk, tk, True)
    o = acc_sc[...] / _lane_rep(l_sc[...], KV_RANK)
    o_ref[...] = o.reshape(MLA_HEADS, tq, KV_RANK).astype(o_ref.dtype)


def _mla_attn_prompt(ql, qp, kc, kp, *, batch, seq, tq=256, tk=512, hc=2):
    t = batch * seq
    nq = seq // tq
    rows = MLA_HEADS * tq
    qmap = lambda b, i: (0, b * nq + i, 0)
    return pl.pallas_call(
        functools.partial(_mla_attn_p_kernel, tq=tq, tk=tk, hc=hc),
        grid=(batch, nq),
        in_specs=[pl.BlockSpec((MLA_HEADS, tq, KV_RANK), qmap),
                  pl.BlockSpec((MLA_HEADS, tq, LANES), qmap),
                  pl.BlockSpec((seq, KV_RANK), lambda b, i: (b, 0)),
                  pl.BlockSpec((seq, LANES), lambda b, i: (b, 0))],
        out_specs=pl.BlockSpec((MLA_HEADS, tq, KV_RANK), qmap),
        out_shape=jax.ShapeDtypeStruct((MLA_HEADS, t, KV_RANK), BF),
        scratch_shapes=[pltpu.VMEM((rows, LANES), F32), pltpu.VMEM((rows, LANES), F32),
                        pltpu.VMEM((rows, KV_RANK), F32)],
        compiler_params=_cparams("parallel", "arbitrary"),
        name="mla_attn_prompt",
    )(ql, qp, kc, kp)


def _mla_attn_s_kernel(pt_ref, ql_ref, qp_ref, new_ref, *rest, nb, pages, **kw):
    del pt_ref
    o_ref = rest[nb * pages]
    m_sc, l_sc, acc_sc, kc_sc, kp_sc = rest[nb * pages + 1:]
    seqs = [dict(ql_ref=ql_ref.at[:, e], qp_ref=qp_ref.at[:, e], new_ref=new_ref.at[e],
                 cache_refs=rest[e * pages:(e + 1) * pages], o_ref=o_ref.at[:, e], m_sc=m_sc.at[e],
                 l_sc=l_sc.at[e], acc_sc=acc_sc.at[e], kc_sc=kc_sc.at[e], kp_sc=kp_sc.at[e])
            for e in range(nb)]
    step = pl.program_id(1)

    @pl.when(step == 0)
    def _():
        for sq in seqs:
            _mla_decode_init(**sq)

    for sq in seqs:
        _mla_decode_pages(**sq, **kw)

    @pl.when(step == kw["n_steps"] - 1)
    def _():
        for sq in seqs:
            _mla_decode_finish(**sq, **kw)


def _decode_queries(ql_ref, qp_ref, s_new):
    rows = MLA_HEADS * s_new
    return (ql_ref[...].reshape(rows, KV_RANK).astype(BF), qp_ref[...].reshape(rows, LANES).astype(BF))


def _mla_decode_init(*, m_sc, l_sc, acc_sc, kp_sc, **_):
    m_sc[...] = jnp.full(m_sc.shape, NEG, F32)
    l_sc[...] = jnp.zeros(l_sc.shape, F32)
    acc_sc[...] = jnp.zeros(acc_sc.shape, F32)
    kp_sc[ROPE_DIM:, :] = jnp.zeros((LANES - ROPE_DIM, kp_sc.shape[1]), BF)


def _mla_decode_pages(*, ql_ref, qp_ref, cache_refs, m_sc, l_sc, acc_sc, kc_sc, kp_sc, chunk, page_size, s_new,
                      **_):
    ql, qp = _decode_queries(ql_ref, qp_ref, s_new)
    for c in range(len(cache_refs) // chunk):
        for k in range(c * chunk, (c + 1) * chunk):
            kb = cache_refs[k][...]
            sl = slice(k * page_size, (k + 1) * page_size)
            kc_sc[:, sl] = kb[:KV_RANK].astype(BF)
            kp_sc[:ROPE_DIM, sl] = kb[KV_RANK:].astype(BF)
        csl = slice(c * chunk * page_size, (c + 1) * chunk * page_size)
        kc_t = kc_sc[:, csl]
        s = _dot(ql, kc_t) + _dot(qp, kp_sc[:, csl])
        _softmax_update(s, lambda p, kc_t=kc_t: _dot_nt(p, kc_t), m_sc, l_sc, acc_sc)


def _mla_decode_finish(*, ql_ref, qp_ref, new_ref, o_ref, m_sc, l_sc, acc_sc, s_new, **_):
    ql, qp = _decode_queries(ql_ref, qp_ref, s_new)
    pad_r = jnp.zeros((LANES - s_new, KV_RANK + ROPE_DIM), F32)
    nb = jnp.concatenate([new_ref[...], pad_r], axis=0)
    nc = nb[:, :KV_RANK].astype(BF)
    npe = jnp.concatenate([nb[:, KV_RANK:], jnp.zeros((LANES, LANES - ROPE_DIM), F32)], axis=1).astype(BF)
    sn = _dot_nt(ql, nc) + _dot_nt(qp, npe)
    q_pos = lax.broadcasted_iota(jnp.int32, sn.shape, 0) % s_new
    k_pos = lax.broadcasted_iota(jnp.int32, sn.shape, 1)
    sn = jnp.where(k_pos <= q_pos, sn, NEG)
    _softmax_update(sn, lambda p: _dot(p, nc), m_sc, l_sc, acc_sc)
    o = acc_sc[...] / _lane_rep(l_sc[...], KV_RANK)
    o_ref[...] = o.reshape(MLA_HEADS, s_new, KV_RANK)


def _mla_attn_sample(ql, qp, rows_new, cache, page_table, *, nb=1, pages=64, chunk=16):
    _, bs, s_new, _ = ql.shape
    n_pages = page_table.shape[1]
    page_size = cache.shape[1]
    row_w = cache.shape[2]
    nb = nb if bs % nb == 0 else 1
    pages = min(pages, n_pages)
    n_steps = n_pages // pages
    rows = MLA_HEADS * s_new
    pt = page_table.reshape(-1)
    qmap = lambda b, s, pt: (0, b, 0, 0)
    cache_t = jnp.transpose(cache, (0, 2, 1))

    def cache_spec(e, k):
        return pl.BlockSpec((None, row_w, page_size),
                            lambda b, s, pt: (pt[(b * nb + e) * n_pages + s * pages + k], 0, 0))

    grid_spec = pltpu.PrefetchScalarGridSpec(
        num_scalar_prefetch=1,
        grid=(bs // nb, n_steps),
        in_specs=[pl.BlockSpec((MLA_HEADS, nb, s_new, KV_RANK), qmap),
                  pl.BlockSpec((MLA_HEADS, nb, s_new, LANES), qmap),
                  pl.BlockSpec((nb, s_new, row_w), lambda b, s, pt: (b, 0, 0))]
                 + [cache_spec(e, k) for e in range(nb) for k in range(pages)],
        out_specs=pl.BlockSpec((MLA_HEADS, nb, s_new, KV_RANK), qmap),
        scratch_shapes=[pltpu.VMEM((nb, rows, LANES), F32), pltpu.VMEM((nb, rows, LANES), F32),
                        pltpu.VMEM((nb, rows, KV_RANK), F32),
                        pltpu.VMEM((nb, KV_RANK, pages * page_size), BF),
                        pltpu.VMEM((nb, LANES, pages * page_size), BF)],
    )
    return pl.pallas_call(
        functools.partial(_mla_attn_s_kernel, nb=nb, pages=pages, chunk=min(chunk, pages), n_steps=n_steps,
                          page_size=page_size, s_new=s_new),
        grid_spec=grid_spec,
        out_shape=jax.ShapeDtypeStruct((MLA_HEADS, bs, s_new, KV_RANK), F32),
        compiler_params=_cparams("parallel", "arbitrary"),
        name="mla_attn_sample",
    )(pt, ql, qp, rows_new, *([cache_t] * (nb * pages)))


def _mla_out_kernel(x_ref, o_ref, wuv_ref, wo_ref, y_ref):
    vs = [_dot(o_ref[hh].astype(BF), wuv_ref[hh]).astype(BF) for hh in range(MLA_HEADS)]
    v = jnp.concatenate(vs, axis=1)
    y_ref[...] = x_ref[...] + _dot(v, wo_ref[...])


def _mla_out(x, o, w_uv_h, w_o, *, tm=1024):
    t, d = x.shape
    tm = min(tm, t)
    return pl.pallas_call(
        _mla_out_kernel,
        grid=(t // tm,),
        in_specs=[pl.BlockSpec((tm, d), lambda i: (i, 0)),
                  pl.BlockSpec((MLA_HEADS, tm, KV_RANK), lambda i: (0, i, 0)),
                  _const_spec(w_uv_h.shape), _const_spec(w_o.shape)],
        out_specs=pl.BlockSpec((tm, d), lambda i: (i, 0)),
        out_shape=jax.ShapeDtypeStruct((t, d), F32),
        compiler_params=_cparams("parallel"),
        name="mla_out",
    )(x, o, w_uv_h, w_o)


def _conv_kernel(x_ref, st_ref, gm_ref, win_ref, wc_ref, wout_ref, y_ref, cs_ref, carry_sc,
                 *, tiles_per_seq, seq_rows):
    x = x_ref[...]
    tm, d = x.shape
    h = _rms(x, gm_ref[...]).astype(BF)
    bch = _dot(h, win_ref[...])
    gate_b = bch[:, :d]
    z = bch[:, d:2 * d] * bch[:, 2 * d:]
    z1 = pltpu.roll(z, 1, axis=0)
    z2 = pltpu.roll(z, 2, axis=0)
    row = lax.broadcasted_iota(jnp.int32, z.shape, 0)
    if tiles_per_seq is not None:
        i = pl.program_id(0)
        first = (i % tiles_per_seq) == 0
        st = st_ref[...]
        prev2 = jnp.where(first, st[0:1], carry_sc[6:7])
        prev1 = jnp.where(first, st[1:2], carry_sc[7:8])
        z1 = jnp.where(row == 0, prev1, z1)
        z2 = jnp.where(row == 0, prev2, jnp.where(row == 1, prev1, z2))
        carry_sc[...] = z[tm - 8:tm]
        cs_ref[...] = z[tm - 2:tm]
    else:
        nb = tm // seq_rows
        st = st_ref[...]
        j = lax.broadcasted_iota(jnp.int32, (nb, seq_rows, d), 1)
        z3 = z.reshape(nb, seq_rows, d)
        p1 = st[:, 1:2, :]
        p2 = st[:, 0:1, :]
        z1 = jnp.where(j == 0, p1, z1.reshape(nb, seq_rows, d)).reshape(tm, d)
        z2 = jnp.where(j == 0, p2, jnp.where(j == 1, p1, z2.reshape(nb, seq_rows, d))).reshape(tm, d)
        cs_ref[...] = z3[:, seq_rows - 2:, :]
    wc = wc_ref[...]
    y = wc[0:1] * z2 + wc[1:2] * z1 + wc[2:3] * z
    y_ref[...] = x + _dot((gate_b * y).astype(BF), wout_ref[...])


def _conv_prompt(x, state, gm, w_in, w_conv, w_out, *, batch, seq, tm=1024):
    t, d = x.shape
    tps = seq // tm
    return pl.pallas_call(
        functools.partial(_conv_kernel, tiles_per_seq=tps, seq_rows=seq),
        grid=(t // tm,),
        in_specs=[pl.BlockSpec((tm, d), lambda i: (i, 0)),
                  pl.BlockSpec((None, 2, d), lambda i: (i // tps, 0, 0)),
                  _const_spec(gm.shape), _const_spec(w_in.shape), _const_spec(w_conv.shape),
                  _const_spec(w_out.shape)],
        out_specs=[pl.BlockSpec((tm, d), lambda i: (i, 0)),
                   pl.BlockSpec((None, 2, d), lambda i: (i // tps, 0, 0))],
        out_shape=[jax.ShapeDtypeStruct((t, d), F32), jax.ShapeDtypeStruct((batch, 2, d), F32)],
        scratch_shapes=[pltpu.VMEM((8, d), F32)],
        compiler_params=_cparams("arbitrary"),
        name="conv_prompt",
    )(x, state, gm, w_in, w_conv, w_out)


def _conv_sample(x, state, gm, w_in, w_conv, w_out, *, batch, seq, tm=512):
    t, d = x.shape
    tm = min(tm, t)
    nb = tm // seq
    return pl.pallas_call(
        functools.partial(_conv_kernel, tiles_per_seq=None, seq_rows=seq),
        grid=(t // tm,),
        in_specs=[pl.BlockSpec((tm, d), lambda i: (i, 0)),
                  pl.BlockSpec((nb, 2, d), lambda i: (i, 0, 0)),
                  _const_spec(gm.shape), _const_spec(w_in.shape), _const_spec(w_conv.shape),
                  _const_spec(w_out.shape)],
        out_specs=[pl.BlockSpec((tm, d), lambda i: (i, 0)),
                   pl.BlockSpec((nb, 2, d), lambda i: (i, 0, 0))],
        out_shape=[jax.ShapeDtypeStruct((t, d), F32), jax.ShapeDtypeStruct((batch, 2, d), F32)],
        scratch_shapes=[pltpu.VMEM((8, d), F32)],
        compiler_params=_cparams("parallel"),
        name="conv_sample",
    )(x, state, gm, w_in, w_conv, w_out)


GW = HPG * HD


def _dil_proj_kernel(x_ref, gm_ref, w_ref, q_ref, kv0_ref, kv1_ref, kv2_ref, *, scale):
    h = _rms(x_ref[...], gm_ref[...]).astype(BF)
    nq = 3 * GW
    q_ref[...] = (_dot(h, w_ref[:, :nq]) * scale).astype(q_ref.dtype)
    for g, kv_ref in enumerate((kv0_ref, kv1_ref, kv2_ref)):
        kv_ref[...] = _dot(h, w_ref[:, nq + g * 2 * GW: nq + (g + 1) * 2 * GW])


def _dil_proj(x, gm, w_qkv, *, q_dtype, tm=256):
    t, d = x.shape
    tm = min(tm, t)
    row = lambda i: (i, 0)
    return pl.pallas_call(
        functools.partial(_dil_proj_kernel, scale=1.0 / math.sqrt(HD)),
        grid=(t // tm,),
        in_specs=[pl.BlockSpec((tm, d), row), _const_spec(gm.shape), _const_spec(w_qkv.shape)],
        out_specs=[pl.BlockSpec((tm, 3 * GW), row)] + [pl.BlockSpec((tm, 2 * GW), row)] * 3,
        out_shape=[jax.ShapeDtypeStruct((t, 3 * GW), q_dtype)]
                  + [jax.ShapeDtypeStruct((t, 2 * GW), F32)] * 3,
        compiler_params=_cparams("parallel"),
        name="dil_proj",
    )(x, gm, w_qkv)


def _dil_proj_p_kernel(x_ref, gm_ref, w_ref, q0_ref, q1_ref, q2_ref, kv0_ref, kv1_ref, kv2_ref, kvt_ref,
                       *, scale):
    tm = x_ref.shape[0]
    h = _rms(x_ref[...], gm_ref[...]).astype(BF)
    nq = 3 * GW
    q = _dot(h, w_ref[:, :nq]) * scale
    for g, q_ref in enumerate((q0_ref, q1_ref, q2_ref)):
        for s in range(HPG):
            q_ref[s] = q[:, g * GW + s * HD: g * GW + (s + 1) * HD]
    for g, kv_ref in enumerate((kv0_ref, kv1_ref, kv2_ref)):
        kv = _dot(h, w_ref[:, nq + g * 2 * GW: nq + (g + 1) * 2 * GW])
        for s in range(2 * HPG):
            kv_ref[s] = kv[:, s * HD:(s + 1) * HD]
            if g == 2:
                kvt_ref[pl.ds(s, tm, stride=2 * HPG), :] = kv[:, s * HD:(s + 1) * HD]


def _dil_proj_prompt(x, gm, w_qkv, *, batch, seq, tm=512):
    t, d = x.shape
    tps = seq // tm
    slab = lambda i: (i // tps, 0, i % tps, 0)
    return pl.pallas_call(
        functools.partial(_dil_proj_p_kernel, scale=1.0 / math.sqrt(HD)),
        grid=(t // tm,),
        in_specs=[pl.BlockSpec((tm, d), lambda i: (i, 0)), _const_spec(gm.shape), _const_spec(w_qkv.shape)],
        out_specs=[pl.BlockSpec((None, HPG, tm, HD), slab)] * 3
                  + [pl.BlockSpec((None, 2 * HPG, tm, HD), slab)] * 3
                  + [pl.BlockSpec((tm * 2 * HPG, HD), lambda i: (i, 0))],
        out_shape=[jax.ShapeDtypeStruct((batch, HPG, seq, HD), F32)] * 3
                  + [jax.ShapeDtypeStruct((batch, 2 * HPG, seq, HD), F32)] * 3
                  + [jax.ShapeDtypeStruct((t * 2 * HPG, HD), F32)],
        compiler_params=_cparams("parallel"),
        name="dil_proj_prompt",
    )(x, gm, w_qkv)


def _dil_attn_p_kernel(q_ref, kv_ref, bias_ref, o_ref, lse_ref, *, dil, n_i):
    lane_grp = lax.broadcasted_iota(jnp.int32, (WIN_TILE, LANES), 1) // (LANES // HPG)
    col = lax.broadcasted_iota(jnp.int32, (WIN_TILE, 2 * WIN_TILE), 1)
    span = WIN_TILE * dil

    def rows_at(start):
        return pl.ds(start, WIN_TILE) if dil == 1 else pl.ds(start, WIN_TILE, stride=dil)

    def unit(u, carry):
        r = u // n_i
        i = u % n_i
        start = r + i * span
        cur = rows_at(start)
        prev = rows_at(jnp.maximum(start - span, r))
        prev_mask = jnp.where((col < WIN_TILE) & (i == 0), NEG, 0.0)
        lse_tile = jnp.zeros((WIN_TILE, LANES), F32)
        for hh in range(HPG):
            q = q_ref[hh, cur, :].astype(BF)
            k = jnp.concatenate([kv_ref[hh, prev, :], kv_ref[hh, cur, :]], axis=0).astype(BF)
            v = jnp.concatenate([kv_ref[HPG + hh, prev, :], kv_ref[HPG + hh, cur, :]], axis=0).astype(BF)
            s = _dot_nt(q, k) + bias_ref[hh] + prev_mask
            m = jnp.max(s, axis=-1, keepdims=True)
            e = jnp.exp(s - m)
            l = jnp.sum(e, axis=-1, keepdims=True)
            p = (e / l).astype(BF)
            o_ref[hh, cur, :] = _dot(p, v)
            lse_tile = jnp.where(lane_grp == hh, m + jnp.log(l), lse_tile)
        lse_ref[cur, :] = lse_tile
        return carry

    lax.fori_loop(0, dil * n_i, unit, 0, unroll=4)


def _dil_attn_prompt(q, kv, bias, *, g, batch, seq):
    dil = DILATIONS[g]
    n_i = seq // dil // WIN_TILE
    return pl.pallas_call(
        functools.partial(_dil_attn_p_kernel, dil=dil, n_i=n_i),
        grid=(batch,),
        in_specs=[pl.BlockSpec((None, HPG, seq, HD), lambda b: (b, 0, 0, 0)),
                  pl.BlockSpec((None, 2 * HPG, seq, HD), lambda b: (b, 0, 0, 0)),
                  _const_spec(bias.shape)],
        out_specs=[pl.BlockSpec((None, HPG, seq, HD), lambda b: (b, 0, 0, 0)),
                   pl.BlockSpec((None, seq, LANES), lambda b: (b, 0, 0))],
        out_shape=[jax.ShapeDtypeStruct((batch, HPG, seq, HD), F32),
                   jax.ShapeDtypeStruct((batch, seq, LANES), F32)],
        compiler_params=_cparams("parallel"),
        name=f"dil_attn_prompt_g{g}",
    )(q, kv, bias)


def _dil_comb_kernel(x_ref, o0_ref, o1_ref, o2_ref, l0_ref, l1_ref, l2_ref, w_ref, y_ref):
    l0, l1, l2 = l0_ref[...], l1_ref[...], l2_ref[...]
    m = jnp.maximum(jnp.maximum(l0, l1), l2)
    e0, e1, e2 = jnp.exp(l0 - m), jnp.exp(l1 - m), jnp.exp(l2 - m)
    den = e0 + e1 + e2
    ws = (e0 / den, e1 / den, e2 / den)
    tm = x_ref.shape[0]
    parts = []
    for hh in range(HPG):
        acc = jnp.zeros((tm, HD), F32)
        for wg, o_ref in zip(ws, (o0_ref, o1_ref, o2_ref)):
            wcol = wg[:, hh * (LANES // HPG): hh * (LANES // HPG) + 1]
            acc = acc + wcol * o_ref[hh]
        parts.append(acc.astype(BF))
    y_ref[...] = x_ref[...] + _dot(jnp.concatenate(parts, axis=1), w_ref[...])


def _dil_comb(x, os_, lses, w_o, *, batch, seq, tm=512):
    t, d = x.shape
    tps = seq // tm
    row = lambda i: (i, 0)
    return pl.pallas_call(
        _dil_comb_kernel,
        grid=(t // tm,),
        in_specs=[pl.BlockSpec((tm, d), row)]
                 + [pl.BlockSpec((None, HPG, tm, HD), lambda i: (i // tps, 0, i % tps, 0))] * 3
                 + [pl.BlockSpec((None, tm, LANES), lambda i: (i // tps, i % tps, 0))] * 3
                 + [_const_spec(w_o.shape)],
        out_specs=pl.BlockSpec((tm, d), row),
        out_shape=jax.ShapeDtypeStruct((t, d), F32),
        compiler_params=_cparams("parallel"),
        name="dil_comb",
    )(x, *os_, *lses, w_o)


def _dil_attn_s_kernel(q_ref, kn0_ref, kn1_ref, kn2_ref, b0_ref, b1_ref, b2_ref,
                       t0_ref, t1_ref, t2_ref, n0_ref, n1_ref, n2_ref, o_ref, *, s_new, nb):
    for e in range(nb):
        rs = pl.ds(e * s_new, s_new)
        _dil_attn_s_one(q_ref.at[rs], (kn0_ref.at[rs], kn1_ref.at[rs], kn2_ref.at[rs]),
                        (b0_ref.at[e], b1_ref.at[e], b2_ref.at[e]), (t0_ref, t1_ref, t2_ref),
                        (n0_ref, n1_ref, n2_ref), o_ref.at[rs], s_new=s_new)


def _dil_attn_s_one(q_ref, kn_refs, b_refs, t_refs, n_refs, o_ref, *, s_new):
    q = q_ref[...]
    rows_w = lax.broadcasted_iota(jnp.int32, (LANES, GW), 0)
    lanes_w = lax.broadcasted_iota(jnp.int32, (LANES, GW), 1)
    head_sel = (lanes_w // HD) == (rows_w // s_new)
    logits = []
    vals = []
    for g, (kn_ref, b_ref, t_ref, n_ref) in enumerate(zip(kn_refs, b_refs, t_refs, n_refs)):
        qg = q[:, g * GW:(g + 1) * GW]
        wq = jnp.where(head_sel, jnp.tile(qg, (LANES // s_new, 1)), 0.0).astype(BF)
        n_rows = b_ref.shape[0] // (2 * HPG)
        if len(b_ref.shape) == 2:
            slab = lambda s: b_ref[pl.ds(s, n_rows, stride=2 * HPG), :]
        else:
            slab = lambda s: b_ref[:, pl.ds(s, s_new, stride=2 * HPG), :].reshape(-1, HD)
        buf_k = jnp.concatenate([slab(s) for s in range(HPG)], axis=1)
        buf_v = jnp.concatenate([slab(HPG + s) for s in range(HPG)], axis=1)
        new = jnp.concatenate([kn_ref[...], jnp.zeros((LANES - s_new, 2 * GW), F32)], axis=0)
        logits.append(_dot_nt(buf_k.astype(BF), wq) + t_ref[...])
        vals.append(buf_v.astype(BF))
        logits.append(_dot_nt(new[:, :GW].astype(BF), wq) + n_ref[...])
        vals.append(new[:, GW:].astype(BF))
    m = logits[0].max(axis=0, keepdims=True)
    for lg in logits[1:]:
        m = jnp.maximum(m, lg.max(axis=0, keepdims=True))
    ps = [jnp.exp(lg - m) for lg in logits]
    den = ps[0].sum(axis=0, keepdims=True)
    for p in ps[1:]:
        den = den + p.sum(axis=0, keepdims=True)
    inv = 1.0 / den
    out = jnp.zeros((LANES, GW), F32)
    for p, v in zip(ps, vals):
        out = out + _dot_tn((p * inv).astype(BF), v)
    for hh in range(HPG):
        o_ref[:, hh * HD:(hh + 1) * HD] = out[hh * s_new:(hh + 1) * s_new, hh * HD:(hh + 1) * HD]


def _dil_attn_sample(q, kv_new, bufs, tabs, ntabs, *, batch, s_new, nb=2):
    t = batch * s_new
    nb = nb if batch % nb == 0 else 1
    bufs2, buf_specs, tabs = [], [], list(tabs)
    for g, b in enumerate(bufs):
        r, dil = b.shape[1], DILATIONS[g]
        if dil > s_new:
            bufs2.append(b.reshape(batch, r // dil, dil * 2 * HPG, HD))
            buf_specs.append(pl.BlockSpec((nb, r // dil, s_new * 2 * HPG, HD), lambda b: (b, 0, 0, 0)))
            tabs[g] = tabs[g].reshape(r // dil, dil, LANES)[:, :s_new].reshape(-1, LANES)
        else:
            bufs2.append(b.reshape(batch, r * 2 * HPG, HD))
            buf_specs.append(pl.BlockSpec((nb, r * 2 * HPG, HD), lambda b: (b, 0, 0)))
    return pl.pallas_call(
        functools.partial(_dil_attn_s_kernel, s_new=s_new, nb=nb),
        grid=(batch // nb,),
        in_specs=[pl.BlockSpec((nb * s_new, 3 * GW), lambda b: (b, 0))]
                 + [pl.BlockSpec((nb * s_new, 2 * GW), lambda b: (b, 0))] * 3
                 + buf_specs
                 + [_const_spec(tb.shape) for tb in tabs]
                 + [_const_spec(tb.shape) for tb in ntabs],
        out_specs=pl.BlockSpec((nb * s_new, GW), lambda b: (b, 0)),
        out_shape=jax.ShapeDtypeStruct((t, GW), F32),
        compiler_params=_cparams("parallel"),
        name="dil_attn_sample",
    )(q, *kv_new, *bufs2, *tabs, *ntabs)


def _sgu_kernel(x_ref, gm_ref, wuv_ref, gs_ref, wmix_ref, bmix_ref, wout_ref, y_ref, *rest, emit_v):
    if emit_v:
        v_ref, mixed_sc = rest
    else:
        (mixed_sc,) = rest
    x = x_ref[...]
    tm, d = x.shape
    h = _rms(x, gm_ref[...]).astype(BF)
    uv = jax.nn.gelu(_dot(h, wuv_ref[...]))
    w = uv.shape[1] // 2
    u = uv[:, :w]
    v = _rms(uv[:, w:], gs_ref[...])
    if emit_v:
        v_ref[...] = v
    vb = v.astype(BF)
    gd = w // SGU_GROUPS
    for c in range(tm // CHUNK):
        rs = slice(c * CHUNK, (c + 1) * CHUNK)
        for g in range(SGU_GROUPS):
            cs = slice(g * gd, (g + 1) * gd)
            mixed_sc[rs, cs] = _dot(wmix_ref[g], vb[rs, cs]) + bmix_ref[:, cs]
    y_ref[...] = x + _dot((u * mixed_sc[...]).astype(BF), wout_ref[...])


def _sgu(x, gm, w_uv, g_sgu, wmix, bmix, w_out, *, emit_v, tm=1024):
    t, d = x.shape
    tm = min(tm, t)
    w = w_uv.shape[1] // 2
    row = lambda i: (i, 0)
    out_specs = [pl.BlockSpec((tm, d), row)]
    out_shape = [jax.ShapeDtypeStruct((t, d), F32)]
    if emit_v:
        out_specs.append(pl.BlockSpec((tm, w), row))
        out_shape.append(jax.ShapeDtypeStruct((t, w), F32))
    consts = [gm, w_uv, g_sgu, wmix, bmix, w_out]
    return pl.pallas_call(
        functools.partial(_sgu_kernel, emit_v=emit_v),
        grid=(t // tm,),
        in_specs=[pl.BlockSpec((tm, d), row)] + [_const_spec(c.shape) for c in consts],
        out_specs=out_specs,
        out_shape=out_shape,
        scratch_shapes=[pltpu.VMEM((tm, w), F32)],
        compiler_params=_cparams("parallel"),
        name="sgu",
    )(x, *consts)


def _rot_half_cols(w):
    half = w.shape[-1] // 2
    return jnp.concatenate([-w[..., half:], w[..., :half]], axis=-1)


def _pad_last(w, n):
    return jnp.pad(w, [(0, 0)] * (w.ndim - 1) + [(0, n - w.shape[-1])])


def _rope_tables(pos):
    half = ROPE_DIM // 2
    inv = ROPE_THETA ** (-jnp.arange(half, dtype=F32) / half)
    ang = pos.astype(F32)[:, None] * inv[None, :]
    cos, sin = jnp.cos(ang), jnp.sin(ang)
    return (_pad_last(jnp.concatenate([cos, cos], axis=1), LANES),
            _pad_last(jnp.concatenate([sin, sin], axis=1), LANES))


def _t5_bucket(dist):
    max_exact = N_BUCKETS // 2
    n = jnp.maximum(dist, 1).astype(F32)
    large = max_exact + (jnp.log(n / max_exact) / math.log(MAX_DISTANCE / max_exact)
                         * (N_BUCKETS - max_exact)).astype(jnp.int32)
    return jnp.where(dist < max_exact, dist, jnp.minimum(large, N_BUCKETS - 1))


def _group_bias(rel_bias, g):
    n_keys = WINDOWS[g] // DILATIONS[g] + 1
    dist = DILATIONS[g] * jnp.arange(n_keys, dtype=jnp.int32)
    return rel_bias[_t5_bucket(dist)][:, g * HPG:(g + 1) * HPG].T


def _prompt_bias_table(bias_g):
    period = 3 * WIN_TILE
    line = jnp.concatenate([bias_g[:, ::-1], jnp.full((HPG, period - WIN_TILE - 1), NEG, F32)], axis=1)
    flat = jnp.tile(line, (1, WIN_TILE))[:, :WIN_TILE * (period - 1)]
    return flat.reshape(HPG, WIN_TILE, period - 1)[:, :, :2 * WIN_TILE]


def _sample_bias_tables(bias_g, g, s_new):
    win, dil = WINDOWS[g], DILATIONS[g]
    n_keys = win // dil + 1
    line = jnp.concatenate([bias_g[:, :, None], jnp.full((HPG, n_keys, dil - 1), NEG, F32)], axis=2)
    line = line.reshape(HPG, n_keys * dil)
    line = jnp.concatenate([line[:, :win + 1], jnp.full((HPG, s_new), NEG, F32)], axis=1)
    buf = jnp.stack([line[:, j + 1: win + j + 1][:, ::-1] for j in range(s_new)])
    buf = jnp.transpose(buf, (2, 1, 0)).reshape(win, HPG * s_new)
    lpad = jnp.concatenate([jnp.full((HPG, s_new - 1), NEG, F32), line[:, :s_new]], axis=1)
    new = jnp.stack([lpad[:, j: j + s_new][:, ::-1] for j in range(s_new)])
    new = jnp.transpose(new, (2, 1, 0)).reshape(s_new, HPG * s_new)
    pad = lambda t, r: jnp.pad(t, ((0, r - t.shape[0]), (0, LANES - t.shape[1])), constant_values=NEG)
    return pad(buf, win), pad(new, LANES)


def _prep_weights(p, s_new):
    w = {}
    w["g_mix0"] = p["norm_mix"][0][None]
    w["w_dq"] = p["w_dq"].astype(BF)
    w["g_q"] = p["g_q"][None]
    q_rank = p["w_uq"].shape[0]
    wuq = p["w_uq"].reshape(q_rank, MLA_HEADS, NOPE_DIM + ROPE_DIM)
    w["w_uq_nope"] = wuq[:, :, :NOPE_DIM].reshape(q_rank, -1).astype(BF)
    wr = wuq[:, :, NOPE_DIM:]
    w["w_uq_rope"] = _pad_last(wr, LANES).reshape(q_rank, -1).astype(BF)
    w["w_uq_rot"] = _pad_last(_rot_half_cols(wr), LANES).reshape(q_rank, -1).astype(BF)
    w["w_uk_t"] = jnp.transpose(p["w_uk"], (1, 2, 0)).astype(BF)
    w["w_dkv"] = _pad_last(p["w_dkv"], KV_RANK + LANES).astype(BF)
    w["w_dkv_rot"] = _pad_last(_rot_half_cols(p["w_dkv"][:, KV_RANK:]), LANES).astype(BF)
    w["g_kv"] = p["g_kv"][None]
    w["w_uv_h"] = jnp.transpose(p["w_uv"], (1, 0, 2)).astype(BF)
    w["w_o_mla"] = p["w_o_mla"].astype(BF)
    w["w_in_conv"] = p["w_in_conv"].astype(BF)
    w["w_out_conv"] = p["w_out_conv"].astype(BF)
    d = p["w_qkv_c"].shape[0]
    wqkv = p["w_qkv_c"].reshape(d, 3, 3, GW)
    cols = [wqkv[:, 0].reshape(d, 3 * GW)]
    for g in range(3):
        cols += [wqkv[:, 1, g], wqkv[:, 2, g]]
    w["w_qkv"] = jnp.concatenate(cols, axis=1).astype(BF)
    w["w_o_c"] = p["w_o_c"].astype(BF)
    biases = [_group_bias(p["rel_bias"], g) for g in range(3)]
    w["bias_p"] = [_prompt_bias_table(b) for b in biases]
    tabs = [_sample_bias_tables(b, g, s_new) for g, b in enumerate(biases)]
    w["tab_s"] = [tb[0] for tb in tabs]
    w["ntab_s"] = [tb[1] for tb in tabs]
    w["w_uv_d"] = p["w_uv_d"].astype(BF)
    w["g_sgu"] = p["g_sgu"][None]
    w["w_out_d"] = p["w_out_d"].astype(BF)
    gd = p["w_uv_d"].shape[1] // 2 // SGU_GROUPS
    tril = jnp.tril(jnp.ones((CHUNK, CHUNK), F32))
    w["wmix_p"] = (p["w_s"] * tril).astype(BF)
    w["bmix_p"] = jnp.repeat(p["b_s"].T, gd, axis=1)
    ws_s = p["w_s"][:, :s_new, :s_new] * jnp.tril(jnp.ones((s_new, s_new), F32))
    eye = jnp.eye(CHUNK // s_new, dtype=F32)
    w["wmix_s"] = jnp.einsum("ab,gij->gaibj", eye, ws_s).reshape(SGU_GROUPS, CHUNK, CHUNK).astype(BF)
    w["bmix_s"] = jnp.repeat(jnp.tile(p["b_s"][:, :s_new].T, (CHUNK // s_new, 1)), gd, axis=1)
    w["w_ffn1"] = p["w_ffn1"].astype(BF)
    w["w_ffn2"] = p["w_ffn2"].astype(BF)
    return w


def kernel(x_prompt, x_sample, cache_mla, page_table, state_conv, state_win1, state_win2, state_win3, norm_mix, norm_ffn, norm_final, w_dq, g_q, w_uq, w_dkv, g_kv, w_uk, w_uv, w_o_mla, w_in_conv, w_conv, w_out_conv, w_qkv_c, w_o_c, rel_bias, w_uv_d, g_sgu, w_s, b_s, w_out_d, w_ffn1, w_ffn2):
    bp, sp, d = x_prompt.shape
    bs, ss, _ = x_sample.shape
    depth = norm_mix.shape[0]
    past_len = page_table.shape[1] * cache_mla.shape[1]
    params = dict(norm_mix=norm_mix, w_dq=w_dq, g_q=g_q, w_uq=w_uq, w_dkv=w_dkv, g_kv=g_kv, w_uk=w_uk,
                  w_uv=w_uv, w_o_mla=w_o_mla, w_in_conv=w_in_conv, w_out_conv=w_out_conv,
                  w_qkv_c=w_qkv_c, w_o_c=w_o_c, rel_bias=rel_bias, w_uv_d=w_uv_d, g_sgu=g_sgu,
                  w_s=w_s, b_s=b_s, w_out_d=w_out_d, w_ffn1=w_ffn1, w_ffn2=w_ffn2)
    w = _prep_weights(params, ss)
    xp = x_prompt.reshape(bp * sp, d)
    xs = x_sample.reshape(bs * ss, d)
    outs = {}
    assert depth == 4, "one layer of each mixer kind"
    gf = norm_final[None]
    gms = [norm_mix[i][None] for i in range(depth)]
    ffn = lambda x, i: _ffn(x, norm_ffn[i][None], w["w_ffn1"], w["w_ffn2"], gf, layer=i,
                            final_norm=(i == depth - 1))

    tm_p = min(1024, sp)
    cos_p, sin_p = _rope_tables(jnp.arange(sp, dtype=jnp.int32))
    ql, qp, rows_p, kc, kp = _mla_proj(xp, cos_p, sin_p, w, pos_blocks=sp // tm_p, q_dtype=BF, tm=tm_p)
    o = _mla_attn_prompt(ql, qp, kc, kp, batch=bp, seq=sp)
    xp = _mla_out(xp, o, w["w_uv_h"], w["w_o_mla"])
    outs["mla_rows_p"] = rows_p.reshape(bp, sp, -1)
    pos_s = past_len + jnp.arange(ss, dtype=jnp.int32)
    cos_s, sin_s = _rope_tables(jnp.tile(pos_s, bs))
    tm_s = min(512, bs * ss)
    ql, qp, rows_s, _, _ = _mla_proj(xs, cos_s, sin_s, w, pos_blocks=bs * ss // tm_s, q_dtype=F32, tm=tm_s)
    outs["mla_rows_s"] = rows_s.reshape(bs, ss, -1)
    o = _mla_attn_sample(ql.reshape(MLA_HEADS, bs, ss, KV_RANK), qp.reshape(MLA_HEADS, bs, ss, LANES),
                         outs["mla_rows_s"], cache_mla, page_table)
    xs = _mla_out(xs, o.reshape(MLA_HEADS, bs * ss, KV_RANK), w["w_uv_h"], w["w_o_mla"])
    xp = ffn(xp, 0)
    xs = ffn(xs, 0)
    zeros_p = jnp.zeros((bp, 2, d), F32)
    xp, outs["conv_p"] = _conv_prompt(xp, zeros_p, gms[1], w["w_in_conv"], w_conv, w["w_out_conv"],
                                      batch=bp, seq=sp)
    xs, outs["conv_s"] = _conv_sample(xs, state_conv, gms[1], w["w_in_conv"], w_conv, w["w_out_conv"],
                                      batch=bs, seq=ss)
    xp = ffn(xp, 1)
    xs = ffn(xs, 1)
    q0, q1, q2, kv0, kv1, kv2, kvt2 = _dil_proj_prompt(xp, gms[2], w["w_qkv"], batch=bp, seq=sp)
    qs_p, kvs = (q0, q1, q2), (kv0, kv1, kv2)
    res = [_dil_attn_prompt(qs_p[g], kvs[g], w["bias_p"][g], g=g, batch=bp, seq=sp) for g in range(3)]
    xp = _dil_comb(xp, [r[0] for r in res], [r[1] for r in res], w["w_o_c"], batch=bp, seq=sp)
    for g in range(2):
        n_last = min(WINDOWS[g], sp)
        last = kvs[g][:, :, sp - n_last:, :]
        outs[f"win{g + 1}_p"] = jnp.transpose(last, (0, 2, 1, 3)).reshape(bp, n_last, 2, HPG, HD)
    n_last = min(WINDOWS[2], sp)
    outs["win3_p"] = kvt2.reshape(bp, sp, 2, HPG, HD)[:, sp - n_last:]
    qs, kn0, kn1, kn2 = _dil_proj(xs, gms[2], w["w_qkv"], q_dtype=F32)
    kns = (kn0, kn1, kn2)
    o = _dil_attn_sample(qs, kns, (state_win1, state_win2, state_win3), w["tab_s"], w["ntab_s"],
                         batch=bs, s_new=ss)
    xs = _proj_res(xs, o, w["w_o_c"])
    for g in range(3):
        outs[f"win{g + 1}_s"] = kns[g].reshape(bs, ss, 2, HPG, HD)
    xp = ffn(xp, 2)
    xs = ffn(xs, 2)
    (xp,) = _sgu(xp, gms[3], w["w_uv_d"], w["g_sgu"], w["wmix_p"], w["bmix_p"], w["w_out_d"], emit_v=False)
    xs, v_s = _sgu(xs, gms[3], w["w_uv_d"], w["g_sgu"], w["wmix_s"], w["bmix_s"], w["w_out_d"], emit_v=True)
    outs["sgu_v_s"] = v_s.reshape(bs, ss, -1)
    xp = ffn(xp, 3)
    xs = ffn(xs, 3)
    return (xp.reshape(bp, sp, d), xs.reshape(bs, ss, d), outs["mla_rows_p"], outs["mla_rows_s"],
            outs["conv_p"], outs["conv_s"], outs["win1_p"], outs["win1_s"], outs["win2_p"],
            outs["win2_s"], outs["win3_p"], outs["win3_s"], outs["sgu_v_s"])
```

```python
import functools
import math

import jax
import jax.numpy as jnp
from jax import lax
from jax.experimental import pallas as pl
from jax.experimental.pallas import tpu as pltpu

EPS = 1e-6
ROPE_THETA = 10000.0
NEG = -1e30
BF = jnp.bfloat16
F32 = jnp.float32
LANES = 128
VMEM_LIMIT = 52 * 1024 * 1024

MLA_HEADS = 8
NOPE_DIM = 128
ROPE_DIM = 64
KV_RANK = 256
V_DIM = 128
WINDOWS = (128, 512, 2048)
DILATIONS = (1, 4, 16)
HPG = 4
HD = 128
N_BUCKETS = 32
MAX_DISTANCE = 2048
CHUNK = 128
SGU_GROUPS = 8
WIN_TILE = 128


def _cparams(*sem):
    return pltpu.CompilerParams(dimension_semantics=sem, vmem_limit_bytes=VMEM_LIMIT)


def _rms(x, g):
    return x * lax.rsqrt(jnp.mean(x * x, axis=-1, keepdims=True) + EPS) * g


def _dot(a, b):
    return jnp.dot(a, b, preferred_element_type=F32)


def _dot_nt(a, b):
    return lax.dot_general(a, b, (((1,), (1,)), ((), ())), preferred_element_type=F32)


def _dot_tn(a, b):
    return lax.dot_general(a, b, (((0,), (0,)), ((), ())), preferred_element_type=F32)


def _const_spec(shape):
    nd = len(shape)
    return pl.BlockSpec(shape, lambda *_: (0,) * nd)


def _ffn_kernel(x_ref, g_ref, w1_ref, w2_ref, gf_ref, o_ref, h_sc, acc_sc, *, nf, final_norm):
    f = pl.program_id(1)

    @pl.when(f == 0)
    def _():
        x = x_ref[...]
        h_sc[...] = _rms(x, g_ref[...]).astype(BF)
        acc_sc[...] = x

    a = _dot(h_sc[...], w1_ref[...])
    a = jnp.maximum(a, 0.0)
    acc_sc[...] += _dot((a * a).astype(BF), w2_ref[...])

    @pl.when(f == nf - 1)
    def _():
        y = acc_sc[...]
        if final_norm:
            y = _rms(y, gf_ref[...])
        o_ref[...] = y


def _ffn(x, g, w1, w2, gf, *, layer, final_norm, tm=1024, tf=2048):
    t, d = x.shape
    dff = w1.shape[2]
    tm = min(tm, t)
    nf = dff // tf
    return pl.pallas_call(
        functools.partial(_ffn_kernel, nf=nf, final_norm=final_norm),
        grid=(t // tm, nf),
        in_specs=[
            pl.BlockSpec((tm, d), lambda i, f: (i, 0)),
            pl.BlockSpec((1, d), lambda i, f: (0, 0)),
            pl.BlockSpec((None, d, tf), lambda i, f: (layer, 0, f)),
            pl.BlockSpec((None, tf, d), lambda i, f: (layer, f, 0)),
            pl.BlockSpec((1, d), lambda i, f: (0, 0)),
        ],
        out_specs=pl.BlockSpec((tm, d), lambda i, f: (i, 0)),
        out_shape=jax.ShapeDtypeStruct((t, d), F32),
        scratch_shapes=[pltpu.VMEM((tm, d), BF), pltpu.VMEM((tm, d), F32)],
        compiler_params=_cparams("parallel", "arbitrary"),
        name="ffn",
    )(x, g, w1, w2, gf)


def _proj_res_kernel(x_ref, a_ref, w_ref, o_ref):
    o_ref[...] = x_ref[...] + _dot(a_ref[...].astype(BF), w_ref[...])


def _proj_res(x, a, w, *, tm=512):
    t, d = x.shape
    k = a.shape[1]
    tm = min(tm, t)
    return pl.pallas_call(
        _proj_res_kernel,
        grid=(t // tm,),
        in_specs=[pl.BlockSpec((tm, d), lambda i: (i, 0)),
                  pl.BlockSpec((tm, k), lambda i: (i, 0)),
                  _const_spec(w.shape)],
        out_specs=pl.BlockSpec((tm, d), lambda i: (i, 0)),
        out_shape=jax.ShapeDtypeStruct((t, d), F32),
        compiler_params=_cparams("parallel"),
        name="proj_res",
    )(x, a, w)


def _mla_proj_kernel(x_ref, cos_ref, sin_ref, gm_ref, wdq_ref, gq_ref, wqn_ref, wqr_ref, wqt_ref,
                     wuk_ref, wkv_ref, wkt_ref, gkv_ref,
                     ql_ref, qp_ref, rows_ref, kc_ref, kp_ref, *, scale):
    h = _rms(x_ref[...], gm_ref[...]).astype(BF)
    cq = _rms(_dot(h, wdq_ref[...]), gq_ref[...]).astype(BF)
    cos = cos_ref[...]
    sin = sin_ref[...]
    q_nope = _dot(cq, wqn_ref[...])
    q_rope = _dot(cq, wqr_ref[...])
    q_rot = _dot(cq, wqt_ref[...])
    for hh in range(MLA_HEADS):
        sl = slice(hh * LANES, (hh + 1) * LANES)
        ql = _dot(q_nope[:, sl].astype(BF), wuk_ref[hh])
        ql_ref[hh] = (ql * scale).astype(ql_ref.dtype)
        qp = q_rope[:, sl] * cos + q_rot[:, sl] * sin
        qp_ref[hh] = (qp * scale).astype(qp_ref.dtype)
    ckv = _dot(h, wkv_ref[...])
    k_rot = _dot(h, wkt_ref[...])
    c = _rms(ckv[:, :KV_RANK], gkv_ref[...])
    kpe = ckv[:, KV_RANK:] * cos + k_rot * sin
    rows_ref[:, :KV_RANK] = c
    rows_ref[:, KV_RANK:] = kpe[:, :ROPE_DIM]
    kc_ref[...] = c.astype(BF)
    kp_ref[...] = kpe.astype(BF)


def _mla_proj(x, cos, sin, w, *, pos_blocks, q_dtype, tm=512):
    t, d = x.shape
    tm = min(tm, t)
    scale = 1.0 / math.sqrt(NOPE_DIM + ROPE_DIM)
    row = lambda i: (i, 0)
    consts = [w["g_mix0"], w["w_dq"], w["g_q"], w["w_uq_nope"], w["w_uq_rope"], w["w_uq_rot"],
              w["w_uk_t"], w["w_dkv"], w["w_dkv_rot"], w["g_kv"]]
    return pl.pallas_call(
        functools.partial(_mla_proj_kernel, scale=scale),
        grid=(t // tm,),
        in_specs=[pl.BlockSpec((tm, d), row),
                  pl.BlockSpec((tm, LANES), lambda i: (i % pos_blocks, 0)),
                  pl.BlockSpec((tm, LANES), lambda i: (i % pos_blocks, 0))]
                 + [_const_spec(c.shape) for c in consts],
        out_specs=[pl.BlockSpec((MLA_HEADS, tm, KV_RANK), lambda i: (0, i, 0)),
                   pl.BlockSpec((MLA_HEADS, tm, LANES), lambda i: (0, i, 0)),
                   pl.BlockSpec((tm, KV_RANK + ROPE_DIM), row),
                   pl.BlockSpec((tm, KV_RANK), row),
                   pl.BlockSpec((tm, LANES), row)],
        out_shape=[jax.ShapeDtypeStruct((MLA_HEADS, t, KV_RANK), q_dtype),
                   jax.ShapeDtypeStruct((MLA_HEADS, t, LANES), q_dtype),
                   jax.ShapeDtypeStruct((t, KV_RANK + ROPE_DIM), F32),
                   jax.ShapeDtypeStruct((t, KV_RANK), BF),
                   jax.ShapeDtypeStruct((t, LANES), BF)],
        compiler_params=_cparams("parallel"),
        name="mla_proj",
    )(x, cos, sin, *consts)


def _lane_rep(x, n):
    return x if n == LANES else jnp.concatenate([x] * (n // LANES), axis=1)


def _softmax_update(s, pv, m_ref, l_ref, acc_ref):
    m_old = m_ref[...]
    m_new = jnp.maximum(m_old, jnp.max(s, axis=-1, keepdims=True))
    alpha = jnp.exp(m_old - m_new)
    p = jnp.exp(s - _lane_rep(m_new, s.shape[1]))
    l_ref[...] = alpha * l_ref[...] + jnp.sum(p, axis=-1, keepdims=True)
    acc_ref[...] = _lane_rep(alpha, acc_ref.shape[1]) * acc_ref[...] + pv(p.astype(BF))
    m_ref[...] = m_new


def _mla_attn_p_kernel(ql_ref, qp_ref, kc_ref, kp_ref, o_ref, m_sc, l_sc, acc_sc, *, tq, tk, hc):
    i = pl.program_id(1)
    m_sc[...] = jnp.full(m_sc.shape, NEG, F32)
    l_sc[...] = jnp.zeros(l_sc.shape, F32)
    acc_sc[...] = jnp.zeros(acc_sc.shape, F32)
    n_full = (i * tq) // tk
    rc = hc * tq

    def tile(start, width, masked):
        start = pl.multiple_of(start, width)
        kc = kc_ref[pl.ds(start, width), :]
        kp = kp_ref[pl.ds(start, width), :]
        for c in range(MLA_HEADS // hc):
            ql = ql_ref[c * hc:(c + 1) * hc].reshape(rc, KV_RANK)
            qp = qp_ref[c * hc:(c + 1) * hc].reshape(rc, LANES)
            s = _dot_nt(ql, kc) + _dot_nt(qp, kp)
            if masked:
                q_pos = i * tq + lax.broadcasted_iota(jnp.int32, s.shape, 0) % tq
                k_pos = start + lax.broadcasted_iota(jnp.int32, s.shape, 1)
                s = jnp.where(k_pos <= q_pos, s, NEG)
            rs = pl.ds(c * rc, rc)
            _softmax_update(s, lambda p: _dot(p, kc), m_sc.at[rs], l_sc.at[rs], acc_sc.at[rs])

    def body(j, carry):
        tile(j * tk, tk, False)
        return carry

    lax.fori_loop(0, n_full, body, 0)
    rem = i * tq - n_full * tk
    if tk == 2 * tq:
        @pl.when(rem == 0)
        def _():
            tile(n_full * tk, tq, True)

        @pl.when(rem != 0)
        def _():
            tile(n_full * tk, tk, True)
    else:
        tile(n_full * tk, tk, True)
    o = acc_sc[...] / _lane_rep(l_sc[...], KV_RANK)
    o_ref[...] = o.reshape(MLA_HEADS, tq, KV_RANK).astype(o_ref.dtype)


def _mla_attn_prompt(ql, qp, kc, kp, *, batch, seq, tq=256, tk=512, hc=2):
    t = batch * seq
    nq = seq // tq
    rows = MLA_HEADS * tq
    qmap = lambda b, i: (0, b * nq + i, 0)
    return pl.pallas_call(
        functools.partial(_mla_attn_p_kernel, tq=tq, tk=tk, hc=hc),
        grid=(batch, nq),
        in_specs=[pl.BlockSpec((MLA_HEADS, tq, KV_RANK), qmap),
                  pl.BlockSpec((MLA_HEADS, tq, LANES), qmap),
                  pl.BlockSpec((seq, KV_RANK), lambda b, i: (b, 0)),
                  pl.BlockSpec((seq, LANES), lambda b, i: (b, 0))],
        out_specs=pl.BlockSpec((MLA_HEADS, tq, KV_RANK), qmap),
        out_shape=jax.ShapeDtypeStruct((MLA_HEADS, t, KV_RANK), BF),
        scratch_shapes=[pltpu.VMEM((rows, LANES), F32), pltpu.VMEM((rows, LANES), F32),
                        pltpu.VMEM((rows, KV_RANK), F32)],
        compiler_params=_cparams("parallel", "arbitrary"),
        name="mla_attn_prompt",
    )(ql, qp, kc, kp)


def _mla_attn_s_kernel(pt_ref, ql_ref, qp_ref, new_ref, cache_hbm, o_ref, m_sc, l_sc, acc_sc, kc_sc, kp_sc,
                       page_buf, page_sem, *, n_pages, chunk, page_size, s_new):
    b = pl.program_id(0)
    slot = lax.rem(b, 2)

    def page_copy(seq, k, slot_):
        return pltpu.make_async_copy(cache_hbm.at[pt_ref[seq * n_pages + k]], page_buf.at[slot_, k],
                                     page_sem.at[slot_])

    def start_pages(seq, slot_):
        def body(k, carry):
            page_copy(seq, k, slot_).start()
            return carry
        lax.fori_loop(0, n_pages, body, 0, unroll=8)

    @pl.when(b == 0)
    def _():
        start_pages(0, 0)

    @pl.when(b + 1 < pl.num_programs(0))
    def _():
        start_pages(b + 1, 1 - slot)

    for k in range(n_pages):
        page_copy(b, k, slot).wait()

    m_sc[...] = jnp.full(m_sc.shape, NEG, F32)
    l_sc[...] = jnp.zeros(l_sc.shape, F32)
    acc_sc[...] = jnp.zeros(acc_sc.shape, F32)
    kp_sc[ROPE_DIM:, :] = jnp.zeros((LANES - ROPE_DIM, kp_sc.shape[1]), BF)

    rows = MLA_HEADS * s_new
    ql = ql_ref[...].reshape(rows, KV_RANK).astype(BF)
    qp = qp_ref[...].reshape(rows, LANES).astype(BF)
    for c in range(n_pages // chunk):
        for k in range(c * chunk, (c + 1) * chunk):
            kb = page_buf[slot, k]
            sl = slice(k * page_size, (k + 1) * page_size)
            kc_sc[:, sl] = kb[:KV_RANK].astype(BF)
            kp_sc[:ROPE_DIM, sl] = kb[KV_RANK:].astype(BF)
        csl = slice(c * chunk * page_size, (c + 1) * chunk * page_size)
        kc_t = kc_sc[:, csl]
        s = _dot(ql, kc_t) + _dot(qp, kp_sc[:, csl])
        _softmax_update(s, lambda p, kc_t=kc_t: _dot_nt(p, kc_t), m_sc, l_sc, acc_sc)

    pad_r = jnp.zeros((LANES - s_new, KV_RANK + ROPE_DIM), F32)
    nb = jnp.concatenate([new_ref[...], pad_r], axis=0)
    nc = nb[:, :KV_RANK].astype(BF)
    npe = jnp.concatenate([nb[:, KV_RANK:], jnp.zeros((LANES, LANES - ROPE_DIM), F32)], axis=1).astype(BF)
    sn = _dot_nt(ql, nc) + _dot_nt(qp, npe)
    q_pos = lax.broadcasted_iota(jnp.int32, sn.shape, 0) % s_new
    k_pos = lax.broadcasted_iota(jnp.int32, sn.shape, 1)
    sn = jnp.where(k_pos <= q_pos, sn, NEG)
    _softmax_update(sn, lambda p: _dot(p, nc), m_sc, l_sc, acc_sc)
    o = acc_sc[...] / _lane_rep(l_sc[...], KV_RANK)
    o_ref[...] = o.reshape(MLA_HEADS, s_new, KV_RANK)


def _mla_attn_sample(ql, qp, rows_new, cache, page_table, *, chunk=16):
    _, bs, s_new, _ = ql.shape
    n_pages = page_table.shape[1]
    page_size = cache.shape[1]
    row_w = cache.shape[2]
    rows = MLA_HEADS * s_new
    pt = page_table.reshape(-1)
    qmap = lambda b, pt: (0, b, 0, 0)
    cache_t = jnp.transpose(cache, (0, 2, 1))
    grid_spec = pltpu.PrefetchScalarGridSpec(
        num_scalar_prefetch=1,
        grid=(bs,),
        in_specs=[pl.BlockSpec((MLA_HEADS, None, s_new, KV_RANK), qmap),
                  pl.BlockSpec((MLA_HEADS, None, s_new, LANES), qmap),
                  pl.BlockSpec((None, s_new, row_w), lambda b, pt: (b, 0, 0)),
                  pl.BlockSpec(memory_space=pl.ANY)],
        out_specs=pl.BlockSpec((MLA_HEADS, None, s_new, KV_RANK), qmap),
        scratch_shapes=[pltpu.VMEM((rows, LANES), F32), pltpu.VMEM((rows, LANES), F32),
                        pltpu.VMEM((rows, KV_RANK), F32),
                        pltpu.VMEM((KV_RANK, n_pages * page_size), BF),
                        pltpu.VMEM((LANES, n_pages * page_size), BF),
                        pltpu.VMEM((2, n_pages, row_w, page_size), F32),
                        pltpu.SemaphoreType.DMA((2,))],
    )
    return pl.pallas_call(
        functools.partial(_mla_attn_s_kernel, n_pages=n_pages, chunk=min(chunk, n_pages),
                          page_size=page_size, s_new=s_new),
        grid_spec=grid_spec,
        out_shape=jax.ShapeDtypeStruct((MLA_HEADS, bs, s_new, KV_RANK), F32),
        compiler_params=_cparams("arbitrary"),
        name="mla_attn_sample",
    )(pt, ql, qp, rows_new, cache_t)


def _mla_out_kernel(x_ref, o_ref, wuv_ref, wo_ref, y_ref):
    vs = [_dot(o_ref[hh].astype(BF), wuv_ref[hh]).astype(BF) for hh in range(MLA_HEADS)]
    v = jnp.concatenate(vs, axis=1)
    y_ref[...] = x_ref[...] + _dot(v, wo_ref[...])


def _mla_out(x, o, w_uv_h, w_o, *, tm=1024):
    t, d = x.shape
    tm = min(tm, t)
    return pl.pallas_call(
        _mla_out_kernel,
        grid=(t // tm,),
        in_specs=[pl.BlockSpec((tm, d), lambda i: (i, 0)),
                  pl.BlockSpec((MLA_HEADS, tm, KV_RANK), lambda i: (0, i, 0)),
                  _const_spec(w_uv_h.shape), _const_spec(w_o.shape)],
        out_specs=pl.BlockSpec((tm, d), lambda i: (i, 0)),
        out_shape=jax.ShapeDtypeStruct((t, d), F32),
        compiler_params=_cparams("parallel"),
        name="mla_out",
    )(x, o, w_uv_h, w_o)


def _conv_kernel(x_ref, st_ref, gm_ref, win_ref, wc_ref, wout_ref, y_ref, cs_ref, carry_sc,
                 *, tiles_per_seq, seq_rows):
    x = x_ref[...]
    tm, d = x.shape
    h = _rms(x, gm_ref[...]).astype(BF)
    bch = _dot(h, win_ref[...])
    gate_b = bch[:, :d]
    z = bch[:, d:2 * d] * bch[:, 2 * d:]
    z1 = pltpu.roll(z, 1, axis=0)
    z2 = pltpu.roll(z, 2, axis=0)
    row = lax.broadcasted_iota(jnp.int32, z.shape, 0)
    if tiles_per_seq is not None:
        i = pl.program_id(0)
        first = (i % tiles_per_seq) == 0
        st = st_ref[...]
        prev2 = jnp.where(first, st[0:1], carry_sc[6:7])
        prev1 = jnp.where(first, st[1:2], carry_sc[7:8])
        z1 = jnp.where(row == 0, prev1, z1)
        z2 = jnp.where(row == 0, prev2, jnp.where(row == 1, prev1, z2))
        carry_sc[...] = z[tm - 8:tm]
        cs_ref[...] = z[tm - 2:tm]
    else:
        nb = tm // seq_rows
        st = st_ref[...]
        j = lax.broadcasted_iota(jnp.int32, (nb, seq_rows, d), 1)
        z3 = z.reshape(nb, seq_rows, d)
        p1 = st[:, 1:2, :]
        p2 = st[:, 0:1, :]
        z1 = jnp.where(j == 0, p1, z1.reshape(nb, seq_rows, d)).reshape(tm, d)
        z2 = jnp.where(j == 0, p2, jnp.where(j == 1, p1, z2.reshape(nb, seq_rows, d))).reshape(tm, d)
        cs_ref[...] = z3[:, seq_rows - 2:, :]
    wc = wc_ref[...]
    y = wc[0:1] * z2 + wc[1:2] * z1 + wc[2:3] * z
    y_ref[...] = x + _dot((gate_b * y).astype(BF), wout_ref[...])


def _conv_prompt(x, state, gm, w_in, w_conv, w_out, *, batch, seq, tm=1024):
    t, d = x.shape
    tps = seq // tm
    return pl.pallas_call(
        functools.partial(_conv_kernel, tiles_per_seq=tps, seq_rows=seq),
        grid=(t // tm,),
        in_specs=[pl.BlockSpec((tm, d), lambda i: (i, 0)),
                  pl.BlockSpec((None, 2, d), lambda i: (i // tps, 0, 0)),
                  _const_spec(gm.shape), _const_spec(w_in.shape), _const_spec(w_conv.shape),
                  _const_spec(w_out.shape)],
        out_specs=[pl.BlockSpec((tm, d), lambda i: (i, 0)),
                   pl.BlockSpec((None, 2, d), lambda i: (i // tps, 0, 0))],
        out_shape=[jax.ShapeDtypeStruct((t, d), F32), jax.ShapeDtypeStruct((batch, 2, d), F32)],
        scratch_shapes=[pltpu.VMEM((8, d), F32)],
        compiler_params=_cparams("arbitrary"),
        name="conv_prompt",
    )(x, state, gm, w_in, w_conv, w_out)


def _conv_sample(x, state, gm, w_in, w_conv, w_out, *, batch, seq, tm=512):
    t, d = x.shape
    tm = min(tm, t)
    nb = tm // seq
    return pl.pallas_call(
        functools.partial(_conv_kernel, tiles_per_seq=None, seq_rows=seq),
        grid=(t // tm,),
        in_specs=[pl.BlockSpec((tm, d), lambda i: (i, 0)),
                  pl.BlockSpec((nb, 2, d), lambda i: (i, 0, 0)),
                  _const_spec(gm.shape), _const_spec(w_in.shape), _const_spec(w_conv.shape),
                  _const_spec(w_out.shape)],
        out_specs=[pl.BlockSpec((tm, d), lambda i: (i, 0)),
                   pl.BlockSpec((nb, 2, d), lambda i: (i, 0, 0))],
        out_shape=[jax.ShapeDtypeStruct((t, d), F32), jax.ShapeDtypeStruct((batch, 2, d), F32)],
        scratch_shapes=[pltpu.VMEM((8, d), F32)],
        compiler_params=_cparams("parallel"),
        name="conv_sample",
    )(x, state, gm, w_in, w_conv, w_out)


GW = HPG * HD


def _dil_proj_kernel(x_ref, gm_ref, w_ref, q_ref, kv0_ref, kv1_ref, kv2_ref, *, scale):
    h = _rms(x_ref[...], gm_ref[...]).astype(BF)
    nq = 3 * GW
    q_ref[...] = (_dot(h, w_ref[:, :nq]) * scale).astype(q_ref.dtype)
    for g, kv_ref in enumerate((kv0_ref, kv1_ref, kv2_ref)):
        kv_ref[...] = _dot(h, w_ref[:, nq + g * 2 * GW: nq + (g + 1) * 2 * GW])


def _dil_proj(x, gm, w_qkv, *, q_dtype, tm=256):
    t, d = x.shape
    tm = min(tm, t)
    row = lambda i: (i, 0)
    return pl.pallas_call(
        functools.partial(_dil_proj_kernel, scale=1.0 / math.sqrt(HD)),
        grid=(t // tm,),
        in_specs=[pl.BlockSpec((tm, d), row), _const_spec(gm.shape), _const_spec(w_qkv.shape)],
        out_specs=[pl.BlockSpec((tm, 3 * GW), row)] + [pl.BlockSpec((tm, 2 * GW), row)] * 3,
        out_shape=[jax.ShapeDtypeStruct((t, 3 * GW), q_dtype)]
                  + [jax.ShapeDtypeStruct((t, 2 * GW), F32)] * 3,
        compiler_params=_cparams("parallel"),
        name="dil_proj",
    )(x, gm, w_qkv)


def _dil_proj_p_kernel(x_ref, gm_ref, w_ref, q0_ref, q1_ref, q2_ref, kv0_ref, kv1_ref, kv2_ref, kvt_ref,
                       *, scale):
    tm = x_ref.shape[0]
    h = _rms(x_ref[...], gm_ref[...]).astype(BF)
    nq = 3 * GW
    q = _dot(h, w_ref[:, :nq]) * scale
    for g, q_ref in enumerate((q0_ref, q1_ref, q2_ref)):
        for s in range(HPG):
            q_ref[s] = q[:, g * GW + s * HD: g * GW + (s + 1) * HD]
    for g, kv_ref in enumerate((kv0_ref, kv1_ref, kv2_ref)):
        kv = _dot(h, w_ref[:, nq + g * 2 * GW: nq + (g + 1) * 2 * GW])
        for s in range(2 * HPG):
            kv_ref[s] = kv[:, s * HD:(s + 1) * HD]
            if g == 2:
                kvt_ref[pl.ds(s, tm, stride=2 * HPG), :] = kv[:, s * HD:(s + 1) * HD]


def _dil_proj_prompt(x, gm, w_qkv, *, batch, seq, tm=512):
    t, d = x.shape
    tps = seq // tm
    slab = lambda i: (i // tps, 0, i % tps, 0)
    return pl.pallas_call(
        functools.partial(_dil_proj_p_kernel, scale=1.0 / math.sqrt(HD)),
        grid=(t // tm,),
        in_specs=[pl.BlockSpec((tm, d), lambda i: (i, 0)), _const_spec(gm.shape), _const_spec(w_qkv.shape)],
        out_specs=[pl.BlockSpec((None, HPG, tm, HD), slab)] * 3
                  + [pl.BlockSpec((None, 2 * HPG, tm, HD), slab)] * 3
                  + [pl.BlockSpec((tm * 2 * HPG, HD), lambda i: (i, 0))],
        out_shape=[jax.ShapeDtypeStruct((batch, HPG, seq, HD), F32)] * 3
                  + [jax.ShapeDtypeStruct((batch, 2 * HPG, seq, HD), F32)] * 3
                  + [jax.ShapeDtypeStruct((t * 2 * HPG, HD), F32)],
        compiler_params=_cparams("parallel"),
        name="dil_proj_prompt",
    )(x, gm, w_qkv)


def _dil_attn_p_kernel(q_ref, kv_ref, bias_ref, o_ref, lse_ref, *, dil, n_i):
    lane_grp = lax.broadcasted_iota(jnp.int32, (WIN_TILE, LANES), 1) // (LANES // HPG)
    col = lax.broadcasted_iota(jnp.int32, (WIN_TILE, 2 * WIN_TILE), 1)
    span = WIN_TILE * dil

    def rows_at(start):
        return pl.ds(start, WIN_TILE) if dil == 1 else pl.ds(start, WIN_TILE, stride=dil)

    def unit(u, carry):
        r = u // n_i
        i = u % n_i
        start = r + i * span
        cur = rows_at(start)
        prev = rows_at(jnp.maximum(start - span, r))
        prev_mask = jnp.where((col < WIN_TILE) & (i == 0), NEG, 0.0)
        lse_tile = jnp.zeros((WIN_TILE, LANES), F32)
        for hh in range(HPG):
            q = q_ref[hh, cur, :].astype(BF)
            k = jnp.concatenate([kv_ref[hh, prev, :], kv_ref[hh, cur, :]], axis=0).astype(BF)
            v = jnp.concatenate([kv_ref[HPG + hh, prev, :], kv_ref[HPG + hh, cur, :]], axis=0).astype(BF)
            s = _dot_nt(q, k) + bias_ref[hh] + prev_mask
            m = jnp.max(s, axis=-1, keepdims=True)
            e = jnp.exp(s - m)
            l = jnp.sum(e, axis=-1, keepdims=True)
            p = (e / l).astype(BF)
            o_ref[hh, cur, :] = _dot(p, v)
            lse_tile = jnp.where(lane_grp == hh, m + jnp.log(l), lse_tile)
        lse_ref[cur, :] = lse_tile
        return carry

    lax.fori_loop(0, dil * n_i, unit, 0, unroll=4)


def _dil_attn_prompt(q, kv, bias, *, g, batch, seq):
    dil = DILATIONS[g]
    n_i = seq // dil // WIN_TILE
    return pl.pallas_call(
        functools.partial(_dil_attn_p_kernel, dil=dil, n_i=n_i),
        grid=(batch,),
        in_specs=[pl.BlockSpec((None, HPG, seq, HD), lambda b: (b, 0, 0, 0)),
                  pl.BlockSpec((None, 2 * HPG, seq, HD), lambda b: (b, 0, 0, 0)),
                  _const_spec(bias.shape)],
        out_specs=[pl.BlockSpec((None, HPG, seq, HD), lambda b: (b, 0, 0, 0)),
                   pl.BlockSpec((None, seq, LANES), lambda b: (b, 0, 0))],
        out_shape=[jax.ShapeDtypeStruct((batch, HPG, seq, HD), F32),
                   jax.ShapeDtypeStruct((batch, seq, LANES), F32)],
        compiler_params=_cparams("parallel"),
        name=f"dil_attn_prompt_g{g}",
    )(q, kv, bias)


def _dil_comb_kernel(x_ref, o0_ref, o1_ref, o2_ref, l0_ref, l1_ref, l2_ref, w_ref, y_ref):
    l0, l1, l2 = l0_ref[...], l1_ref[...], l2_ref[...]
    m = jnp.maximum(jnp.maximum(l0, l1), l2)
    e0, e1, e2 = jnp.exp(l0 - m), jnp.exp(l1 - m), jnp.exp(l2 - m)
    den = e0 + e1 + e2
    ws = (e0 / den, e1 / den, e2 / den)
    tm = x_ref.shape[0]
    parts = []
    for hh in range(HPG):
        acc = jnp.zeros((tm, HD), F32)
        for wg, o_ref in zip(ws, (o0_ref, o1_ref, o2_ref)):
            wcol = wg[:, hh * (LANES // HPG): hh * (LANES // HPG) + 1]
            acc = acc + wcol * o_ref[hh]
        parts.append(acc.astype(BF))
    y_ref[...] = x_ref[...] + _dot(jnp.concatenate(parts, axis=1), w_ref[...])


def _dil_comb(x, os_, lses, w_o, *, batch, seq, tm=512):
    t, d = x.shape
    tps = seq // tm
    row = lambda i: (i, 0)
    return pl.pallas_call(
        _dil_comb_kernel,
        grid=(t // tm,),
        in_specs=[pl.BlockSpec((tm, d), row)]
                 + [pl.BlockSpec((None, HPG, tm, HD), lambda i: (i // tps, 0, i % tps, 0))] * 3
                 + [pl.BlockSpec((None, tm, LANES), lambda i: (i // tps, i % tps, 0))] * 3
                 + [_const_spec(w_o.shape)],
        out_specs=pl.BlockSpec((tm, d), row),
        out_shape=jax.ShapeDtypeStruct((t, d), F32),
        compiler_params=_cparams("parallel"),
        name="dil_comb",
    )(x, *os_, *lses, w_o)


def _dil_attn_s_kernel(q_ref, kn0_ref, kn1_ref, kn2_ref, b0_ref, b1_ref, b2_ref,
                       t0_ref, t1_ref, t2_ref, n0_ref, n1_ref, n2_ref, o_ref, *, s_new, nb):
    for e in range(nb):
        rs = pl.ds(e * s_new, s_new)
        _dil_attn_s_one(q_ref.at[rs], (kn0_ref.at[rs], kn1_ref.at[rs], kn2_ref.at[rs]),
                        (b0_ref.at[e], b1_ref.at[e], b2_ref.at[e]), (t0_ref, t1_ref, t2_ref),
                        (n0_ref, n1_ref, n2_ref), o_ref.at[rs], s_new=s_new)


def _dil_attn_s_one(q_ref, kn_refs, b_refs, t_refs, n_refs, o_ref, *, s_new):
    q = q_ref[...]
    rows_w = lax.broadcasted_iota(jnp.int32, (LANES, GW), 0)
    lanes_w = lax.broadcasted_iota(jnp.int32, (LANES, GW), 1)
    head_sel = (lanes_w // HD) == (rows_w // s_new)
    logits = []
    vals = []
    for g, (kn_ref, b_ref, t_ref, n_ref) in enumerate(zip(kn_refs, b_refs, t_refs, n_refs)):
        qg = q[:, g * GW:(g + 1) * GW]
        wq = jnp.where(head_sel, jnp.tile(qg, (LANES // s_new, 1)), 0.0).astype(BF)
        n_rows = b_ref.shape[0] // (2 * HPG)
        if len(b_ref.shape) == 2:
            slab = lambda s: b_ref[pl.ds(s, n_rows, stride=2 * HPG), :]
        else:
            slab = lambda s: b_ref[:, pl.ds(s, s_new, stride=2 * HPG), :].reshape(-1, HD)
        buf_k = jnp.concatenate([slab(s) for s in range(HPG)], axis=1)
        buf_v = jnp.concatenate([slab(HPG + s) for s in range(HPG)], axis=1)
        new = jnp.concatenate([kn_ref[...], jnp.zeros((LANES - s_new, 2 * GW), F32)], axis=0)
        logits.append(_dot_nt(buf_k.astype(BF), wq) + t_ref[...])
        vals.append(buf_v.astype(BF))
        logits.append(_dot_nt(new[:, :GW].astype(BF), wq) + n_ref[...])
        vals.append(new[:, GW:].astype(BF))
    m = logits[0].max(axis=0, keepdims=True)
    for lg in logits[1:]:
        m = jnp.maximum(m, lg.max(axis=0, keepdims=True))
    ps = [jnp.exp(lg - m) for lg in logits]
    den = ps[0].sum(axis=0, keepdims=True)
    for p in ps[1:]:
        den = den + p.sum(axis=0, keepdims=True)
    inv = 1.0 / den
    out = jnp.zeros((LANES, GW), F32)
    for p, v in zip(ps, vals):
        out = out + _dot_tn((p * inv).astype(BF), v)
    for hh in range(HPG):
        o_ref[:, hh * HD:(hh + 1) * HD] = out[hh * s_new:(hh + 1) * s_new, hh * HD:(hh + 1) * HD]


def _dil_attn_sample(q, kv_new, bufs, tabs, ntabs, *, batch, s_new, nb=2):
    t = batch * s_new
    nb = nb if batch % nb == 0 else 1
    bufs2, buf_specs, tabs = [], [], list(tabs)
    for g, b in enumerate(bufs):
        r, dil = b.shape[1], DILATIONS[g]
        if dil > s_new:
            bufs2.append(b.reshape(batch, r // dil, dil * 2 * HPG, HD))
            buf_specs.append(pl.BlockSpec((nb, r // dil, s_new * 2 * HPG, HD), lambda b: (b, 0, 0, 0)))
            tabs[g] = tabs[g].reshape(r // dil, dil, LANES)[:, :s_new].reshape(-1, LANES)
        else:
            bufs2.append(b.reshape(batch, r * 2 * HPG, HD))
            buf_specs.append(pl.BlockSpec((nb, r * 2 * HPG, HD), lambda b: (b, 0, 0)))
    return pl.pallas_call(
        functools.partial(_dil_attn_s_kernel, s_new=s_new, nb=nb),
        grid=(batch // nb,),
        in_specs=[pl.BlockSpec((nb * s_new, 3 * GW), lambda b: (b, 0))]
                 + [pl.BlockSpec((nb * s_new, 2 * GW), lambda b: (b, 0))] * 3
                 + buf_specs
                 + [_const_spec(tb.shape) for tb in tabs]
                 + [_const_spec(tb.shape) for tb in ntabs],
        out_specs=pl.BlockSpec((nb * s_new, GW), lambda b: (b, 0)),
        out_shape=jax.ShapeDtypeStruct((t, GW), F32),
        compiler_params=_cparams("parallel"),
        name="dil_attn_sample",
    )(q, *kv_new, *bufs2, *tabs, *ntabs)


def _sgu_kernel(x_ref, gm_ref, wuv_ref, gs_ref, wmix_ref, bmix_ref, wout_ref, y_ref, *rest, emit_v):
    if emit_v:
        v_ref, mixed_sc = rest
    else:
        (mixed_sc,) = rest
    x = x_ref[...]
    tm, d = x.shape
    h = _rms(x, gm_ref[...]).astype(BF)
    uv = jax.nn.gelu(_dot(h, wuv_ref[...]))
    w = uv.shape[1] // 2
    u = uv[:, :w]
    v = _rms(uv[:, w:], gs_ref[...])
    if emit_v:
        v_ref[...] = v
    vb = v.astype(BF)
    gd = w // SGU_GROUPS
    for c in range(tm // CHUNK):
        rs = slice(c * CHUNK, (c + 1) * CHUNK)
        for g in range(SGU_GROUPS):
            cs = slice(g * gd, (g + 1) * gd)
            mixed_sc[rs, cs] = _dot(wmix_ref[g], vb[rs, cs]) + bmix_ref[:, cs]
    y_ref[...] = x + _dot((u * mixed_sc[...]).astype(BF), wout_ref[...])


def _sgu(x, gm, w_uv, g_sgu, wmix, bmix, w_out, *, emit_v, tm=1024):
    t, d = x.shape
    tm = min(tm, t)
    w = w_uv.shape[1] // 2
    row = lambda i: (i, 0)
    out_specs = [pl.BlockSpec((tm, d), row)]
    out_shape = [jax.ShapeDtypeStruct((t, d), F32)]
    if emit_v:
        out_specs.append(pl.BlockSpec((tm, w), row))
        out_shape.append(jax.ShapeDtypeStruct((t, w), F32))
    consts = [gm, w_uv, g_sgu, wmix, bmix, w_out]
    return pl.pallas_call(
        functools.partial(_sgu_kernel, emit_v=emit_v),
        grid=(t // tm,),
        in_specs=[pl.BlockSpec((tm, d), row)] + [_const_spec(c.shape) for c in consts],
        out_specs=out_specs,
        out_shape=out_shape,
        scratch_shapes=[pltpu.VMEM((tm, w), F32)],
        compiler_params=_cparams("parallel"),
        name="sgu",
    )(x, *consts)


def _rot_half_cols(w):
    half = w.shape[-1] // 2
    return jnp.concatenate([-w[..., half:], w[..., :half]], axis=-1)


def _pad_last(w, n):
    return jnp.pad(w, [(0, 0)] * (w.ndim - 1) + [(0, n - w.shape[-1])])


def _rope_tables(pos):
    half = ROPE_DIM // 2
    inv = ROPE_THETA ** (-jnp.arange(half, dtype=F32) / half)
    ang = pos.astype(F32)[:, None] * inv[None, :]
    cos, sin = jnp.cos(ang), jnp.sin(ang)
    return (_pad_last(jnp.concatenate([cos, cos], axis=1), LANES),
            _pad_last(jnp.concatenate([sin, sin], axis=1), LANES))


def _t5_bucket(dist):
    max_exact = N_BUCKETS // 2
    n = jnp.maximum(dist, 1).astype(F32)
    large = max_exact + (jnp.log(n / max_exact) / math.log(MAX_DISTANCE / max_exact)
                         * (N_BUCKETS - max_exact)).astype(jnp.int32)
    return jnp.where(dist < max_exact, dist, jnp.minimum(large, N_BUCKETS - 1))


def _group_bias(rel_bias, g):
    n_keys = WINDOWS[g] // DILATIONS[g] + 1
    dist = DILATIONS[g] * jnp.arange(n_keys, dtype=jnp.int32)
    return rel_bias[_t5_bucket(dist)][:, g * HPG:(g + 1) * HPG].T


def _prompt_bias_table(bias_g):
    period = 3 * WIN_TILE
    line = jnp.concatenate([bias_g[:, ::-1], jnp.full((HPG, period - WIN_TILE - 1), NEG, F32)], axis=1)
    flat = jnp.tile(line, (1, WIN_TILE))[:, :WIN_TILE * (period - 1)]
    return flat.reshape(HPG, WIN_TILE, period - 1)[:, :, :2 * WIN_TILE]


def _sample_bias_tables(bias_g, g, s_new):
    win, dil = WINDOWS[g], DILATIONS[g]
    n_keys = win // dil + 1
    line = jnp.concatenate([bias_g[:, :, None], jnp.full((HPG, n_keys, dil - 1), NEG, F32)], axis=2)
    line = line.reshape(HPG, n_keys * dil)
    line = jnp.concatenate([line[:, :win + 1], jnp.full((HPG, s_new), NEG, F32)], axis=1)
    buf = jnp.stack([line[:, j + 1: win + j + 1][:, ::-1] for j in range(s_new)])
    buf = jnp.transpose(buf, (2, 1, 0)).reshape(win, HPG * s_new)
    lpad = jnp.concatenate([jnp.full((HPG, s_new - 1), NEG, F32), line[:, :s_new]], axis=1)
    new = jnp.stack([lpad[:, j: j + s_new][:, ::-1] for j in range(s_new)])
    new = jnp.transpose(new, (2, 1, 0)).reshape(s_new, HPG * s_new)
    pad = lambda t, r: jnp.pad(t, ((0, r - t.shape[0]), (0, LANES - t.shape[1])), constant_values=NEG)
    return pad(buf, win), pad(new, LANES)


def _prep_weights(p, s_new):
    w = {}
    w["g_mix0"] = p["norm_mix"][0][None]
    w["w_dq"] = p["w_dq"].astype(BF)
    w["g_q"] = p["g_q"][None]
    q_rank = p["w_uq"].shape[0]
    wuq = p["w_uq"].reshape(q_rank, MLA_HEADS, NOPE_DIM + ROPE_DIM)
    w["w_uq_nope"] = wuq[:, :, :NOPE_DIM].reshape(q_rank, -1).astype(BF)
    wr = wuq[:, :, NOPE_DIM:]
    w["w_uq_rope"] = _pad_last(wr, LANES).reshape(q_rank, -1).astype(BF)
    w["w_uq_rot"] = _pad_last(_rot_half_cols(wr), LANES).reshape(q_rank, -1).astype(BF)
    w["w_uk_t"] = jnp.transpose(p["w_uk"], (1, 2, 0)).astype(BF)
    w["w_dkv"] = _pad_last(p["w_dkv"], KV_RANK + LANES).astype(BF)
    w["w_dkv_rot"] = _pad_last(_rot_half_cols(p["w_dkv"][:, KV_RANK:]), LANES).astype(BF)
    w["g_kv"] = p["g_kv"][None]
    w["w_uv_h"] = jnp.transpose(p["w_uv"], (1, 0, 2)).astype(BF)
    w["w_o_mla"] = p["w_o_mla"].astype(BF)
    w["w_in_conv"] = p["w_in_conv"].astype(BF)
    w["w_out_conv"] = p["w_out_conv"].astype(BF)
    d = p["w_qkv_c"].shape[0]
    wqkv = p["w_qkv_c"].reshape(d, 3, 3, GW)
    cols = [wqkv[:, 0].reshape(d, 3 * GW)]
    for g in range(3):
        cols += [wqkv[:, 1, g], wqkv[:, 2, g]]
    w["w_qkv"] = jnp.concatenate(cols, axis=1).astype(BF)
    w["w_o_c"] = p["w_o_c"].astype(BF)
    biases = [_group_bias(p["rel_bias"], g) for g in range(3)]
    w["bias_p"] = [_prompt_bias_table(b) for b in biases]
    tabs = [_sample_bias_tables(b, g, s_new) for g, b in enumerate(biases)]
    w["tab_s"] = [tb[0] for tb in tabs]
    w["ntab_s"] = [tb[1] for tb in tabs]
    w["w_uv_d"] = p["w_uv_d"].astype(BF)
    w["g_sgu"] = p["g_sgu"][None]
    w["w_out_d"] = p["w_out_d"].astype(BF)
    gd = p["w_uv_d"].shape[1] // 2 // SGU_GROUPS
    tril = jnp.tril(jnp.ones((CHUNK, CHUNK), F32))
    w["wmix_p"] = (p["w_s"] * tril).astype(BF)
    w["bmix_p"] = jnp.repeat(p["b_s"].T, gd, axis=1)
    ws_s = p["w_s"][:, :s_new, :s_new] * jnp.tril(jnp.ones((s_new, s_new), F32))
    eye = jnp.eye(CHUNK // s_new, dtype=F32)
    w["wmix_s"] = jnp.einsum("ab,gij->gaibj", eye, ws_s).reshape(SGU_GROUPS, CHUNK, CHUNK).astype(BF)
    w["bmix_s"] = jnp.repeat(jnp.tile(p["b_s"][:, :s_new].T, (CHUNK // s_new, 1)), gd, axis=1)
    w["w_ffn1"] = p["w_ffn1"].astype(BF)
    w["w_ffn2"] = p["w_ffn2"].astype(BF)
    return w


def kernel(x_prompt, x_sample, cache_mla, page_table, state_conv, state_win1, state_win2, state_win3, norm_mix, norm_ffn, norm_final, w_dq, g_q, w_uq, w_dkv, g_kv, w_uk, w_uv, w_o_mla, w_in_conv, w_conv, w_out_conv, w_qkv_c, w_o_c, rel_bias, w_uv_d, g_sgu, w_s, b_s, w_out_d, w_ffn1, w_ffn2):
    bp, sp, d = x_prompt.shape
    bs, ss, _ = x_sample.shape
    depth = norm_mix.shape[0]
    past_len = page_table.shape[1] * cache_mla.shape[1]
    params = dict(norm_mix=norm_mix, w_dq=w_dq, g_q=g_q, w_uq=w_uq, w_dkv=w_dkv, g_kv=g_kv, w_uk=w_uk,
                  w_uv=w_uv, w_o_mla=w_o_mla, w_in_conv=w_in_conv, w_out_conv=w_out_conv,
                  w_qkv_c=w_qkv_c, w_o_c=w_o_c, rel_bias=rel_bias, w_uv_d=w_uv_d, g_sgu=g_sgu,
                  w_s=w_s, b_s=b_s, w_out_d=w_out_d, w_ffn1=w_ffn1, w_ffn2=w_ffn2)
    w = _prep_weights(params, ss)
    xp = x_prompt.reshape(bp * sp, d)
    xs = x_sample.reshape(bs * ss, d)
    outs = {}
    assert depth == 4, "one layer of each mixer kind"
    gf = norm_final[None]
    gms = [norm_mix[i][None] for i in range(depth)]
    ffn = lambda x, i: _ffn(x, norm_ffn[i][None], w["w_ffn1"], w["w_ffn2"], gf, layer=i,
                            final_norm=(i == depth - 1))

    tm_p = min(1024, sp)
    cos_p, sin_p = _rope_tables(jnp.arange(sp, dtype=jnp.int32))
    ql, qp, rows_p, kc, kp = _mla_proj(xp, cos_p, sin_p, w, pos_blocks=sp // tm_p, q_dtype=BF, tm=tm_p)
    o = _mla_attn_prompt(ql, qp, kc, kp, batch=bp, seq=sp)
    xp = _mla_out(xp, o, w["w_uv_h"], w["w_o_mla"])
    outs["mla_rows_p"] = rows_p.reshape(bp, sp, -1)
    pos_s = past_len + jnp.arange(ss, dtype=jnp.int32)
    cos_s, sin_s = _rope_tables(jnp.tile(pos_s, bs))
    tm_s = min(512, bs * ss)
    ql, qp, rows_s, _, _ = _mla_proj(xs, cos_s, sin_s, w, pos_blocks=bs * ss // tm_s, q_dtype=F32, tm=tm_s)
    outs["mla_rows_s"] = rows_s.reshape(bs, ss, -1)
    o = _mla_attn_sample(ql.reshape(MLA_HEADS, bs, ss, KV_RANK), qp.reshape(MLA_HEADS, bs, ss, LANES),
                         outs["mla_rows_s"], cache_mla, page_table)
    xs = _mla_out(xs, o.reshape(MLA_HEADS, bs * ss, KV_RANK), w["w_uv_h"], w["w_o_mla"])
    xp = ffn(xp, 0)
    xs = ffn(xs, 0)
    zeros_p = jnp.zeros((bp, 2, d), F32)
    xp, outs["conv_p"] = _conv_prompt(xp, zeros_p, gms[1], w["w_in_conv"], w_conv, w["w_out_conv"],
                                      batch=bp, seq=sp)
    xs, outs["conv_s"] = _conv_sample(xs, state_conv, gms[1], w["w_in_conv"], w_conv, w["w_out_conv"],
                                      batch=bs, seq=ss)
    xp = ffn(xp, 1)
    xs = ffn(xs, 1)
    q0, q1, q2, kv0, kv1, kv2, kvt2 = _dil_proj_prompt(xp, gms[2], w["w_qkv"], batch=bp, seq=sp)
    qs_p, kvs = (q0, q1, q2), (kv0, kv1, kv2)
    res = [_dil_attn_prompt(qs_p[g], kvs[g], w["bias_p"][g], g=g, batch=bp, seq=sp) for g in range(3)]
    xp = _dil_comb(xp, [r[0] for r in res], [r[1] for r in res], w["w_o_c"], batch=bp, seq=sp)
    for g in range(2):
        n_last = min(WINDOWS[g], sp)
        last = kvs[g][:, :, sp - n_last:, :]
        outs[f"win{g + 1}_p"] = jnp.transpose(last, (0, 2, 1, 3)).reshape(bp, n_last, 2, HPG, HD)
    n_last = min(WINDOWS[2], sp)
    outs["win3_p"] = kvt2.reshape(bp, sp, 2, HPG, HD)[:, sp - n_last:]
    qs, kn0, kn1, kn2 = _dil_proj(xs, gms[2], w["w_qkv"], q_dtype=F32)
    kns = (kn0, kn1, kn2)
    o = _dil_attn_sample(qs, kns, (state_win1, state_win2, state_win3), w["tab_s"], w["ntab_s"],
                         batch=bs, s_new=ss)
    xs = _proj_res(xs, o, w["w_o_c"])
    for g in range(3):
        outs[f"win{g + 1}_s"] = kns[g].reshape(bs, ss, 2, HPG, HD)
    xp = ffn(xp, 2)
    xs = ffn(xs, 2)
    (xp,) = _sgu(xp, gms[3], w["w_uv_d"], w["g_sgu"], w["wmix_p"], w["bmix_p"], w["w_out_d"], emit_v=False)
    xs, v_s = _sgu(xs, gms[3], w["w_uv_d"], w["g_sgu"], w["wmix_s"], w["bmix_s"], w["w_out_d"], emit_v=True)
    outs["sgu_v_s"] = v_s.reshape(bs, ss, -1)
    xp = ffn(xp, 3)
    xs = ffn(xs, 3)
    return (xp.reshape(bp, sp, d), xs.reshape(bs, ss, d), outs["mla_rows_p"], outs["mla_rows_s"],
            outs["conv_p"], outs["conv_s"], outs["win1_p"], outs["win1_s"], outs["win2_p"],
            outs["win2_s"], outs["win3_p"], outs["win3_s"], outs["sgu_v_s"])
```

```python
import functools
import math

import jax
import jax.numpy as jnp
from jax import lax
from jax.experimental import pallas as pl
from jax.experimental.pallas import tpu as pltpu

EPS = 1e-6
ROPE_THETA = 10000.0
NEG = -1e30
BF = jnp.bfloat16
F32 = jnp.float32
LANES = 128
VMEM_LIMIT = 52 * 1024 * 1024

MLA_HEADS = 8
NOPE_DIM = 128
ROPE_DIM = 64
KV_RANK = 256
V_DIM = 128
WINDOWS = (128, 512, 2048)
DILATIONS = (1, 4, 16)
HPG = 4
HD = 128
N_BUCKETS = 32
MAX_DISTANCE = 2048
CHUNK = 128
SGU_GROUPS = 8
WIN_TILE = 128
PAGE_BURST = 8


def _cparams(*sem):
    return pltpu.CompilerParams(dimension_semantics=sem, vmem_limit_bytes=VMEM_LIMIT)


def _rms(x, g):
    return x * lax.rsqrt(jnp.mean(x * x, axis=-1, keepdims=True) + EPS) * g


def _dot(a, b):
    return jnp.dot(a, b, preferred_element_type=F32)


def _dot_nt(a, b):
    return lax.dot_general(a, b, (((1,), (1,)), ((), ())), preferred_element_type=F32)


def _dot_tn(a, b):
    return lax.dot_general(a, b, (((0,), (0,)), ((), ())), preferred_element_type=F32)


def _const_spec(shape):
    nd = len(shape)
    return pl.BlockSpec(shape, lambda *_: (0,) * nd)


def _ffn_kernel(x_ref, g_ref, w1_ref, w2_ref, gf_ref, o_ref, h_sc, acc_sc, *, nf, final_norm):
    f = pl.program_id(1)

    @pl.when(f == 0)
    def _():
        x = x_ref[...]
        h_sc[...] = _rms(x, g_ref[...]).astype(BF)
        acc_sc[...] = x

    a = _dot(h_sc[...], w1_ref[...])
    a = jnp.maximum(a, 0.0)
    acc_sc[...] += _dot((a * a).astype(BF), w2_ref[...])

    @pl.when(f == nf - 1)
    def _():
        y = acc_sc[...]
        if final_norm:
            y = _rms(y, gf_ref[...])
        o_ref[...] = y


def _ffn(x, g, w1, w2, gf, *, layer, final_norm, tm=1024, tf=2048):
    t, d = x.shape
    dff = w1.shape[2]
    tm = min(tm, t)
    nf = dff // tf
    return pl.pallas_call(
        functools.partial(_ffn_kernel, nf=nf, final_norm=final_norm),
        grid=(t // tm, nf),
        in_specs=[
            pl.BlockSpec((tm, d), lambda i, f: (i, 0)),
            pl.BlockSpec((1, d), lambda i, f: (0, 0)),
            pl.BlockSpec((None, d, tf), lambda i, f: (layer, 0, f)),
            pl.BlockSpec((None, tf, d), lambda i, f: (layer, f, 0)),
            pl.BlockSpec((1, d), lambda i, f: (0, 0)),
        ],
        out_specs=pl.BlockSpec((tm, d), lambda i, f: (i, 0)),
        out_shape=jax.ShapeDtypeStruct((t, d), F32),
        scratch_shapes=[pltpu.VMEM((tm, d), BF), pltpu.VMEM((tm, d), F32)],
        compiler_params=_cparams("parallel", "arbitrary"),
        name="ffn",
    )(x, g, w1, w2, gf)


def _proj_res_kernel(x_ref, a_ref, w_ref, o_ref):
    o_ref[...] = x_ref[...] + _dot(a_ref[...].astype(BF), w_ref[...])


def _proj_res(x, a, w, *, tm=512):
    t, d = x.shape
    k = a.shape[1]
    tm = min(tm, t)
    return pl.pallas_call(
        _proj_res_kernel,
        grid=(t // tm,),
        in_specs=[pl.BlockSpec((tm, d), lambda i: (i, 0)),
                  pl.BlockSpec((tm, k), lambda i: (i, 0)),
                  _const_spec(w.shape)],
        out_specs=pl.BlockSpec((tm, d), lambda i: (i, 0)),
        out_shape=jax.ShapeDtypeStruct((t, d), F32),
        compiler_params=_cparams("parallel"),
        name="proj_res",
    )(x, a, w)


def _mla_proj_kernel(x_ref, cos_ref, sin_ref, gm_ref, wdq_ref, gq_ref, wqn_ref, wqr_ref, wqt_ref,
                     wuk_ref, wkv_ref, wkt_ref, gkv_ref,
                     ql_ref, qp_ref, rows_ref, kc_ref, kp_ref, *, scale):
    h = _rms(x_ref[...], gm_ref[...]).astype(BF)
    cq = _rms(_dot(h, wdq_ref[...]), gq_ref[...]).astype(BF)
    cos = cos_ref[...]
    sin = sin_ref[...]
    q_nope = _dot(cq, wqn_ref[...])
    q_rope = _dot(cq, wqr_ref[...])
    q_rot = _dot(cq, wqt_ref[...])
    for hh in range(MLA_HEADS):
        sl = slice(hh * LANES, (hh + 1) * LANES)
        ql = _dot(q_nope[:, sl].astype(BF), wuk_ref[hh])
        ql_ref[hh] = (ql * scale).astype(ql_ref.dtype)
        qp = q_rope[:, sl] * cos + q_rot[:, sl] * sin
        qp_ref[hh] = (qp * scale).astype(qp_ref.dtype)
    ckv = _dot(h, wkv_ref[...])
    k_rot = _dot(h, wkt_ref[...])
    c = _rms(ckv[:, :KV_RANK], gkv_ref[...])
    kpe = ckv[:, KV_RANK:] * cos + k_rot * sin
    rows_ref[:, :KV_RANK] = c
    rows_ref[:, KV_RANK:] = kpe[:, :ROPE_DIM]
    kc_ref[...] = c.astype(BF)
    kp_ref[...] = kpe.astype(BF)


def _mla_proj(x, cos, sin, w, *, pos_blocks, q_dtype, tm=512):
    t, d = x.shape
    tm = min(tm, t)
    scale = 1.0 / math.sqrt(NOPE_DIM + ROPE_DIM)
    row = lambda i: (i, 0)
    consts = [w["g_mix0"], w["w_dq"], w["g_q"], w["w_uq_nope"], w["w_uq_rope"], w["w_uq_rot"],
              w["w_uk_t"], w["w_dkv"], w["w_dkv_rot"], w["g_kv"]]
    return pl.pallas_call(
        functools.partial(_mla_proj_kernel, scale=scale),
        grid=(t // tm,),
        in_specs=[pl.BlockSpec((tm, d), row),
                  pl.BlockSpec((tm, LANES), lambda i: (i % pos_blocks, 0)),
                  pl.BlockSpec((tm, LANES), lambda i: (i % pos_blocks, 0))]
                 + [_const_spec(c.shape) for c in consts],
        out_specs=[pl.BlockSpec((MLA_HEADS, tm, KV_RANK), lambda i: (0, i, 0)),
                   pl.BlockSpec((MLA_HEADS, tm, LANES), lambda i: (0, i, 0)),
                   pl.BlockSpec((tm, KV_RANK + ROPE_DIM), row),
                   pl.BlockSpec((tm, KV_RANK), row),
                   pl.BlockSpec((tm, LANES), row)],
        out_shape=[jax.ShapeDtypeStruct((MLA_HEADS, t, KV_RANK), q_dtype),
                   jax.ShapeDtypeStruct((MLA_HEADS, t, LANES), q_dtype),
                   jax.ShapeDtypeStruct((t, KV_RANK + ROPE_DIM), F32),
                   jax.ShapeDtypeStruct((t, KV_RANK), BF),
                   jax.ShapeDtypeStruct((t, LANES), BF)],
        compiler_params=_cparams("parallel"),
        name="mla_proj",
    )(x, cos, sin, *consts)


def _lane_rep(x, n):
    return x if n == LANES else jnp.concatenate([x] * (n // LANES), axis=1)


def _softmax_update(s, pv, m_ref, l_ref, acc_ref):
    m_old = m_ref[...]
    m_new = jnp.maximum(m_old, jnp.max(s, axis=-1, keepdims=True))
    alpha = jnp.exp(m_old - m_new)
    p = jnp.exp(s - _lane_rep(m_new, s.shape[1]))
    l_ref[...] = alpha * l_ref[...] + jnp.sum(p, axis=-1, keepdims=True)
    acc_ref[...] = _lane_rep(alpha, acc_ref.shape[1]) * acc_ref[...] + pv(p.astype(BF))
    m_ref[...] = m_new


def _mla_attn_p_kernel(ql_ref, qp_ref, kc_ref, kp_ref, o_ref, m_sc, l_sc, acc_sc, *, tq, tk, hc):
    i = pl.program_id(1)
    m_sc[...] = jnp.full(m_sc.shape, NEG, F32)
    l_sc[...] = jnp.zeros(l_sc.shape, F32)
    acc_sc[...] = jnp.zeros(acc_sc.shape, F32)
    n_full = (i * tq) // tk
    rc = hc * tq

    def tile(start, width, masked):
        start = pl.multiple_of(start, width)
        kc = kc_ref[pl.ds(start, width), :]
        kp = kp_ref[pl.ds(start, width), :]
        for c in range(MLA_HEADS // hc):
            ql = ql_ref[c * hc:(c + 1) * hc].reshape(rc, KV_RANK)
            qp = qp_ref[c * hc:(c + 1) * hc].reshape(rc, LANES)
            s = _dot_nt(ql, kc) + _dot_nt(qp, kp)
            if masked:
                q_pos = i * tq + lax.broadcasted_iota(jnp.int32, s.shape, 0) % tq
                k_pos = start + lax.broadcasted_iota(jnp.int32, s.shape, 1)
                s = jnp.where(k_pos <= q_pos, s, NEG)
            rs = pl.ds(c * rc, rc)
            _softmax_update(s, lambda p: _dot(p, kc), m_sc.at[rs], l_sc.at[rs], acc_sc.at[rs])

    def body(j, carry):
        tile(j * tk, tk, False)
        return carry

    lax.fori_loop(0, n_full, body, 0)
    rem = i * tq - n_full * tk
    if tk == 2 * tq:
        @pl.when(rem == 0)
        def _():
            tile(n_full * tk, tq, True)

        @pl.when(rem != 0)
        def _():
            tile(n_full * tk, tk, True)
    else:
        tile(n_full * tk, tk, True)
    o = acc_sc[...] / _lane_rep(l_sc[...], KV_RANK)
    o_ref[...] = o.reshape(MLA_HEADS, tq, KV_RANK).astype(o_ref.dtype)


def _mla_attn_prompt(ql, qp, kc, kp, *, batch, seq, tq=256, tk=512, hc=2):
    t = batch * seq
    nq = seq // tq
    rows = MLA_HEADS * tq
    qmap = lambda b, i: (0, b * nq + i, 0)
    return pl.pallas_call(
        functools.partial(_mla_attn_p_kernel, tq=tq, tk=tk, hc=hc),
        grid=(batch, nq),
        in_specs=[pl.BlockSpec((MLA_HEADS, tq, KV_RANK), qmap),
                  pl.BlockSpec((MLA_HEADS, tq, LANES), qmap),
                  pl.BlockSpec((seq, KV_RANK), lambda b, i: (b, 0)),
                  pl.BlockSpec((seq, LANES), lambda b, i: (b, 0))],
        out_specs=pl.BlockSpec((MLA_HEADS, tq, KV_RANK), qmap),
        out_shape=jax.ShapeDtypeStruct((MLA_HEADS, t, KV_RANK), BF),
        scratch_shapes=[pltpu.VMEM((rows, LANES), F32), pltpu.VMEM((rows, LANES), F32),
                        pltpu.VMEM((rows, KV_RANK), F32)],
        compiler_params=_cparams("parallel", "arbitrary"),
        name="mla_attn_prompt",
    )(ql, qp, kc, kp)


def _mla_attn_s_kernel(pt_ref, ql_ref, qp_ref, new_ref, cache_hbm, o_ref, m_sc, l_sc, acc_sc, kc_sc, kp_sc,
                       page_buf, page_sem, *, n_pages, chunk, page_size, s_new):
    b = pl.program_id(0)
    slot = lax.rem(b, 2)

    def page_copy(seq, k, slot_):
        return pltpu.make_async_copy(cache_hbm.at[pt_ref[seq * n_pages + k]], page_buf.at[slot_, k],
                                     page_sem.at[slot_])

    def start_pages(seq, slot_):
        def body(g, carry):
            for j in range(PAGE_BURST):
                page_copy(seq, g * PAGE_BURST + j, slot_).start(priority=j % 2)
            return carry
        lax.fori_loop(0, n_pages // PAGE_BURST, body, 0)

    @pl.when(b == 0)
    def _():
        start_pages(0, 0)

    @pl.when(b + 1 < pl.num_programs(0))
    def _():
        start_pages(b + 1, 1 - slot)

    for k in range(n_pages):
        page_copy(b, k, slot).wait()

    m_sc[...] = jnp.full(m_sc.shape, NEG, F32)
    l_sc[...] = jnp.zeros(l_sc.shape, F32)
    acc_sc[...] = jnp.zeros(acc_sc.shape, F32)
    kp_sc[ROPE_DIM:, :] = jnp.zeros((LANES - ROPE_DIM, kp_sc.shape[1]), BF)

    rows = MLA_HEADS * s_new
    ql = ql_ref[...].reshape(rows, KV_RANK).astype(BF)
    qp = qp_ref[...].reshape(rows, LANES).astype(BF)
    for c in range(n_pages // chunk):
        for k in range(c * chunk, (c + 1) * chunk):
            kb = page_buf[slot, k]
            sl = slice(k * page_size, (k + 1) * page_size)
            kc_sc[:, sl] = kb[:KV_RANK].astype(BF)
            kp_sc[:ROPE_DIM, sl] = kb[KV_RANK:].astype(BF)
        csl = slice(c * chunk * page_size, (c + 1) * chunk * page_size)
        kc_t = kc_sc[:, csl]
        s = _dot(ql, kc_t) + _dot(qp, kp_sc[:, csl])
        _softmax_update(s, lambda p, kc_t=kc_t: _dot_nt(p, kc_t), m_sc, l_sc, acc_sc)

    pad_r = jnp.zeros((LANES - s_new, KV_RANK + ROPE_DIM), F32)
    nb = jnp.concatenate([new_ref[...], pad_r], axis=0)
    nc = nb[:, :KV_RANK].astype(BF)
    npe = jnp.concatenate([nb[:, KV_RANK:], jnp.zeros((LANES, LANES - ROPE_DIM), F32)], axis=1).astype(BF)
    sn = _dot_nt(ql, nc) + _dot_nt(qp, npe)
    q_pos = lax.broadcasted_iota(jnp.int32, sn.shape, 0) % s_new
    k_pos = lax.broadcasted_iota(jnp.int32, sn.shape, 1)
    sn = jnp.where(k_pos <= q_pos, sn, NEG)
    _softmax_update(sn, lambda p: _dot(p, nc), m_sc, l_sc, acc_sc)
    o = acc_sc[...] / _lane_rep(l_sc[...], KV_RANK)
    o_ref[...] = o.reshape(MLA_HEADS, s_new, KV_RANK)


def _mla_attn_sample(ql, qp, rows_new, cache, page_table, *, chunk=16):
    _, bs, s_new, _ = ql.shape
    n_pages = page_table.shape[1]
    page_size = cache.shape[1]
    row_w = cache.shape[2]
    rows = MLA_HEADS * s_new
    assert n_pages % PAGE_BURST == 0 and n_pages % min(chunk, n_pages) == 0
    pt = page_table.reshape(-1)
    qmap = lambda b, pt: (0, b, 0, 0)
    cache_t = jnp.transpose(cache, (0, 2, 1))
    grid_spec = pltpu.PrefetchScalarGridSpec(
        num_scalar_prefetch=1,
        grid=(bs,),
        in_specs=[pl.BlockSpec((MLA_HEADS, None, s_new, KV_RANK), qmap),
                  pl.BlockSpec((MLA_HEADS, None, s_new, LANES), qmap),
                  pl.BlockSpec((None, s_new, row_w), lambda b, pt: (b, 0, 0)),
                  pl.BlockSpec(memory_space=pl.ANY)],
        out_specs=pl.BlockSpec((MLA_HEADS, None, s_new, KV_RANK), qmap),
        scratch_shapes=[pltpu.VMEM((rows, LANES), F32), pltpu.VMEM((rows, LANES), F32),
                        pltpu.VMEM((rows, KV_RANK), F32),
                        pltpu.VMEM((KV_RANK, n_pages * page_size), BF),
                        pltpu.VMEM((LANES, n_pages * page_size), BF),
                        pltpu.VMEM((2, n_pages, row_w, page_size), F32),
                        pltpu.SemaphoreType.DMA((2,))],
    )
    return pl.pallas_call(
        functools.partial(_mla_attn_s_kernel, n_pages=n_pages, chunk=min(chunk, n_pages),
                          page_size=page_size, s_new=s_new),
        grid_spec=grid_spec,
        out_shape=jax.ShapeDtypeStruct((MLA_HEADS, bs, s_new, KV_RANK), F32),
        compiler_params=_cparams("arbitrary"),
        name="mla_attn_sample",
    )(pt, ql, qp, rows_new, cache_t)


def _mla_out_kernel(x_ref, o_ref, wuv_ref, wo_ref, y_ref):
    vs = [_dot(o_ref[hh].astype(BF), wuv_ref[hh]).astype(BF) for hh in range(MLA_HEADS)]
    v = jnp.concatenate(vs, axis=1)
    y_ref[...] = x_ref[...] + _dot(v, wo_ref[...])


def _mla_out(x, o, w_uv_h, w_o, *, tm=1024):
    t, d = x.shape
    tm = min(tm, t)
    return pl.pallas_call(
        _mla_out_kernel,
        grid=(t // tm,),
        in_specs=[pl.BlockSpec((tm, d), lambda i: (i, 0)),
                  pl.BlockSpec((MLA_HEADS, tm, KV_RANK), lambda i: (0, i, 0)),
                  _const_spec(w_uv_h.shape), _const_spec(w_o.shape)],
        out_specs=pl.BlockSpec((tm, d), lambda i: (i, 0)),
        out_shape=jax.ShapeDtypeStruct((t, d), F32),
        compiler_params=_cparams("parallel"),
        name="mla_out",
    )(x, o, w_uv_h, w_o)


def _conv_kernel(x_ref, st_ref, gm_ref, win_ref, wc_ref, wout_ref, y_ref, cs_ref, carry_sc,
                 *, tiles_per_seq, seq_rows):
    x = x_ref[...]
    tm, d = x.shape
    h = _rms(x, gm_ref[...]).astype(BF)
    bch = _dot(h, win_ref[...])
    gate_b = bch[:, :d]
    z = bch[:, d:2 * d] * bch[:, 2 * d:]
    z1 = pltpu.roll(z, 1, axis=0)
    z2 = pltpu.roll(z, 2, axis=0)
    row = lax.broadcasted_iota(jnp.int32, z.shape, 0)
    if tiles_per_seq is not None:
        i = pl.program_id(0)
        first = (i % tiles_per_seq) == 0
        st = st_ref[...]
        prev2 = jnp.where(first, st[0:1], carry_sc[6:7])
        prev1 = jnp.where(first, st[1:2], carry_sc[7:8])
        z1 = jnp.where(row == 0, prev1, z1)
        z2 = jnp.where(row == 0, prev2, jnp.where(row == 1, prev1, z2))
        carry_sc[...] = z[tm - 8:tm]
        cs_ref[...] = z[tm - 2:tm]
    else:
        nb = tm // seq_rows
        st = st_ref[...]
        j = lax.broadcasted_iota(jnp.int32, (nb, seq_rows, d), 1)
        z3 = z.reshape(nb, seq_rows, d)
        p1 = st[:, 1:2, :]
        p2 = st[:, 0:1, :]
        z1 = jnp.where(j == 0, p1, z1.reshape(nb, seq_rows, d)).reshape(tm, d)
        z2 = jnp.where(j == 0, p2, jnp.where(j == 1, p1, z2.reshape(nb, seq_rows, d))).reshape(tm, d)
        cs_ref[...] = z3[:, seq_rows - 2:, :]
    wc = wc_ref[...]
    y = wc[0:1] * z2 + wc[1:2] * z1 + wc[2:3] * z
    y_ref[...] = x + _dot((gate_b * y).astype(BF), wout_ref[...])


def _conv_prompt(x, state, gm, w_in, w_conv, w_out, *, batch, seq, tm=1024):
    t, d = x.shape
    tps = seq // tm
    return pl.pallas_call(
        functools.partial(_conv_kernel, tiles_per_seq=tps, seq_rows=seq),
        grid=(t // tm,),
        in_specs=[pl.BlockSpec((tm, d), lambda i: (i, 0)),
                  pl.BlockSpec((None, 2, d), lambda i: (i // tps, 0, 0)),
                  _const_spec(gm.shape), _const_spec(w_in.shape), _const_spec(w_conv.shape),
                  _const_spec(w_out.shape)],
        out_specs=[pl.BlockSpec((tm, d), lambda i: (i, 0)),
                   pl.BlockSpec((None, 2, d), lambda i: (i // tps, 0, 0))],
        out_shape=[jax.ShapeDtypeStruct((t, d), F32), jax.ShapeDtypeStruct((batch, 2, d), F32)],
        scratch_shapes=[pltpu.VMEM((8, d), F32)],
        compiler_params=_cparams("arbitrary"),
        name="conv_prompt",
    )(x, state, gm, w_in, w_conv, w_out)


def _conv_sample(x, state, gm, w_in, w_conv, w_out, *, batch, seq, tm=512):
    t, d = x.shape
    tm = min(tm, t)
    nb = tm // seq
    return pl.pallas_call(
        functools.partial(_conv_kernel, tiles_per_seq=None, seq_rows=seq),
        grid=(t // tm,),
        in_specs=[pl.BlockSpec((tm, d), lambda i: (i, 0)),
                  pl.BlockSpec((nb, 2, d), lambda i: (i, 0, 0)),
                  _const_spec(gm.shape), _const_spec(w_in.shape), _const_spec(w_conv.shape),
                  _const_spec(w_out.shape)],
        out_specs=[pl.BlockSpec((tm, d), lambda i: (i, 0)),
                   pl.BlockSpec((nb, 2, d), lambda i: (i, 0, 0))],
        out_shape=[jax.ShapeDtypeStruct((t, d), F32), jax.ShapeDtypeStruct((batch, 2, d), F32)],
        scratch_shapes=[pltpu.VMEM((8, d), F32)],
        compiler_params=_cparams("parallel"),
        name="conv_sample",
    )(x, state, gm, w_in, w_conv, w_out)


GW = HPG * HD


def _kv_group(h, w_ref, g):
    k = _dot(h, w_ref[:, (3 + g) * GW:(4 + g) * GW])
    v = _dot(h, w_ref[:, (6 + g) * GW:(7 + g) * GW])
    return jnp.concatenate([k, v], axis=1)


def _dil_proj_kernel(x_ref, gm_ref, w_ref, q_ref, kv0_ref, kv1_ref, kv2_ref, *, scale):
    h = _rms(x_ref[...], gm_ref[...]).astype(BF)
    q_ref[...] = (_dot(h, w_ref[:, :3 * GW]) * scale).astype(q_ref.dtype)
    for g, kv_ref in enumerate((kv0_ref, kv1_ref, kv2_ref)):
        kv_ref[...] = _kv_group(h, w_ref, g)


def _dil_proj(x, gm, w_qkv, *, q_dtype, tm=256):
    t, d = x.shape
    tm = min(tm, t)
    row = lambda i: (i, 0)
    return pl.pallas_call(
        functools.partial(_dil_proj_kernel, scale=1.0 / math.sqrt(HD)),
        grid=(t // tm,),
        in_specs=[pl.BlockSpec((tm, d), row), _const_spec(gm.shape), _const_spec(w_qkv.shape)],
        out_specs=[pl.BlockSpec((tm, 3 * GW), row)] + [pl.BlockSpec((tm, 2 * GW), row)] * 3,
        out_shape=[jax.ShapeDtypeStruct((t, 3 * GW), q_dtype)]
                  + [jax.ShapeDtypeStruct((t, 2 * GW), F32)] * 3,
        compiler_params=_cparams("parallel"),
        name="dil_proj",
    )(x, gm, w_qkv)


def _dil_proj_p_kernel(x_ref, gm_ref, w_ref, q0_ref, q1_ref, q2_ref, kv0_ref, kv1_ref, kv2_ref, kvt_ref,
                       *, scale):
    tm = x_ref.shape[0]
    h = _rms(x_ref[...], gm_ref[...]).astype(BF)
    q = _dot(h, w_ref[:, :3 * GW]) * scale
    for g, q_ref in enumerate((q0_ref, q1_ref, q2_ref)):
        for s in range(HPG):
            q_ref[s] = q[:, g * GW + s * HD: g * GW + (s + 1) * HD]
    for g, kv_ref in enumerate((kv0_ref, kv1_ref, kv2_ref)):
        kv = _kv_group(h, w_ref, g)
        for s in range(2 * HPG):
            kv_ref[s] = kv[:, s * HD:(s + 1) * HD]
            if g == 2:
                kvt_ref[pl.ds(s, tm, stride=2 * HPG), :] = kv[:, s * HD:(s + 1) * HD]


def _dil_proj_prompt(x, gm, w_qkv, *, batch, seq, tm=512):
    t, d = x.shape
    tps = seq // tm
    slab = lambda i: (i // tps, 0, i % tps, 0)
    return pl.pallas_call(
        functools.partial(_dil_proj_p_kernel, scale=1.0 / math.sqrt(HD)),
        grid=(t // tm,),
        in_specs=[pl.BlockSpec((tm, d), lambda i: (i, 0)), _const_spec(gm.shape), _const_spec(w_qkv.shape)],
        out_specs=[pl.BlockSpec((None, HPG, tm, HD), slab)] * 3
                  + [pl.BlockSpec((None, 2 * HPG, tm, HD), slab)] * 3
                  + [pl.BlockSpec((tm * 2 * HPG, HD), lambda i: (i, 0))],
        out_shape=[jax.ShapeDtypeStruct((batch, HPG, seq, HD), F32)] * 3
                  + [jax.ShapeDtypeStruct((batch, 2 * HPG, seq, HD), F32)] * 3
                  + [jax.ShapeDtypeStruct((t * 2 * HPG, HD), F32)],
        compiler_params=_cparams("parallel"),
        name="dil_proj_prompt",
    )(x, gm, w_qkv)


def _dil_attn_p_kernel(q_ref, kv_ref, bias_ref, o_ref, lse_ref, *, dil, n_i):
    lane_grp = lax.broadcasted_iota(jnp.int32, (WIN_TILE, LANES), 1) // (LANES // HPG)
    col = lax.broadcasted_iota(jnp.int32, (WIN_TILE, 2 * WIN_TILE), 1)
    span = WIN_TILE * dil

    def rows_at(start):
        return pl.ds(start, WIN_TILE) if dil == 1 else pl.ds(start, WIN_TILE, stride=dil)

    def unit(u, carry):
        r = u // n_i
        i = u % n_i
        start = r + i * span
        cur = rows_at(start)
        prev = rows_at(jnp.maximum(start - span, r))
        prev_mask = jnp.where((col < WIN_TILE) & (i == 0), NEG, 0.0)
        lse_tile = jnp.zeros((WIN_TILE, LANES), F32)
        for hh in range(HPG):
            q = q_ref[hh, cur, :].astype(BF)
            k = jnp.concatenate([kv_ref[hh, prev, :], kv_ref[hh, cur, :]], axis=0).astype(BF)
            v = jnp.concatenate([kv_ref[HPG + hh, prev, :], kv_ref[HPG + hh, cur, :]], axis=0).astype(BF)
            s = _dot_nt(q, k) + bias_ref[hh] + prev_mask
            m = jnp.max(s, axis=-1, keepdims=True)
            e = jnp.exp(s - m)
            l = jnp.sum(e, axis=-1, keepdims=True)
            p = (e / l).astype(BF)
            o_ref[hh, cur, :] = _dot(p, v)
            lse_tile = jnp.where(lane_grp == hh, m + jnp.log(l), lse_tile)
        lse_ref[cur, :] = lse_tile
        return carry

    lax.fori_loop(0, dil * n_i, unit, 0, unroll=4)


def _dil_attn_prompt(q, kv, bias, *, g, batch, seq):
    dil = DILATIONS[g]
    n_i = seq // dil // WIN_TILE
    return pl.pallas_call(
        functools.partial(_dil_attn_p_kernel, dil=dil, n_i=n_i),
        grid=(batch,),
        in_specs=[pl.BlockSpec((None, HPG, seq, HD), lambda b: (b, 0, 0, 0)),
                  pl.BlockSpec((None, 2 * HPG, seq, HD), lambda b: (b, 0, 0, 0)),
                  _const_spec(bias.shape)],
        out_specs=[pl.BlockSpec((None, HPG, seq, HD), lambda b: (b, 0, 0, 0)),
                   pl.BlockSpec((None, seq, LANES), lambda b: (b, 0, 0))],
        out_shape=[jax.ShapeDtypeStruct((batch, HPG, seq, HD), F32),
                   jax.ShapeDtypeStruct((batch, seq, LANES), F32)],
        compiler_params=_cparams("parallel"),
        name=f"dil_attn_prompt_g{g}",
    )(q, kv, bias)


def _dil_comb_kernel(x_ref, o0_ref, o1_ref, o2_ref, l0_ref, l1_ref, l2_ref, w_ref, y_ref):
    l0, l1, l2 = l0_ref[...], l1_ref[...], l2_ref[...]
    m = jnp.maximum(jnp.maximum(l0, l1), l2)
    e0, e1, e2 = jnp.exp(l0 - m), jnp.exp(l1 - m), jnp.exp(l2 - m)
    den = e0 + e1 + e2
    ws = (e0 / den, e1 / den, e2 / den)
    tm = x_ref.shape[0]
    parts = []
    for hh in range(HPG):
        acc = jnp.zeros((tm, HD), F32)
        for wg, o_ref in zip(ws, (o0_ref, o1_ref, o2_ref)):
            wcol = wg[:, hh * (LANES // HPG): hh * (LANES // HPG) + 1]
            acc = acc + wcol * o_ref[hh]
        parts.append(acc.astype(BF))
    y_ref[...] = x_ref[...] + _dot(jnp.concatenate(parts, axis=1), w_ref[...])


def _dil_comb(x, os_, lses, w_o, *, batch, seq, tm=512):
    t, d = x.shape
    tps = seq // tm
    row = lambda i: (i, 0)
    return pl.pallas_call(
        _dil_comb_kernel,
        grid=(t // tm,),
        in_specs=[pl.BlockSpec((tm, d), row)]
                 + [pl.BlockSpec((None, HPG, tm, HD), lambda i: (i // tps, 0, i % tps, 0))] * 3
                 + [pl.BlockSpec((None, tm, LANES), lambda i: (i // tps, i % tps, 0))] * 3
                 + [_const_spec(w_o.shape)],
        out_specs=pl.BlockSpec((tm, d), row),
        out_shape=jax.ShapeDtypeStruct((t, d), F32),
        compiler_params=_cparams("parallel"),
        name="dil_comb",
    )(x, *os_, *lses, w_o)


def _dil_attn_s_kernel(q_ref, kn0_ref, kn1_ref, kn2_ref, b0_ref, b1_ref, b2_ref,
                       t0_ref, t1_ref, t2_ref, n0_ref, n1_ref, n2_ref, o_ref, *, s_new, nb):
    for e in range(nb):
        rs = pl.ds(e * s_new, s_new)
        _dil_attn_s_one(q_ref.at[rs], (kn0_ref.at[rs], kn1_ref.at[rs], kn2_ref.at[rs]),
                        (b0_ref.at[e], b1_ref.at[e], b2_ref.at[e]), (t0_ref, t1_ref, t2_ref),
                        (n0_ref, n1_ref, n2_ref), o_ref.at[rs], s_new=s_new)


def _dil_attn_s_one(q_ref, kn_refs, b_refs, t_refs, n_refs, o_ref, *, s_new):
    q = q_ref[...]
    rows_w = lax.broadcasted_iota(jnp.int32, (LANES, GW), 0)
    lanes_w = lax.broadcasted_iota(jnp.int32, (LANES, GW), 1)
    head_sel = (lanes_w // HD) == (rows_w // s_new)
    logits = []
    vals = []
    for g, (kn_ref, b_ref, t_ref, n_ref) in enumerate(zip(kn_refs, b_refs, t_refs, n_refs)):
        qg = q[:, g * GW:(g + 1) * GW]
        wq = jnp.where(head_sel, jnp.tile(qg, (LANES // s_new, 1)), 0.0).astype(BF)
        n_rows = b_ref.shape[0] // (2 * HPG)
        if len(b_ref.shape) == 2:
            slab = lambda s: b_ref[pl.ds(s, n_rows, stride=2 * HPG), :]
        else:
            slab = lambda s: b_ref[:, pl.ds(s, s_new, stride=2 * HPG), :].reshape(-1, HD)
        buf_k = jnp.concatenate([slab(s) for s in range(HPG)], axis=1)
        buf_v = jnp.concatenate([slab(HPG + s) for s in range(HPG)], axis=1)
        new = jnp.concatenate([kn_ref[...], jnp.zeros((LANES - s_new, 2 * GW), F32)], axis=0)
        logits.append(_dot_nt(buf_k.astype(BF), wq) + t_ref[...])
        vals.append(buf_v.astype(BF))
        logits.append(_dot_nt(new[:, :GW].astype(BF), wq) + n_ref[...])
        vals.append(new[:, GW:].astype(BF))
    m = logits[0].max(axis=0, keepdims=True)
    for lg in logits[1:]:
        m = jnp.maximum(m, lg.max(axis=0, keepdims=True))
    ps = [jnp.exp(lg - m) for lg in logits]
    den = ps[0].sum(axis=0, keepdims=True)
    for p in ps[1:]:
        den = den + p.sum(axis=0, keepdims=True)
    inv = 1.0 / den
    out = jnp.zeros((LANES, GW), F32)
    for p, v in zip(ps, vals):
        out = out + _dot_tn((p * inv).astype(BF), v)
    for hh in range(HPG):
        o_ref[:, hh * HD:(hh + 1) * HD] = out[hh * s_new:(hh + 1) * s_new, hh * HD:(hh + 1) * HD]


def _dil_attn_sample(q, kv_new, bufs, tabs, ntabs, *, batch, s_new, nb=2):
    t = batch * s_new
    nb = nb if batch % nb == 0 else 1
    bufs2, buf_specs, tabs = [], [], list(tabs)
    for g, b in enumerate(bufs):
        r, dil = b.shape[1], DILATIONS[g]
        if dil > s_new:
            bufs2.append(b.reshape(batch, r // dil, dil * 2 * HPG, HD))
            buf_specs.append(pl.BlockSpec((nb, r // dil, s_new * 2 * HPG, HD), lambda b: (b, 0, 0, 0)))
            tabs[g] = tabs[g].reshape(r // dil, dil, LANES)[:, :s_new].reshape(-1, LANES)
        else:
            bufs2.append(b.reshape(batch, r * 2 * HPG, HD))
            buf_specs.append(pl.BlockSpec((nb, r * 2 * HPG, HD), lambda b: (b, 0, 0)))
    return pl.pallas_call(
        functools.partial(_dil_attn_s_kernel, s_new=s_new, nb=nb),
        grid=(batch // nb,),
        in_specs=[pl.BlockSpec((nb * s_new, 3 * GW), lambda b: (b, 0))]
                 + [pl.BlockSpec((nb * s_new, 2 * GW), lambda b: (b, 0))] * 3
                 + buf_specs
                 + [_const_spec(tb.shape) for tb in tabs]
                 + [_const_spec(tb.shape) for tb in ntabs],
        out_specs=pl.BlockSpec((nb * s_new, GW), lambda b: (b, 0)),
        out_shape=jax.ShapeDtypeStruct((t, GW), F32),
        compiler_params=_cparams("parallel"),
        name="dil_attn_sample",
    )(q, *kv_new, *bufs2, *tabs, *ntabs)


def _sgu_kernel(x_ref, gm_ref, wuv_ref, gs_ref, wmix_ref, bmix_ref, wout_ref, y_ref, *rest, emit_v):
    if emit_v:
        v_ref, mixed_sc = rest
    else:
        (mixed_sc,) = rest
    x = x_ref[...]
    tm, d = x.shape
    h = _rms(x, gm_ref[...]).astype(BF)
    uv = jax.nn.gelu(_dot(h, wuv_ref[...]))
    w = uv.shape[1] // 2
    u = uv[:, :w]
    v = _rms(uv[:, w:], gs_ref[...])
    if emit_v:
        v_ref[...] = v
    vb = v.astype(BF)
    gd = w // SGU_GROUPS
    for c in range(tm // CHUNK):
        rs = slice(c * CHUNK, (c + 1) * CHUNK)
        for g in range(SGU_GROUPS):
            cs = slice(g * gd, (g + 1) * gd)
            mixed_sc[rs, cs] = _dot(wmix_ref[g], vb[rs, cs]) + bmix_ref[:, cs]
    y_ref[...] = x + _dot((u * mixed_sc[...]).astype(BF), wout_ref[...])


def _sgu(x, gm, w_uv, g_sgu, wmix, bmix, w_out, *, emit_v, tm=1024):
    t, d = x.shape
    tm = min(tm, t)
    w = w_uv.shape[1] // 2
    row = lambda i: (i, 0)
    out_specs = [pl.BlockSpec((tm, d), row)]
    out_shape = [jax.ShapeDtypeStruct((t, d), F32)]
    if emit_v:
        out_specs.append(pl.BlockSpec((tm, w), row))
        out_shape.append(jax.ShapeDtypeStruct((t, w), F32))
    consts = [gm, w_uv, g_sgu, wmix, bmix, w_out]
    return pl.pallas_call(
        functools.partial(_sgu_kernel, emit_v=emit_v),
        grid=(t // tm,),
        in_specs=[pl.BlockSpec((tm, d), row)] + [_const_spec(c.shape) for c in consts],
        out_specs=out_specs,
        out_shape=out_shape,
        scratch_shapes=[pltpu.VMEM((tm, w), F32)],
        compiler_params=_cparams("parallel"),
        name="sgu",
    )(x, *consts)


def _rot_half_cols(w):
    half = w.shape[-1] // 2
    return jnp.concatenate([-w[..., half:], w[..., :half]], axis=-1)


def _pad_last(w, n):
    return jnp.pad(w, [(0, 0)] * (w.ndim - 1) + [(0, n - w.shape[-1])])


def _rope_tables(pos):
    half = ROPE_DIM // 2
    inv = ROPE_THETA ** (-jnp.arange(half, dtype=F32) / half)
    ang = pos.astype(F32)[:, None] * inv[None, :]
    cos, sin = jnp.cos(ang), jnp.sin(ang)
    return (_pad_last(jnp.concatenate([cos, cos], axis=1), LANES),
            _pad_last(jnp.concatenate([sin, sin], axis=1), LANES))


def _t5_bucket(dist):
    max_exact = N_BUCKETS // 2
    n = jnp.maximum(dist, 1).astype(F32)
    large = max_exact + (jnp.log(n / max_exact) / math.log(MAX_DISTANCE / max_exact)
                         * (N_BUCKETS - max_exact)).astype(jnp.int32)
    return jnp.where(dist < max_exact, dist, jnp.minimum(large, N_BUCKETS - 1))


def _group_bias(rel_bias, g):
    n_keys = WINDOWS[g] // DILATIONS[g] + 1
    dist = DILATIONS[g] * jnp.arange(n_keys, dtype=jnp.int32)
    return rel_bias[_t5_bucket(dist)][:, g * HPG:(g + 1) * HPG].T


def _prompt_bias_table(bias_g):
    period = 3 * WIN_TILE
    line = jnp.concatenate([bias_g[:, ::-1], jnp.full((HPG, period - WIN_TILE - 1), NEG, F32)], axis=1)
    flat = jnp.tile(line, (1, WIN_TILE))[:, :WIN_TILE * (period - 1)]
    return flat.reshape(HPG, WIN_TILE, period - 1)[:, :, :2 * WIN_TILE]


def _sample_bias_tables(bias_g, g, s_new):
    win, dil = WINDOWS[g], DILATIONS[g]
    n_keys = win // dil + 1
    line = jnp.concatenate([bias_g[:, :, None], jnp.full((HPG, n_keys, dil - 1), NEG, F32)], axis=2)
    line = line.reshape(HPG, n_keys * dil)
    line = jnp.concatenate([line[:, :win + 1], jnp.full((HPG, s_new), NEG, F32)], axis=1)
    buf = jnp.stack([line[:, j + 1: win + j + 1][:, ::-1] for j in range(s_new)])
    buf = jnp.transpose(buf, (2, 1, 0)).reshape(win, HPG * s_new)
    lpad = jnp.concatenate([jnp.full((HPG, s_new - 1), NEG, F32), line[:, :s_new]], axis=1)
    new = jnp.stack([lpad[:, j: j + s_new][:, ::-1] for j in range(s_new)])
    new = jnp.transpose(new, (2, 1, 0)).reshape(s_new, HPG * s_new)
    pad = lambda t, r: jnp.pad(t, ((0, r - t.shape[0]), (0, LANES - t.shape[1])), constant_values=NEG)
    return pad(buf, win), pad(new, LANES)


def _prep_weights(p, s_new):
    w = {}
    w["g_mix0"] = p["norm_mix"][0][None]
    w["w_dq"] = p["w_dq"].astype(BF)
    w["g_q"] = p["g_q"][None]
    q_rank = p["w_uq"].shape[0]
    wuq = p["w_uq"].reshape(q_rank, MLA_HEADS, NOPE_DIM + ROPE_DIM)
    w["w_uq_nope"] = wuq[:, :, :NOPE_DIM].reshape(q_rank, -1).astype(BF)
    wr = wuq[:, :, NOPE_DIM:]
    w["w_uq_rope"] = _pad_last(wr, LANES).reshape(q_rank, -1).astype(BF)
    w["w_uq_rot"] = _pad_last(_rot_half_cols(wr), LANES).reshape(q_rank, -1).astype(BF)
    w["w_uk_t"] = jnp.transpose(p["w_uk"], (1, 2, 0)).astype(BF)
    w["w_dkv"] = _pad_last(p["w_dkv"], KV_RANK + LANES).astype(BF)
    w["w_dkv_rot"] = _pad_last(_rot_half_cols(p["w_dkv"][:, KV_RANK:]), LANES).astype(BF)
    w["g_kv"] = p["g_kv"][None]
    w["w_uv_h"] = jnp.transpose(p["w_uv"], (1, 0, 2)).astype(BF)
    w["w_o_mla"] = p["w_o_mla"].astype(BF)
    w["w_in_conv"] = p["w_in_conv"].astype(BF)
    w["w_out_conv"] = p["w_out_conv"].astype(BF)
    w["w_qkv"] = p["w_qkv_c"].astype(BF)
    w["w_o_c"] = p["w_o_c"].astype(BF)
    biases = [_group_bias(p["rel_bias"], g) for g in range(3)]
    w["bias_p"] = [_prompt_bias_table(b) for b in biases]
    tabs = [_sample_bias_tables(b, g, s_new) for g, b in enumerate(biases)]
    w["tab_s"] = [tb[0] for tb in tabs]
    w["ntab_s"] = [tb[1] for tb in tabs]
    w["w_uv_d"] = p["w_uv_d"].astype(BF)
    w["g_sgu"] = p["g_sgu"][None]
    w["w_out_d"] = p["w_out_d"].astype(BF)
    gd = p["w_uv_d"].shape[1] // 2 // SGU_GROUPS
    tril = jnp.tril(jnp.ones((CHUNK, CHUNK), F32))
    w["wmix_p"] = (p["w_s"] * tril).astype(BF)
    w["bmix_p"] = jnp.repeat(p["b_s"].T, gd, axis=1)
    ws_s = p["w_s"][:, :s_new, :s_new] * jnp.tril(jnp.ones((s_new, s_new), F32))
    eye = jnp.eye(CHUNK // s_new, dtype=F32)
    w["wmix_s"] = jnp.einsum("ab,gij->gaibj", eye, ws_s).reshape(SGU_GROUPS, CHUNK, CHUNK).astype(BF)
    w["bmix_s"] = jnp.repeat(jnp.tile(p["b_s"][:, :s_new].T, (CHUNK // s_new, 1)), gd, axis=1)
    w["w_ffn1"] = p["w_ffn1"].astype(BF)
    w["w_ffn2"] = p["w_ffn2"].astype(BF)
    return w


def kernel(x_prompt, x_sample, cache_mla, page_table, state_conv, state_win1, state_win2, state_win3, norm_mix, norm_ffn, norm_final, w_dq, g_q, w_uq, w_dkv, g_kv, w_uk, w_uv, w_o_mla, w_in_conv, w_conv, w_out_conv, w_qkv_c, w_o_c, rel_bias, w_uv_d, g_sgu, w_s, b_s, w_out_d, w_ffn1, w_ffn2):
    bp, sp, d = x_prompt.shape
    bs, ss, _ = x_sample.shape
    depth = norm_mix.shape[0]
    past_len = page_table.shape[1] * cache_mla.shape[1]
    params = dict(norm_mix=norm_mix, w_dq=w_dq, g_q=g_q, w_uq=w_uq, w_dkv=w_dkv, g_kv=g_kv, w_uk=w_uk,
                  w_uv=w_uv, w_o_mla=w_o_mla, w_in_conv=w_in_conv, w_out_conv=w_out_conv,
                  w_qkv_c=w_qkv_c, w_o_c=w_o_c, rel_bias=rel_bias, w_uv_d=w_uv_d, g_sgu=g_sgu,
                  w_s=w_s, b_s=b_s, w_out_d=w_out_d, w_ffn1=w_ffn1, w_ffn2=w_ffn2)
    w = _prep_weights(params, ss)
    xp = x_prompt.reshape(bp * sp, d)
    xs = x_sample.reshape(bs * ss, d)
    outs = {}
    assert depth == 4, "one layer of each mixer kind"
    gf = norm_final[None]
    gms = [norm_mix[i][None] for i in range(depth)]
    ffn = lambda x, i: _ffn(x, norm_ffn[i][None], w["w_ffn1"], w["w_ffn2"], gf, layer=i,
                            final_norm=(i == depth - 1))

    tm_p = min(1024, sp)
    cos_p, sin_p = _rope_tables(jnp.arange(sp, dtype=jnp.int32))
    ql, qp, rows_p, kc, kp = _mla_proj(xp, cos_p, sin_p, w, pos_blocks=sp // tm_p, q_dtype=BF, tm=tm_p)
    o = _mla_attn_prompt(ql, qp, kc, kp, batch=bp, seq=sp)
    xp = _mla_out(xp, o, w["w_uv_h"], w["w_o_mla"])
    outs["mla_rows_p"] = rows_p.reshape(bp, sp, -1)
    pos_s = past_len + jnp.arange(ss, dtype=jnp.int32)
    cos_s, sin_s = _rope_tables(jnp.tile(pos_s, bs))
    tm_s = min(512, bs * ss)
    ql, qp, rows_s, _, _ = _mla_proj(xs, cos_s, sin_s, w, pos_blocks=bs * ss // tm_s, q_dtype=F32, tm=tm_s)
    outs["mla_rows_s"] = rows_s.reshape(bs, ss, -1)
    o = _mla_attn_sample(ql.reshape(MLA_HEADS, bs, ss, KV_RANK), qp.reshape(MLA_HEADS, bs, ss, LANES),
                         outs["mla_rows_s"], cache_mla, page_table)
    xs = _mla_out(xs, o.reshape(MLA_HEADS, bs * ss, KV_RANK), w["w_uv_h"], w["w_o_mla"])
    xp = ffn(xp, 0)
    xs = ffn(xs, 0)
    zeros_p = jnp.zeros((bp, 2, d), F32)
    xp, outs["conv_p"] = _conv_prompt(xp, zeros_p, gms[1], w["w_in_conv"], w_conv, w["w_out_conv"],
                                      batch=bp, seq=sp)
    xs, outs["conv_s"] = _conv_sample(xs, state_conv, gms[1], w["w_in_conv"], w_conv, w["w_out_conv"],
                                      batch=bs, seq=ss)
    xp = ffn(xp, 1)
    xs = ffn(xs, 1)
    q0, q1, q2, kv0, kv1, kv2, kvt2 = _dil_proj_prompt(xp, gms[2], w["w_qkv"], batch=bp, seq=sp)
    qs_p, kvs = (q0, q1, q2), (kv0, kv1, kv2)
    res = [_dil_attn_prompt(qs_p[g], kvs[g], w["bias_p"][g], g=g, batch=bp, seq=sp) for g in range(3)]
    xp = _dil_comb(xp, [r[0] for r in res], [r[1] for r in res], w["w_o_c"], batch=bp, seq=sp)
    for g in range(2):
        n_last = min(WINDOWS[g], sp)
        last = kvs[g][:, :, sp - n_last:, :]
        outs[f"win{g + 1}_p"] = jnp.transpose(last, (0, 2, 1, 3)).reshape(bp, n_last, 2, HPG, HD)
    n_last = min(WINDOWS[2], sp)
    outs["win3_p"] = kvt2.reshape(bp, sp, 2, HPG, HD)[:, sp - n_last:]
    qs, kn0, kn1, kn2 = _dil_proj(xs, gms[2], w["w_qkv"], q_dtype=F32)
    kns = (kn0, kn1, kn2)
    o = _dil_attn_sample(qs, kns, (state_win1, state_win2, state_win3), w["tab_s"], w["ntab_s"],
                         batch=bs, s_new=ss)
    xs = _proj_res(xs, o, w["w_o_c"])
    for g in range(3):
        outs[f"win{g + 1}_s"] = kns[g].reshape(bs, ss, 2, HPG, HD)
    xp = ffn(xp, 2)
    xs = ffn(xs, 2)
    (xp,) = _sgu(xp, gms[3], w["w_uv_d"], w["g_sgu"], w["wmix_p"], w["bmix_p"], w["w_out_d"], emit_v=False)
    xs, v_s = _sgu(xs, gms[3], w["w_uv_d"], w["g_sgu"], w["wmix_s"], w["bmix_s"], w["w_out_d"], emit_v=True)
    outs["sgu_v_s"] = v_s.reshape(bs, ss, -1)
    xp = ffn(xp, 3)
    xs = ffn(xs, 3)
    return (xp.reshape(bp, sp, d), xs.reshape(bs, ss, d), outs["mla_rows_p"], outs["mla_rows_s"],
            outs["conv_p"], outs["conv_s"], outs["win1_p"], outs["win1_s"], outs["win2_p"],
            outs["win2_s"], outs["win3_p"], outs["win3_s"], outs["sgu_v_s"])
```

```python
import functools
import math

import jax
import jax.numpy as jnp
from jax import lax
from jax.experimental import pallas as pl
from jax.experimental.pallas import tpu as pltpu

EPS = 1e-6
ROPE_THETA = 10000.0
NEG = -1e30
BF = jnp.bfloat16
F32 = jnp.float32
LANES = 128
VMEM_LIMIT = 52 * 1024 * 1024

MLA_HEADS = 8
NOPE_DIM = 128
ROPE_DIM = 64
KV_RANK = 256
V_DIM = 128
WINDOWS = (128, 512, 2048)
DILATIONS = (1, 4, 16)
HPG = 4
HD = 128
N_BUCKETS = 32
MAX_DISTANCE = 2048
CHUNK = 128
SGU_GROUPS = 8
WIN_TILE = 128
PAGE_BURST = 8


def _cparams(*sem):
    return pltpu.CompilerParams(dimension_semantics=sem, vmem_limit_bytes=VMEM_LIMIT)


def _rms(x, g):
    return x * lax.rsqrt(jnp.mean(x * x, axis=-1, keepdims=True) + EPS) * g


def _dot(a, b):
    return jnp.dot(a, b, preferred_element_type=F32)


def _dot_nt(a, b):
    return lax.dot_general(a, b, (((1,), (1,)), ((), ())), preferred_element_type=F32)


def _dot_tn(a, b):
    return lax.dot_general(a, b, (((0,), (0,)), ((), ())), preferred_element_type=F32)


def _const_spec(shape):
    nd = len(shape)
    return pl.BlockSpec(shape, lambda *_: (0,) * nd)


def _ffn_kernel(x_ref, g_ref, w1_ref, w2_ref, gf_ref, o_ref, h_sc, acc_sc, *, nf, final_norm):
    f = pl.program_id(1)

    @pl.when(f == 0)
    def _():
        x = x_ref[...]
        h_sc[...] = _rms(x, g_ref[...]).astype(BF)
        acc_sc[...] = x

    a = _dot(h_sc[...], w1_ref[...])
    a = jnp.maximum(a, 0.0)
    acc_sc[...] += _dot((a * a).astype(BF), w2_ref[...])

    @pl.when(f == nf - 1)
    def _():
        y = acc_sc[...]
        if final_norm:
            y = _rms(y, gf_ref[...])
        o_ref[...] = y


def _ffn(x, g, w1, w2, gf, *, layer, final_norm, tm=1024, tf=2048):
    t, d = x.shape
    dff = w1.shape[2]
    tm = min(tm, t)
    if t == tm:
        tf = min(tf, 512)
    nf = dff // tf
    return pl.pallas_call(
        functools.partial(_ffn_kernel, nf=nf, final_norm=final_norm),
        grid=(t // tm, nf),
        in_specs=[
            pl.BlockSpec((tm, d), lambda i, f: (i, 0)),
            pl.BlockSpec((1, d), lambda i, f: (0, 0)),
            pl.BlockSpec((None, d, tf), lambda i, f: (layer, 0, f)),
            pl.BlockSpec((None, tf, d), lambda i, f: (layer, f, 0)),
            pl.BlockSpec((1, d), lambda i, f: (0, 0)),
        ],
        out_specs=pl.BlockSpec((tm, d), lambda i, f: (i, 0)),
        out_shape=jax.ShapeDtypeStruct((t, d), F32),
        scratch_shapes=[pltpu.VMEM((tm, d), BF), pltpu.VMEM((tm, d), F32)],
        compiler_params=_cparams("parallel", "arbitrary"),
        name="ffn",
    )(x, g, w1, w2, gf)


def _proj_res_kernel(x_ref, a_ref, w_ref, o_ref):
    o_ref[...] = x_ref[...] + _dot(a_ref[...].astype(BF), w_ref[...])


def _proj_res(x, a, w, *, tm=512):
    t, d = x.shape
    k = a.shape[1]
    tm = min(tm, t)
    return pl.pallas_call(
        _proj_res_kernel,
        grid=(t // tm,),
        in_specs=[pl.BlockSpec((tm, d), lambda i: (i, 0)),
                  pl.BlockSpec((tm, k), lambda i: (i, 0)),
                  _const_spec(w.shape)],
        out_specs=pl.BlockSpec((tm, d), lambda i: (i, 0)),
        out_shape=jax.ShapeDtypeStruct((t, d), F32),
        compiler_params=_cparams("parallel"),
        name="proj_res",
    )(x, a, w)


def _mla_proj_kernel(x_ref, cos_ref, sin_ref, gm_ref, wdq_ref, gq_ref, wqn_ref, wqr_ref, wqt_ref,
                     wuk_ref, wkv_ref, wkt_ref, gkv_ref,
                     ql_ref, qp_ref, rows_ref, kc_ref, kp_ref, *, scale):
    h = _rms(x_ref[...], gm_ref[...]).astype(BF)
    cq = _rms(_dot(h, wdq_ref[...]), gq_ref[...]).astype(BF)
    cos = cos_ref[...]
    sin = sin_ref[...]
    q_nope = _dot(cq, wqn_ref[...])
    q_rope = _dot(cq, wqr_ref[...])
    q_rot = _dot(cq, wqt_ref[...])
    for hh in range(MLA_HEADS):
        sl = slice(hh * LANES, (hh + 1) * LANES)
        ql = _dot(q_nope[:, sl].astype(BF), wuk_ref[hh])
        ql_ref[hh] = (ql * scale).astype(ql_ref.dtype)
        qp = q_rope[:, sl] * cos + q_rot[:, sl] * sin
        qp_ref[hh] = (qp * scale).astype(qp_ref.dtype)
    ckv = _dot(h, wkv_ref[...])
    k_rot = _dot(h, wkt_ref[...])
    c = _rms(ckv[:, :KV_RANK], gkv_ref[...])
    kpe = ckv[:, KV_RANK:] * cos + k_rot * sin
    rows_ref[:, :KV_RANK] = c
    rows_ref[:, KV_RANK:] = kpe[:, :ROPE_DIM]
    kc_ref[...] = c.astype(BF)
    kp_ref[...] = kpe.astype(BF)


def _mla_proj(x, cos, sin, w, *, pos_blocks, q_dtype, tm=512):
    t, d = x.shape
    tm = min(tm, t)
    scale = 1.0 / math.sqrt(NOPE_DIM + ROPE_DIM)
    row = lambda i: (i, 0)
    consts = [w["g_mix0"], w["w_dq"], w["g_q"], w["w_uq_nope"], w["w_uq_rope"], w["w_uq_rot"],
              w["w_uk_t"], w["w_dkv"], w["w_dkv_rot"], w["g_kv"]]
    return pl.pallas_call(
        functools.partial(_mla_proj_kernel, scale=scale),
        grid=(t // tm,),
        in_specs=[pl.BlockSpec((tm, d), row),
                  pl.BlockSpec((tm, LANES), lambda i: (i % pos_blocks, 0)),
                  pl.BlockSpec((tm, LANES), lambda i: (i % pos_blocks, 0))]
                 + [_const_spec(c.shape) for c in consts],
        out_specs=[pl.BlockSpec((MLA_HEADS, tm, KV_RANK), lambda i: (0, i, 0)),
                   pl.BlockSpec((MLA_HEADS, tm, LANES), lambda i: (0, i, 0)),
                   pl.BlockSpec((tm, KV_RANK + ROPE_DIM), row),
                   pl.BlockSpec((tm, KV_RANK), row),
                   pl.BlockSpec((tm, LANES), row)],
        out_shape=[jax.ShapeDtypeStruct((MLA_HEADS, t, KV_RANK), q_dtype),
                   jax.ShapeDtypeStruct((MLA_HEADS, t, LANES), q_dtype),
                   jax.ShapeDtypeStruct((t, KV_RANK + ROPE_DIM), F32),
                   jax.ShapeDtypeStruct((t, KV_RANK), BF),
                   jax.ShapeDtypeStruct((t, LANES), BF)],
        compiler_params=_cparams("parallel"),
        name="mla_proj",
    )(x, cos, sin, *consts)


def _lane_rep(x, n):
    return x if n == LANES else jnp.concatenate([x] * (n // LANES), axis=1)


def _softmax_update(s, pv, m_ref, l_ref, acc_ref):
    m_old = m_ref[...]
    m_new = jnp.maximum(m_old, jnp.max(s, axis=-1, keepdims=True))
    alpha = jnp.exp(m_old - m_new)
    p = jnp.exp(s - _lane_rep(m_new, s.shape[1]))
    l_ref[...] = alpha * l_ref[...] + jnp.sum(p, axis=-1, keepdims=True)
    acc_ref[...] = _lane_rep(alpha, acc_ref.shape[1]) * acc_ref[...] + pv(p.astype(BF))
    m_ref[...] = m_new


def _mla_attn_p_kernel(ql_ref, qp_ref, kc_ref, kp_ref, o_ref, m_sc, l_sc, acc_sc, *, tq, tk, hc):
    i = pl.program_id(1)
    m_sc[...] = jnp.full(m_sc.shape, NEG, F32)
    l_sc[...] = jnp.zeros(l_sc.shape, F32)
    acc_sc[...] = jnp.zeros(acc_sc.shape, F32)
    n_full = (i * tq) // tk
    rc = hc * tq

    def tile(start, width, masked):
        start = pl.multiple_of(start, width)
        kc = kc_ref[pl.ds(start, width), :]
        kp = kp_ref[pl.ds(start, width), :]
        for c in range(MLA_HEADS // hc):
            ql = ql_ref[c * hc:(c + 1) * hc].reshape(rc, KV_RANK)
            qp = qp_ref[c * hc:(c + 1) * hc].reshape(rc, LANES)
            s = _dot_nt(ql, kc) + _dot_nt(qp, kp)
            if masked:
                q_pos = i * tq + lax.broadcasted_iota(jnp.int32, s.shape, 0) % tq
                k_pos = start + lax.broadcasted_iota(jnp.int32, s.shape, 1)
                s = jnp.where(k_pos <= q_pos, s, NEG)
            rs = pl.ds(c * rc, rc)
            _softmax_update(s, lambda p: _dot(p, kc), m_sc.at[rs], l_sc.at[rs], acc_sc.at[rs])

    def body(j, carry):
        tile(j * tk, tk, False)
        return carry

    lax.fori_loop(0, n_full, body, 0)
    rem = i * tq - n_full * tk
    if tk == 2 * tq:
        @pl.when(rem == 0)
        def _():
            tile(n_full * tk, tq, True)

        @pl.when(rem != 0)
        def _():
            tile(n_full * tk, tk, True)
    else:
        tile(n_full * tk, tk, True)
    o = acc_sc[...] / _lane_rep(l_sc[...], KV_RANK)
    o_ref[...] = o.reshape(MLA_HEADS, tq, KV_RANK).astype(o_ref.dtype)


def _mla_attn_prompt(ql, qp, kc, kp, *, batch, seq, tq=256, tk=512, hc=2):
    t = batch * seq
    nq = seq // tq
    rows = MLA_HEADS * tq
    qmap = lambda b, i: (0, b * nq + i, 0)
    return pl.pallas_call(
        functools.partial(_mla_attn_p_kernel, tq=tq, tk=tk, hc=hc),
        grid=(batch, nq),
        in_specs=[pl.BlockSpec((MLA_HEADS, tq, KV_RANK), qmap),
                  pl.BlockSpec((MLA_HEADS, tq, LANES), qmap),
                  pl.BlockSpec((seq, KV_RANK), lambda b, i: (b, 0)),
                  pl.BlockSpec((seq, LANES), lambda b, i: (b, 0))],
        out_specs=pl.BlockSpec((MLA_HEADS, tq, KV_RANK), qmap),
        out_shape=jax.ShapeDtypeStruct((MLA_HEADS, t, KV_RANK), BF),
        scratch_shapes=[pltpu.VMEM((rows, LANES), F32), pltpu.VMEM((rows, LANES), F32),
                        pltpu.VMEM((rows, KV_RANK), F32)],
        compiler_params=_cparams("parallel", "arbitrary"),
        name="mla_attn_prompt",
    )(ql, qp, kc, kp)


def _mla_attn_s_kernel(pt_ref, ql_ref, qp_ref, new_ref, cache_hbm, o_ref, m_sc, l_sc, acc_sc, kc_sc, kp_sc,
                       page_buf, page_sem, *, n_pages, chunk, page_size, s_new):
    b = pl.program_id(0)
    slot = lax.rem(b, 2)

    def page_copy(seq, k, slot_):
        return pltpu.make_async_copy(cache_hbm.at[pt_ref[seq * n_pages + k]], page_buf.at[slot_, k],
                                     page_sem.at[slot_])

    def start_pages(seq, slot_):
        def body(g, carry):
            for j in range(PAGE_BURST):
                page_copy(seq, g * PAGE_BURST + j, slot_).start(priority=j % 2)
            return carry
        lax.fori_loop(0, n_pages // PAGE_BURST, body, 0)

    @pl.when(b == 0)
    def _():
        start_pages(0, 0)

    @pl.when(b + 1 < pl.num_programs(0))
    def _():
        start_pages(b + 1, 1 - slot)

    for k in range(n_pages):
        page_copy(b, k, slot).wait()

    m_sc[...] = jnp.full(m_sc.shape, NEG, F32)
    l_sc[...] = jnp.zeros(l_sc.shape, F32)
    acc_sc[...] = jnp.zeros(acc_sc.shape, F32)
    kp_sc[ROPE_DIM:, :] = jnp.zeros((LANES - ROPE_DIM, kp_sc.shape[1]), BF)

    rows = MLA_HEADS * s_new
    ql = ql_ref[...].reshape(rows, KV_RANK).astype(BF)
    qp = qp_ref[...].reshape(rows, LANES).astype(BF)
    for c in range(n_pages // chunk):
        for k in range(c * chunk, (c + 1) * chunk):
            kb = page_buf[slot, k]
            sl = slice(k * page_size, (k + 1) * page_size)
            kc_sc[:, sl] = kb[:KV_RANK].astype(BF)
            kp_sc[:ROPE_DIM, sl] = kb[KV_RANK:].astype(BF)
        csl = slice(c * chunk * page_size, (c + 1) * chunk * page_size)
        kc_t = kc_sc[:, csl]
        s = _dot(ql, kc_t) + _dot(qp, kp_sc[:, csl])
        _softmax_update(s, lambda p, kc_t=kc_t: _dot_nt(p, kc_t), m_sc, l_sc, acc_sc)

    pad_r = jnp.zeros((LANES - s_new, KV_RANK + ROPE_DIM), F32)
    nb = jnp.concatenate([new_ref[...], pad_r], axis=0)
    nc = nb[:, :KV_RANK].astype(BF)
    npe = jnp.concatenate([nb[:, KV_RANK:], jnp.zeros((LANES, LANES - ROPE_DIM), F32)], axis=1).astype(BF)
    sn = _dot_nt(ql, nc) + _dot_nt(qp, npe)
    q_pos = lax.broadcasted_iota(jnp.int32, sn.shape, 0) % s_new
    k_pos = lax.broadcasted_iota(jnp.int32, sn.shape, 1)
    sn = jnp.where(k_pos <= q_pos, sn, NEG)
    _softmax_update(sn, lambda p: _dot(p, nc), m_sc, l_sc, acc_sc)
    o = acc_sc[...] / _lane_rep(l_sc[...], KV_RANK)
    o_ref[...] = o.reshape(MLA_HEADS, s_new, KV_RANK)


def _mla_attn_sample(ql, qp, rows_new, cache, page_table, *, chunk=16):
    _, bs, s_new, _ = ql.shape
    n_pages = page_table.shape[1]
    page_size = cache.shape[1]
    row_w = cache.shape[2]
    rows = MLA_HEADS * s_new
    assert n_pages % PAGE_BURST == 0 and n_pages % min(chunk, n_pages) == 0
    pt = page_table.reshape(-1)
    qmap = lambda b, pt: (0, b, 0, 0)
    cache_t = jnp.transpose(cache, (0, 2, 1))
    grid_spec = pltpu.PrefetchScalarGridSpec(
        num_scalar_prefetch=1,
        grid=(bs,),
        in_specs=[pl.BlockSpec((MLA_HEADS, None, s_new, KV_RANK), qmap),
                  pl.BlockSpec((MLA_HEADS, None, s_new, LANES), qmap),
                  pl.BlockSpec((None, s_new, row_w), lambda b, pt: (b, 0, 0)),
                  pl.BlockSpec(memory_space=pl.ANY)],
        out_specs=pl.BlockSpec((MLA_HEADS, None, s_new, KV_RANK), qmap),
        scratch_shapes=[pltpu.VMEM((rows, LANES), F32), pltpu.VMEM((rows, LANES), F32),
                        pltpu.VMEM((rows, KV_RANK), F32),
                        pltpu.VMEM((KV_RANK, n_pages * page_size), BF),
                        pltpu.VMEM((LANES, n_pages * page_size), BF),
                        pltpu.VMEM((2, n_pages, row_w, page_size), F32),
                        pltpu.SemaphoreType.DMA((2,))],
    )
    return pl.pallas_call(
        functools.partial(_mla_attn_s_kernel, n_pages=n_pages, chunk=min(chunk, n_pages),
                          page_size=page_size, s_new=s_new),
        grid_spec=grid_spec,
        out_shape=jax.ShapeDtypeStruct((MLA_HEADS, bs, s_new, KV_RANK), F32),
        compiler_params=_cparams("arbitrary"),
        name="mla_attn_sample",
    )(pt, ql, qp, rows_new, cache_t)


def _mla_out_kernel(x_ref, o_ref, wuv_ref, wo_ref, y_ref):
    vs = [_dot(o_ref[hh].astype(BF), wuv_ref[hh]).astype(BF) for hh in range(MLA_HEADS)]
    v = jnp.concatenate(vs, axis=1)
    y_ref[...] = x_ref[...] + _dot(v, wo_ref[...])


def _mla_out(x, o, w_uv_h, w_o, *, tm=1024):
    t, d = x.shape
    tm = min(tm, t)
    return pl.pallas_call(
        _mla_out_kernel,
        grid=(t // tm,),
        in_specs=[pl.BlockSpec((tm, d), lambda i: (i, 0)),
                  pl.BlockSpec((MLA_HEADS, tm, KV_RANK), lambda i: (0, i, 0)),
                  _const_spec(w_uv_h.shape), _const_spec(w_o.shape)],
        out_specs=pl.BlockSpec((tm, d), lambda i: (i, 0)),
        out_shape=jax.ShapeDtypeStruct((t, d), F32),
        compiler_params=_cparams("parallel"),
        name="mla_out",
    )(x, o, w_uv_h, w_o)


def _conv_kernel(x_ref, st_ref, gm_ref, win_ref, wc_ref, wout_ref, y_ref, cs_ref, carry_sc,
                 *, tiles_per_seq, seq_rows):
    x = x_ref[...]
    tm, d = x.shape
    h = _rms(x, gm_ref[...]).astype(BF)
    bch = _dot(h, win_ref[...])
    gate_b = bch[:, :d]
    z = bch[:, d:2 * d] * bch[:, 2 * d:]
    z1 = pltpu.roll(z, 1, axis=0)
    z2 = pltpu.roll(z, 2, axis=0)
    row = lax.broadcasted_iota(jnp.int32, z.shape, 0)
    if tiles_per_seq is not None:
        i = pl.program_id(0)
        first = (i % tiles_per_seq) == 0
        st = st_ref[...]
        prev2 = jnp.where(first, st[0:1], carry_sc[6:7])
        prev1 = jnp.where(first, st[1:2], carry_sc[7:8])
        z1 = jnp.where(row == 0, prev1, z1)
        z2 = jnp.where(row == 0, prev2, jnp.where(row == 1, prev1, z2))
        carry_sc[...] = z[tm - 8:tm]
        cs_ref[...] = z[tm - 2:tm]
    else:
        nb = tm // seq_rows
        st = st_ref[...]
        j = lax.broadcasted_iota(jnp.int32, (nb, seq_rows, d), 1)
        z3 = z.reshape(nb, seq_rows, d)
        p1 = st[:, 1:2, :]
        p2 = st[:, 0:1, :]
        z1 = jnp.where(j == 0, p1, z1.reshape(nb, seq_rows, d)).reshape(tm, d)
        z2 = jnp.where(j == 0, p2, jnp.where(j == 1, p1, z2.reshape(nb, seq_rows, d))).reshape(tm, d)
        cs_ref[...] = z3[:, seq_rows - 2:, :]
    wc = wc_ref[...]
    y = wc[0:1] * z2 + wc[1:2] * z1 + wc[2:3] * z
    y_ref[...] = x + _dot((gate_b * y).astype(BF), wout_ref[...])


def _conv_prompt(x, state, gm, w_in, w_conv, w_out, *, batch, seq, tm=1024):
    t, d = x.shape
    tps = seq // tm
    return pl.pallas_call(
        functools.partial(_conv_kernel, tiles_per_seq=tps, seq_rows=seq),
        grid=(t // tm,),
        in_specs=[pl.BlockSpec((tm, d), lambda i: (i, 0)),
                  pl.BlockSpec((None, 2, d), lambda i: (i // tps, 0, 0)),
                  _const_spec(gm.shape), _const_spec(w_in.shape), _const_spec(w_conv.shape),
                  _const_spec(w_out.shape)],
        out_specs=[pl.BlockSpec((tm, d), lambda i: (i, 0)),
                   pl.BlockSpec((None, 2, d), lambda i: (i // tps, 0, 0))],
        out_shape=[jax.ShapeDtypeStruct((t, d), F32), jax.ShapeDtypeStruct((batch, 2, d), F32)],
        scratch_shapes=[pltpu.VMEM((8, d), F32)],
        compiler_params=_cparams("arbitrary"),
        name="conv_prompt",
    )(x, state, gm, w_in, w_conv, w_out)


def _conv_sample(x, state, gm, w_in, w_conv, w_out, *, batch, seq, tm=512):
    t, d = x.shape
    tm = min(tm, t)
    nb = tm // seq
    return pl.pallas_call(
        functools.partial(_conv_kernel, tiles_per_seq=None, seq_rows=seq),
        grid=(t // tm,),
        in_specs=[pl.BlockSpec((tm, d), lambda i: (i, 0)),
                  pl.BlockSpec((nb, 2, d), lambda i: (i, 0, 0)),
                  _const_spec(gm.shape), _const_spec(w_in.shape), _const_spec(w_conv.shape),
                  _const_spec(w_out.shape)],
        out_specs=[pl.BlockSpec((tm, d), lambda i: (i, 0)),
                   pl.BlockSpec((nb, 2, d), lambda i: (i, 0, 0))],
        out_shape=[jax.ShapeDtypeStruct((t, d), F32), jax.ShapeDtypeStruct((batch, 2, d), F32)],
        scratch_shapes=[pltpu.VMEM((8, d), F32)],
        compiler_params=_cparams("parallel"),
        name="conv_sample",
    )(x, state, gm, w_in, w_conv, w_out)


GW = HPG * HD


def _kv_group(h, w_ref, g):
    k = _dot(h, w_ref[:, (3 + g) * GW:(4 + g) * GW])
    v = _dot(h, w_ref[:, (6 + g) * GW:(7 + g) * GW])
    return jnp.concatenate([k, v], axis=1)


def _dil_proj_kernel(x_ref, gm_ref, w_ref, q_ref, kv0_ref, kv1_ref, kv2_ref, *, scale):
    h = _rms(x_ref[...], gm_ref[...]).astype(BF)
    q_ref[...] = (_dot(h, w_ref[:, :3 * GW]) * scale).astype(q_ref.dtype)
    for g, kv_ref in enumerate((kv0_ref, kv1_ref, kv2_ref)):
        kv_ref[...] = _kv_group(h, w_ref, g)


def _dil_proj(x, gm, w_qkv, *, q_dtype, tm=256):
    t, d = x.shape
    tm = min(tm, t)
    row = lambda i: (i, 0)
    return pl.pallas_call(
        functools.partial(_dil_proj_kernel, scale=1.0 / math.sqrt(HD)),
        grid=(t // tm,),
        in_specs=[pl.BlockSpec((tm, d), row), _const_spec(gm.shape), _const_spec(w_qkv.shape)],
        out_specs=[pl.BlockSpec((tm, 3 * GW), row)] + [pl.BlockSpec((tm, 2 * GW), row)] * 3,
        out_shape=[jax.ShapeDtypeStruct((t, 3 * GW), q_dtype)]
                  + [jax.ShapeDtypeStruct((t, 2 * GW), F32)] * 3,
        compiler_params=_cparams("parallel"),
        name="dil_proj",
    )(x, gm, w_qkv)


def _dil_proj_p_kernel(x_ref, gm_ref, w_ref, q0_ref, q1_ref, q2_ref, kv0_ref, kv1_ref, kv2_ref, kvt_ref,
                       *, scale):
    tm = x_ref.shape[0]
    h = _rms(x_ref[...], gm_ref[...]).astype(BF)
    q = _dot(h, w_ref[:, :3 * GW]) * scale
    for g, q_ref in enumerate((q0_ref, q1_ref, q2_ref)):
        for s in range(HPG):
            q_ref[s] = q[:, g * GW + s * HD: g * GW + (s + 1) * HD]
    for g, kv_ref in enumerate((kv0_ref, kv1_ref, kv2_ref)):
        kv = _kv_group(h, w_ref, g)
        for s in range(2 * HPG):
            kv_ref[s] = kv[:, s * HD:(s + 1) * HD]
            if g == 2:
                kvt_ref[pl.ds(s, tm, stride=2 * HPG), :] = kv[:, s * HD:(s + 1) * HD]


def _dil_proj_prompt(x, gm, w_qkv, *, batch, seq, tm=512):
    t, d = x.shape
    tps = seq // tm
    slab = lambda i: (i // tps, 0, i % tps, 0)
    return pl.pallas_call(
        functools.partial(_dil_proj_p_kernel, scale=1.0 / math.sqrt(HD)),
        grid=(t // tm,),
        in_specs=[pl.BlockSpec((tm, d), lambda i: (i, 0)), _const_spec(gm.shape), _const_spec(w_qkv.shape)],
        out_specs=[pl.BlockSpec((None, HPG, tm, HD), slab)] * 3
                  + [pl.BlockSpec((None, 2 * HPG, tm, HD), slab)] * 3
                  + [pl.BlockSpec((tm * 2 * HPG, HD), lambda i: (i, 0))],
        out_shape=[jax.ShapeDtypeStruct((batch, HPG, seq, HD), F32)] * 3
                  + [jax.ShapeDtypeStruct((batch, 2 * HPG, seq, HD), F32)] * 3
                  + [jax.ShapeDtypeStruct((t * 2 * HPG, HD), F32)],
        compiler_params=_cparams("parallel"),
        name="dil_proj_prompt",
    )(x, gm, w_qkv)


def _dil_attn_p_kernel(q_ref, kv_ref, bias_ref, o_ref, lse_ref, *, dil, n_i):
    lane_grp = lax.broadcasted_iota(jnp.int32, (WIN_TILE, LANES), 1) // (LANES // HPG)
    col = lax.broadcasted_iota(jnp.int32, (WIN_TILE, 2 * WIN_TILE), 1)
    span = WIN_TILE * dil

    def rows_at(start):
        return pl.ds(start, WIN_TILE) if dil == 1 else pl.ds(start, WIN_TILE, stride=dil)

    def unit(u, carry):
        r = u // n_i
        i = u % n_i
        start = r + i * span
        cur = rows_at(start)
        prev = rows_at(jnp.maximum(start - span, r))
        prev_mask = jnp.where((col < WIN_TILE) & (i == 0), NEG, 0.0)
        lse_tile = jnp.zeros((WIN_TILE, LANES), F32)
        for hh in range(HPG):
            q = q_ref[hh, cur, :].astype(BF)
            if n_i == 1:
                k_prev = v_prev = jnp.zeros((WIN_TILE, HD), F32)
            else:
                k_prev, v_prev = kv_ref[hh, prev, :], kv_ref[HPG + hh, prev, :]
            k = jnp.concatenate([k_prev, kv_ref[hh, cur, :]], axis=0).astype(BF)
            v = jnp.concatenate([v_prev, kv_ref[HPG + hh, cur, :]], axis=0).astype(BF)
            s = _dot_nt(q, k) + bias_ref[hh] + prev_mask
            m = jnp.max(s, axis=-1, keepdims=True)
            e = jnp.exp(s - m)
            l = jnp.sum(e, axis=-1, keepdims=True)
            p = (e / l).astype(BF)
            o_ref[hh, cur, :] = _dot(p, v)
            lse_tile = jnp.where(lane_grp == hh, m + jnp.log(l), lse_tile)
        lse_ref[cur, :] = lse_tile
        return carry

    lax.fori_loop(0, dil * n_i, unit, 0, unroll=4)


def _dil_attn_prompt(q, kv, bias, *, g, batch, seq):
    dil = DILATIONS[g]
    n_i = seq // dil // WIN_TILE
    return pl.pallas_call(
        functools.partial(_dil_attn_p_kernel, dil=dil, n_i=n_i),
        grid=(batch,),
        in_specs=[pl.BlockSpec((None, HPG, seq, HD), lambda b: (b, 0, 0, 0)),
                  pl.BlockSpec((None, 2 * HPG, seq, HD), lambda b: (b, 0, 0, 0)),
                  _const_spec(bias.shape)],
        out_specs=[pl.BlockSpec((None, HPG, seq, HD), lambda b: (b, 0, 0, 0)),
                   pl.BlockSpec((None, seq, LANES), lambda b: (b, 0, 0))],
        out_shape=[jax.ShapeDtypeStruct((batch, HPG, seq, HD), F32),
                   jax.ShapeDtypeStruct((batch, seq, LANES), F32)],
        compiler_params=_cparams("parallel"),
        name=f"dil_attn_prompt_g{g}",
    )(q, kv, bias)


def _dil_comb_kernel(x_ref, o0_ref, o1_ref, o2_ref, l0_ref, l1_ref, l2_ref, w_ref, y_ref):
    l0, l1, l2 = l0_ref[...], l1_ref[...], l2_ref[...]
    m = jnp.maximum(jnp.maximum(l0, l1), l2)
    e0, e1, e2 = jnp.exp(l0 - m), jnp.exp(l1 - m), jnp.exp(l2 - m)
    den = e0 + e1 + e2
    ws = (e0 / den, e1 / den, e2 / den)
    tm = x_ref.shape[0]
    parts = []
    for hh in range(HPG):
        acc = jnp.zeros((tm, HD), F32)
        for wg, o_ref in zip(ws, (o0_ref, o1_ref, o2_ref)):
            wcol = wg[:, hh * (LANES // HPG): hh * (LANES // HPG) + 1]
            acc = acc + wcol * o_ref[hh]
        parts.append(acc.astype(BF))
    y_ref[...] = x_ref[...] + _dot(jnp.concatenate(parts, axis=1), w_ref[...])


def _dil_comb(x, os_, lses, w_o, *, batch, seq, tm=512):
    t, d = x.shape
    tps = seq // tm
    row = lambda i: (i, 0)
    return pl.pallas_call(
        _dil_comb_kernel,
        grid=(t // tm,),
        in_specs=[pl.BlockSpec((tm, d), row)]
                 + [pl.BlockSpec((None, HPG, tm, HD), lambda i: (i // tps, 0, i % tps, 0))] * 3
                 + [pl.BlockSpec((None, tm, LANES), lambda i: (i // tps, i % tps, 0))] * 3
                 + [_const_spec(w_o.shape)],
        out_specs=pl.BlockSpec((tm, d), row),
        out_shape=jax.ShapeDtypeStruct((t, d), F32),
        compiler_params=_cparams("parallel"),
        name="dil_comb",
    )(x, *os_, *lses, w_o)


def _dil_attn_s_kernel(q_ref, kn0_ref, kn1_ref, kn2_ref, b0_ref, b1_ref, b2_ref,
                       t0_ref, t1_ref, t2_ref, n0_ref, n1_ref, n2_ref, o_ref, *, s_new, nb):
    for e in range(nb):
        rs = pl.ds(e * s_new, s_new)
        _dil_attn_s_one(q_ref.at[rs], (kn0_ref.at[rs], kn1_ref.at[rs], kn2_ref.at[rs]),
                        (b0_ref.at[e], b1_ref.at[e], b2_ref.at[e]), (t0_ref, t1_ref, t2_ref),
                        (n0_ref, n1_ref, n2_ref), o_ref.at[rs], s_new=s_new)


def _dil_attn_s_one(q_ref, kn_refs, b_refs, t_refs, n_refs, o_ref, *, s_new):
    q = q_ref[...]
    rows_w = lax.broadcasted_iota(jnp.int32, (LANES, GW), 0)
    lanes_w = lax.broadcasted_iota(jnp.int32, (LANES, GW), 1)
    head_sel = (lanes_w // HD) == (rows_w // s_new)
    logits = []
    vals = []
    for g, (kn_ref, b_ref, t_ref, n_ref) in enumerate(zip(kn_refs, b_refs, t_refs, n_refs)):
        qg = q[:, g * GW:(g + 1) * GW]
        wq = jnp.where(head_sel, jnp.tile(qg, (LANES // s_new, 1)), 0.0).astype(BF)
        n_rows = b_ref.shape[0] // (2 * HPG)
        if len(b_ref.shape) == 2:
            slab = lambda s: b_ref[pl.ds(s, n_rows, stride=2 * HPG), :]
        else:
            slab = lambda s: b_ref[:, pl.ds(s, s_new, stride=2 * HPG), :].reshape(-1, HD)
        buf_k = jnp.concatenate([slab(s) for s in range(HPG)], axis=1)
        buf_v = jnp.concatenate([slab(HPG + s) for s in range(HPG)], axis=1)
        new = jnp.concatenate([kn_ref[...], jnp.zeros((LANES - s_new, 2 * GW), F32)], axis=0)
        logits.append(_dot_nt(buf_k.astype(BF), wq) + t_ref[...])
        vals.append(buf_v.astype(BF))
        logits.append(_dot_nt(new[:, :GW].astype(BF), wq) + n_ref[...])
        vals.append(new[:, GW:].astype(BF))
    m = logits[0].max(axis=0, keepdims=True)
    for lg in logits[1:]:
        m = jnp.maximum(m, lg.max(axis=0, keepdims=True))
    ps = [jnp.exp(lg - m) for lg in logits]
    den = ps[0].sum(axis=0, keepdims=True)
    for p in ps[1:]:
        den = den + p.sum(axis=0, keepdims=True)
    inv = 1.0 / den
    out = jnp.zeros((LANES, GW), F32)
    for p, v in zip(ps, vals):
        out = out + _dot_tn((p * inv).astype(BF), v)
    for hh in range(HPG):
        o_ref[:, hh * HD:(hh + 1) * HD] = out[hh * s_new:(hh + 1) * s_new, hh * HD:(hh + 1) * HD]


def _dil_attn_sample(q, kv_new, bufs, tabs, ntabs, *, batch, s_new, nb=2):
    t = batch * s_new
    nb = nb if batch % nb == 0 else 1
    bufs2, buf_specs, tabs = [], [], list(tabs)
    for g, b in enumerate(bufs):
        r, dil = b.shape[1], DILATIONS[g]
        if dil > s_new:
            bufs2.append(b.reshape(batch, r // dil, dil * 2 * HPG, HD))
            buf_specs.append(pl.BlockSpec((nb, r // dil, s_new * 2 * HPG, HD), lambda b: (b, 0, 0, 0)))
            tabs[g] = tabs[g].reshape(r // dil, dil, LANES)[:, :s_new].reshape(-1, LANES)
        else:
            bufs2.append(b.reshape(batch, r * 2 * HPG, HD))
            buf_specs.append(pl.BlockSpec((nb, r * 2 * HPG, HD), lambda b: (b, 0, 0)))
    return pl.pallas_call(
        functools.partial(_dil_attn_s_kernel, s_new=s_new, nb=nb),
        grid=(batch // nb,),
        in_specs=[pl.BlockSpec((nb * s_new, 3 * GW), lambda b: (b, 0))]
                 + [pl.BlockSpec((nb * s_new, 2 * GW), lambda b: (b, 0))] * 3
                 + buf_specs
                 + [_const_spec(tb.shape) for tb in tabs]
                 + [_const_spec(tb.shape) for tb in ntabs],
        out_specs=pl.BlockSpec((nb * s_new, GW), lambda b: (b, 0)),
        out_shape=jax.ShapeDtypeStruct((t, GW), F32),
        compiler_params=_cparams("parallel"),
        name="dil_attn_sample",
    )(q, *kv_new, *bufs2, *tabs, *ntabs)


def _sgu_kernel(x_ref, gm_ref, wuv_ref, gs_ref, wmix_ref, bmix_ref, wout_ref, y_ref, *rest, emit_v):
    if emit_v:
        v_ref, mixed_sc = rest
    else:
        (mixed_sc,) = rest
    x = x_ref[...]
    tm, d = x.shape
    h = _rms(x, gm_ref[...]).astype(BF)
    uv = jax.nn.gelu(_dot(h, wuv_ref[...]))
    w = uv.shape[1] // 2
    u = uv[:, :w]
    v = _rms(uv[:, w:], gs_ref[...])
    if emit_v:
        v_ref[...] = v
    vb = v.astype(BF)
    gd = w // SGU_GROUPS
    for c in range(tm // CHUNK):
        rs = slice(c * CHUNK, (c + 1) * CHUNK)
        for g in range(SGU_GROUPS):
            cs = slice(g * gd, (g + 1) * gd)
            mixed_sc[rs, cs] = _dot(wmix_ref[g], vb[rs, cs]) + bmix_ref[:, cs]
    y_ref[...] = x + _dot((u * mixed_sc[...]).astype(BF), wout_ref[...])


def _sgu(x, gm, w_uv, g_sgu, wmix, bmix, w_out, *, emit_v, tm=1024):
    t, d = x.shape
    tm = min(tm, t)
    w = w_uv.shape[1] // 2
    row = lambda i: (i, 0)
    out_specs = [pl.BlockSpec((tm, d), row)]
    out_shape = [jax.ShapeDtypeStruct((t, d), F32)]
    if emit_v:
        out_specs.append(pl.BlockSpec((tm, w), row))
        out_shape.append(jax.ShapeDtypeStruct((t, w), F32))
    consts = [gm, w_uv, g_sgu, wmix, bmix, w_out]
    return pl.pallas_call(
        functools.partial(_sgu_kernel, emit_v=emit_v),
        grid=(t // tm,),
        in_specs=[pl.BlockSpec((tm, d), row)] + [_const_spec(c.shape) for c in consts],
        out_specs=out_specs,
        out_shape=out_shape,
        scratch_shapes=[pltpu.VMEM((tm, w), F32)],
        compiler_params=_cparams("parallel"),
        name="sgu",
    )(x, *consts)


def _rot_half_cols(w):
    half = w.shape[-1] // 2
    return jnp.concatenate([-w[..., half:], w[..., :half]], axis=-1)


def _pad_last(w, n):
    return jnp.pad(w, [(0, 0)] * (w.ndim - 1) + [(0, n - w.shape[-1])])


def _rope_tables(pos):
    half = ROPE_DIM // 2
    inv = ROPE_THETA ** (-jnp.arange(half, dtype=F32) / half)
    ang = pos.astype(F32)[:, None] * inv[None, :]
    cos, sin = jnp.cos(ang), jnp.sin(ang)
    return (_pad_last(jnp.concatenate([cos, cos], axis=1), LANES),
            _pad_last(jnp.concatenate([sin, sin], axis=1), LANES))


def _t5_bucket(dist):
    max_exact = N_BUCKETS // 2
    n = jnp.maximum(dist, 1).astype(F32)
    large = max_exact + (jnp.log(n / max_exact) / math.log(MAX_DISTANCE / max_exact)
                         * (N_BUCKETS - max_exact)).astype(jnp.int32)
    return jnp.where(dist < max_exact, dist, jnp.minimum(large, N_BUCKETS - 1))


def _group_bias(rel_bias, g):
    n_keys = WINDOWS[g] // DILATIONS[g] + 1
    dist = DILATIONS[g] * jnp.arange(n_keys, dtype=jnp.int32)
    return rel_bias[_t5_bucket(dist)][:, g * HPG:(g + 1) * HPG].T


def _prompt_bias_table(bias_g):
    period = 3 * WIN_TILE
    line = jnp.concatenate([bias_g[:, ::-1], jnp.full((HPG, period - WIN_TILE - 1), NEG, F32)], axis=1)
    flat = jnp.tile(line, (1, WIN_TILE))[:, :WIN_TILE * (period - 1)]
    return flat.reshape(HPG, WIN_TILE, period - 1)[:, :, :2 * WIN_TILE]


def _sample_bias_tables(bias_g, g, s_new):
    win, dil = WINDOWS[g], DILATIONS[g]
    n_keys = win // dil + 1
    line = jnp.concatenate([bias_g[:, :, None], jnp.full((HPG, n_keys, dil - 1), NEG, F32)], axis=2)
    line = line.reshape(HPG, n_keys * dil)
    line = jnp.concatenate([line[:, :win + 1], jnp.full((HPG, s_new), NEG, F32)], axis=1)
    buf = jnp.stack([line[:, j + 1: win + j + 1][:, ::-1] for j in range(s_new)])
    buf = jnp.transpose(buf, (2, 1, 0)).reshape(win, HPG * s_new)
    lpad = jnp.concatenate([jnp.full((HPG, s_new - 1), NEG, F32), line[:, :s_new]], axis=1)
    new = jnp.stack([lpad[:, j: j + s_new][:, ::-1] for j in range(s_new)])
    new = jnp.transpose(new, (2, 1, 0)).reshape(s_new, HPG * s_new)
    pad = lambda t, r: jnp.pad(t, ((0, r - t.shape[0]), (0, LANES - t.shape[1])), constant_values=NEG)
    return pad(buf, win), pad(new, LANES)


def _prep_weights(p, s_new):
    w = {}
    w["g_mix0"] = p["norm_mix"][0][None]
    w["w_dq"] = p["w_dq"].astype(BF)
    w["g_q"] = p["g_q"][None]
    q_rank = p["w_uq"].shape[0]
    wuq = p["w_uq"].reshape(q_rank, MLA_HEADS, NOPE_DIM + ROPE_DIM)
    w["w_uq_nope"] = wuq[:, :, :NOPE_DIM].reshape(q_rank, -1).astype(BF)
    wr = wuq[:, :, NOPE_DIM:]
    w["w_uq_rope"] = _pad_last(wr, LANES).reshape(q_rank, -1).astype(BF)
    w["w_uq_rot"] = _pad_last(_rot_half_cols(wr), LANES).reshape(q_rank, -1).astype(BF)
    w["w_uk_t"] = jnp.transpose(p["w_uk"], (1, 2, 0)).astype(BF)
    w["w_dkv"] = _pad_last(p["w_dkv"], KV_RANK + LANES).astype(BF)
    w["w_dkv_rot"] = _pad_last(_rot_half_cols(p["w_dkv"][:, KV_RANK:]), LANES).astype(BF)
    w["g_kv"] = p["g_kv"][None]
    w["w_uv_h"] = jnp.transpose(p["w_uv"], (1, 0, 2)).astype(BF)
    w["w_o_mla"] = p["w_o_mla"].astype(BF)
    w["w_in_conv"] = p["w_in_conv"].astype(BF)
    w["w_out_conv"] = p["w_out_conv"].astype(BF)
    w["w_qkv"] = p["w_qkv_c"].astype(BF)
    w["w_o_c"] = p["w_o_c"].astype(BF)
    biases = [_group_bias(p["rel_bias"], g) for g in range(3)]
    w["bias_p"] = [_prompt_bias_table(b) for b in biases]
    tabs = [_sample_bias_tables(b, g, s_new) for g, b in enumerate(biases)]
    w["tab_s"] = [tb[0] for tb in tabs]
    w["ntab_s"] = [tb[1] for tb in tabs]
    w["w_uv_d"] = p["w_uv_d"].astype(BF)
    w["g_sgu"] = p["g_sgu"][None]
    w["w_out_d"] = p["w_out_d"].astype(BF)
    gd = p["w_uv_d"].shape[1] // 2 // SGU_GROUPS
    tril = jnp.tril(jnp.ones((CHUNK, CHUNK), F32))
    w["wmix_p"] = (p["w_s"] * tril).astype(BF)
    w["bmix_p"] = jnp.repeat(p["b_s"].T, gd, axis=1)
    ws_s = p["w_s"][:, :s_new, :s_new] * jnp.tril(jnp.ones((s_new, s_new), F32))
    eye = jnp.eye(CHUNK // s_new, dtype=F32)
    w["wmix_s"] = jnp.einsum("ab,gij->gaibj", eye, ws_s).reshape(SGU_GROUPS, CHUNK, CHUNK).astype(BF)
    w["bmix_s"] = jnp.repeat(jnp.tile(p["b_s"][:, :s_new].T, (CHUNK // s_new, 1)), gd, axis=1)
    w["w_ffn1"] = p["w_ffn1"].astype(BF)
    w["w_ffn2"] = p["w_ffn2"].astype(BF)
    return w


def kernel(x_prompt, x_sample, cache_mla, page_table, state_conv, state_win1, state_win2, state_win3, norm_mix, norm_ffn, norm_final, w_dq, g_q, w_uq, w_dkv, g_kv, w_uk, w_uv, w_o_mla, w_in_conv, w_conv, w_out_conv, w_qkv_c, w_o_c, rel_bias, w_uv_d, g_sgu, w_s, b_s, w_out_d, w_ffn1, w_ffn2):
    bp, sp, d = x_prompt.shape
    bs, ss, _ = x_sample.shape
    depth = norm_mix.shape[0]
    past_len = page_table.shape[1] * cache_mla.shape[1]
    params = dict(norm_mix=norm_mix, w_dq=w_dq, g_q=g_q, w_uq=w_uq, w_dkv=w_dkv, g_kv=g_kv, w_uk=w_uk,
                  w_uv=w_uv, w_o_mla=w_o_mla, w_in_conv=w_in_conv, w_out_conv=w_out_conv,
                  w_qkv_c=w_qkv_c, w_o_c=w_o_c, rel_bias=rel_bias, w_uv_d=w_uv_d, g_sgu=g_sgu,
                  w_s=w_s, b_s=b_s, w_out_d=w_out_d, w_ffn1=w_ffn1, w_ffn2=w_ffn2)
    w = _prep_weights(params, ss)
    xp = x_prompt.reshape(bp * sp, d)
    xs = x_sample.reshape(bs * ss, d)
    outs = {}
    assert depth == 4, "one layer of each mixer kind"
    gf = norm_final[None]
    gms = [norm_mix[i][None] for i in range(depth)]
    ffn = lambda x, i: _ffn(x, norm_ffn[i][None], w["w_ffn1"], w["w_ffn2"], gf, layer=i,
                            final_norm=(i == depth - 1))

    tm_p = min(1024, sp)
    cos_p, sin_p = _rope_tables(jnp.arange(sp, dtype=jnp.int32))
    ql, qp, rows_p, kc, kp = _mla_proj(xp, cos_p, sin_p, w, pos_blocks=sp // tm_p, q_dtype=BF, tm=tm_p)
    o = _mla_attn_prompt(ql, qp, kc, kp, batch=bp, seq=sp)
    xp = _mla_out(xp, o, w["w_uv_h"], w["w_o_mla"])
    outs["mla_rows_p"] = rows_p.reshape(bp, sp, -1)
    pos_s = past_len + jnp.arange(ss, dtype=jnp.int32)
    cos_s, sin_s = _rope_tables(jnp.tile(pos_s, bs))
    tm_s = min(512, bs * ss)
    ql, qp, rows_s, _, _ = _mla_proj(xs, cos_s, sin_s, w, pos_blocks=bs * ss // tm_s, q_dtype=F32, tm=tm_s)
    outs["mla_rows_s"] = rows_s.reshape(bs, ss, -1)
    o = _mla_attn_sample(ql.reshape(MLA_HEADS, bs, ss, KV_RANK), qp.reshape(MLA_HEADS, bs, ss, LANES),
                         outs["mla_rows_s"], cache_mla, page_table)
    xs = _mla_out(xs, o.reshape(MLA_HEADS, bs * ss, KV_RANK), w["w_uv_h"], w["w_o_mla"])
    xp = ffn(xp, 0)
    xs = ffn(xs, 0)
    zeros_p = jnp.zeros((bp, 2, d), F32)
    xp, outs["conv_p"] = _conv_prompt(xp, zeros_p, gms[1], w["w_in_conv"], w_conv, w["w_out_conv"],
                                      batch=bp, seq=sp)
    xs, outs["conv_s"] = _conv_sample(xs, state_conv, gms[1], w["w_in_conv"], w_conv, w["w_out_conv"],
                                      batch=bs, seq=ss)
    xp = ffn(xp, 1)
    xs = ffn(xs, 1)
    q0, q1, q2, kv0, kv1, kv2, kvt2 = _dil_proj_prompt(xp, gms[2], w["w_qkv"], batch=bp, seq=sp)
    qs_p, kvs = (q0, q1, q2), (kv0, kv1, kv2)
    res = [_dil_attn_prompt(qs_p[g], kvs[g], w["bias_p"][g], g=g, batch=bp, seq=sp) for g in range(3)]
    xp = _dil_comb(xp, [r[0] for r in res], [r[1] for r in res], w["w_o_c"], batch=bp, seq=sp)
    for g in range(2):
        n_last = min(WINDOWS[g], sp)
        last = kvs[g][:, :, sp - n_last:, :]
        outs[f"win{g + 1}_p"] = jnp.transpose(last, (0, 2, 1, 3)).reshape(bp, n_last, 2, HPG, HD)
    n_last = min(WINDOWS[2], sp)
    outs["win3_p"] = kvt2.reshape(bp, sp, 2, HPG, HD)[:, sp - n_last:]
    qs, kn0, kn1, kn2 = _dil_proj(xs, gms[2], w["w_qkv"], q_dtype=F32)
    kns = (kn0, kn1, kn2)
    o = _dil_attn_sample(qs, kns, (state_win1, state_win2, state_win3), w["tab_s"], w["ntab_s"],
                         batch=bs, s_new=ss)
    xs = _proj_res(xs, o, w["w_o_c"])
    for g in range(3):
        outs[f"win{g + 1}_s"] = kns[g].reshape(bs, ss, 2, HPG, HD)
    xp = ffn(xp, 2)
    xs = ffn(xs, 2)
    (xp,) = _sgu(xp, gms[3], w["w_uv_d"], w["g_sgu"], w["wmix_p"], w["bmix_p"], w["w_out_d"], emit_v=False)
    xs, v_s = _sgu(xs, gms[3], w["w_uv_d"], w["g_sgu"], w["wmix_s"], w["bmix_s"], w["w_out_d"], emit_v=True)
    outs["sgu_v_s"] = v_s.reshape(bs, ss, -1)
    xp = ffn(xp, 3)
    xs = ffn(xs, 3)
    return (xp.reshape(bp, sp, d), xs.reshape(bs, ss, d), outs["mla_rows_p"], outs["mla_rows_s"],
            outs["conv_p"], outs["conv_s"], outs["win1_p"], outs["win1_s"], outs["win2_p"],
            outs["win2_s"], outs["win3_p"], outs["win3_s"], outs["sgu_v_s"])
```

```python
import functools
import math

import jax
import jax.numpy as jnp
from jax import lax
from jax.experimental import pallas as pl
from jax.experimental.pallas import tpu as pltpu

EPS = 1e-6
ROPE_THETA = 10000.0
NEG = -1e30
BF = jnp.bfloat16
F32 = jnp.float32
LANES = 128
VMEM_LIMIT = 52 * 1024 * 1024

MLA_HEADS = 8
NOPE_DIM = 128
ROPE_DIM = 64
KV_RANK = 256
V_DIM = 128
WINDOWS = (128, 512, 2048)
DILATIONS = (1, 4, 16)
HPG = 4
HD = 128
N_BUCKETS = 32
MAX_DISTANCE = 2048
CHUNK = 128
SGU_GROUPS = 8
WIN_TILE = 128
PAGE_BURST = 8


def _cparams(*sem):
    return pltpu.CompilerParams(dimension_semantics=sem, vmem_limit_bytes=VMEM_LIMIT)


def _rms(x, g):
    return x * lax.rsqrt(jnp.mean(x * x, axis=-1, keepdims=True) + EPS) * g


def _dot(a, b):
    return jnp.dot(a, b, preferred_element_type=F32)


def _dot_nt(a, b):
    return lax.dot_general(a, b, (((1,), (1,)), ((), ())), preferred_element_type=F32)


def _dot_tn(a, b):
    return lax.dot_general(a, b, (((0,), (0,)), ((), ())), preferred_element_type=F32)


def _const_spec(shape):
    nd = len(shape)
    return pl.BlockSpec(shape, lambda *_: (0,) * nd)


def _ffn_kernel(x_ref, g_ref, w1_ref, w2_ref, gf_ref, o_ref, h_sc, acc_sc, *, nf, final_norm):
    f = pl.program_id(1)

    @pl.when(f == 0)
    def _():
        x = x_ref[...]
        h_sc[...] = _rms(x, g_ref[...]).astype(BF)
        acc_sc[...] = x

    a = _dot(h_sc[...], w1_ref[...])
    a = jnp.maximum(a, 0.0)
    acc_sc[...] += _dot((a * a).astype(BF), w2_ref[...])

    @pl.when(f == nf - 1)
    def _():
        y = acc_sc[...]
        if final_norm:
            y = _rms(y, gf_ref[...])
        o_ref[...] = y


def _ffn(x, g, w1, w2, gf, *, layer, final_norm, tm=1024, tf=2048):
    t, d = x.shape
    dff = w1.shape[2]
    tm = min(tm, t)
    if t == tm:
        tf = min(tf, 512)
    nf = dff // tf
    return pl.pallas_call(
        functools.partial(_ffn_kernel, nf=nf, final_norm=final_norm),
        grid=(t // tm, nf),
        in_specs=[
            pl.BlockSpec((tm, d), lambda i, f: (i, 0)),
            pl.BlockSpec((1, d), lambda i, f: (0, 0)),
            pl.BlockSpec((None, d, tf), lambda i, f: (layer, 0, f)),
            pl.BlockSpec((None, tf, d), lambda i, f: (layer, f, 0)),
            pl.BlockSpec((1, d), lambda i, f: (0, 0)),
        ],
        out_specs=pl.BlockSpec((tm, d), lambda i, f: (i, 0)),
        out_shape=jax.ShapeDtypeStruct((t, d), F32),
        scratch_shapes=[pltpu.VMEM((tm, d), BF), pltpu.VMEM((tm, d), F32)],
        compiler_params=_cparams("parallel", "arbitrary"),
        name="ffn",
    )(x, g, w1, w2, gf)


def _proj_res_kernel(x_ref, a_ref, w_ref, o_ref):
    o_ref[...] = x_ref[...] + _dot(a_ref[...].astype(BF), w_ref[...])


def _proj_res(x, a, w, *, tm=512):
    t, d = x.shape
    k = a.shape[1]
    tm = min(tm, t)
    return pl.pallas_call(
        _proj_res_kernel,
        grid=(t // tm,),
        in_specs=[pl.BlockSpec((tm, d), lambda i: (i, 0)),
                  pl.BlockSpec((tm, k), lambda i: (i, 0)),
                  _const_spec(w.shape)],
        out_specs=pl.BlockSpec((tm, d), lambda i: (i, 0)),
        out_shape=jax.ShapeDtypeStruct((t, d), F32),
        compiler_params=_cparams("parallel"),
        name="proj_res",
    )(x, a, w)


def _mla_proj_kernel(x_ref, cos_ref, sin_ref, gm_ref, wdq_ref, gq_ref, wqn_ref, wqr_ref, wqt_ref,
                     wuk_ref, wkv_ref, wkt_ref, gkv_ref,
                     ql_ref, qp_ref, rows_ref, kc_ref, kp_ref, *, scale):
    h = _rms(x_ref[...], gm_ref[...]).astype(BF)
    cq = _rms(_dot(h, wdq_ref[...]), gq_ref[...]).astype(BF)
    cos = cos_ref[...]
    sin = sin_ref[...]
    q_nope = _dot(cq, wqn_ref[...])
    q_rope = _dot(cq, wqr_ref[...])
    q_rot = _dot(cq, wqt_ref[...])
    for hh in range(MLA_HEADS):
        sl = slice(hh * LANES, (hh + 1) * LANES)
        ql = _dot(q_nope[:, sl].astype(BF), wuk_ref[hh])
        ql_ref[hh] = (ql * scale).astype(ql_ref.dtype)
        qp = q_rope[:, sl] * cos + q_rot[:, sl] * sin
        qp_ref[hh] = (qp * scale).astype(qp_ref.dtype)
    ckv = _dot(h, wkv_ref[...])
    k_rot = _dot(h, wkt_ref[...])
    c = _rms(ckv[:, :KV_RANK], gkv_ref[...])
    kpe = ckv[:, KV_RANK:] * cos + k_rot * sin
    rows_ref[:, :KV_RANK] = c
    rows_ref[:, KV_RANK:] = kpe[:, :ROPE_DIM]
    kc_ref[...] = c.astype(BF)
    kp_ref[...] = kpe.astype(BF)


def _mla_proj(x, cos, sin, w, *, pos_blocks, q_dtype, tm=512):
    t, d = x.shape
    tm = min(tm, t)
    scale = 1.0 / math.sqrt(NOPE_DIM + ROPE_DIM)
    row = lambda i: (i, 0)
    consts = [w["g_mix0"], w["w_dq"], w["g_q"], w["w_uq_nope"], w["w_uq_rope"], w["w_uq_rot"],
              w["w_uk_t"], w["w_dkv"], w["w_dkv_rot"], w["g_kv"]]
    return pl.pallas_call(
        functools.partial(_mla_proj_kernel, scale=scale),
        grid=(t // tm,),
        in_specs=[pl.BlockSpec((tm, d), row),
                  pl.BlockSpec((tm, LANES), lambda i: (i % pos_blocks, 0)),
                  pl.BlockSpec((tm, LANES), lambda i: (i % pos_blocks, 0))]
                 + [_const_spec(c.shape) for c in consts],
        out_specs=[pl.BlockSpec((MLA_HEADS, tm, KV_RANK), lambda i: (0, i, 0)),
                   pl.BlockSpec((MLA_HEADS, tm, LANES), lambda i: (0, i, 0)),
                   pl.BlockSpec((tm, KV_RANK + ROPE_DIM), row),
                   pl.BlockSpec((tm, KV_RANK), row),
                   pl.BlockSpec((tm, LANES), row)],
        out_shape=[jax.ShapeDtypeStruct((MLA_HEADS, t, KV_RANK), q_dtype),
                   jax.ShapeDtypeStruct((MLA_HEADS, t, LANES), q_dtype),
                   jax.ShapeDtypeStruct((t, KV_RANK + ROPE_DIM), F32),
                   jax.ShapeDtypeStruct((t, KV_RANK), BF),
                   jax.ShapeDtypeStruct((t, LANES), BF)],
        compiler_params=_cparams("parallel"),
        name="mla_proj",
    )(x, cos, sin, *consts)


def _lane_rep(x, n):
    return x if n == LANES else jnp.concatenate([x] * (n // LANES), axis=1)


def _softmax_update(s, pv, m_ref, l_ref, acc_ref):
    m_old = m_ref[...]
    m_new = jnp.maximum(m_old, jnp.max(s, axis=-1, keepdims=True))
    alpha = jnp.exp(m_old - m_new)
    p = jnp.exp(s - _lane_rep(m_new, s.shape[1]))
    l_ref[...] = alpha * l_ref[...] + jnp.sum(p, axis=-1, keepdims=True)
    acc_ref[...] = _lane_rep(alpha, acc_ref.shape[1]) * acc_ref[...] + pv(p.astype(BF))
    m_ref[...] = m_new


def _mla_attn_p_kernel(ql_ref, qp_ref, kc_ref, kp_ref, o_ref, m_sc, l_sc, acc_sc, *, tq, tk, hc):
    i = pl.program_id(1)
    m_sc[...] = jnp.full(m_sc.shape, NEG, F32)
    l_sc[...] = jnp.zeros(l_sc.shape, F32)
    acc_sc[...] = jnp.zeros(acc_sc.shape, F32)
    n_full = (i * tq) // tk
    rc = hc * tq

    def tile(start, width, masked):
        start = pl.multiple_of(start, width)
        kc = kc_ref[pl.ds(start, width), :]
        kp = kp_ref[pl.ds(start, width), :]
        for c in range(MLA_HEADS // hc):
            ql = ql_ref[c * hc:(c + 1) * hc].reshape(rc, KV_RANK)
            qp = qp_ref[c * hc:(c + 1) * hc].reshape(rc, LANES)
            s = _dot_nt(ql, kc) + _dot_nt(qp, kp)
            if masked:
                q_pos = i * tq + lax.broadcasted_iota(jnp.int32, s.shape, 0) % tq
                k_pos = start + lax.broadcasted_iota(jnp.int32, s.shape, 1)
                s = jnp.where(k_pos <= q_pos, s, NEG)
            rs = pl.ds(c * rc, rc)
            _softmax_update(s, lambda p: _dot(p, kc), m_sc.at[rs], l_sc.at[rs], acc_sc.at[rs])

    def body(j, carry):
        tile(j * tk, tk, False)
        return carry

    lax.fori_loop(0, n_full, body, 0)
    rem = i * tq - n_full * tk
    if tk == 2 * tq:
        @pl.when(rem == 0)
        def _():
            tile(n_full * tk, tq, True)

        @pl.when(rem != 0)
        def _():
            tile(n_full * tk, tk, True)
    else:
        tile(n_full * tk, tk, True)
    o = acc_sc[...] / _lane_rep(l_sc[...], KV_RANK)
    o_ref[...] = o.reshape(MLA_HEADS, tq, KV_RANK).astype(o_ref.dtype)


def _mla_attn_prompt(ql, qp, kc, kp, *, batch, seq, tq=256, tk=512, hc=2):
    t = batch * seq
    nq = seq // tq
    rows = MLA_HEADS * tq
    qmap = lambda b, i: (0, b * nq + i, 0)
    return pl.pallas_call(
        functools.partial(_mla_attn_p_kernel, tq=tq, tk=tk, hc=hc),
        grid=(batch, nq),
        in_specs=[pl.BlockSpec((MLA_HEADS, tq, KV_RANK), qmap),
                  pl.BlockSpec((MLA_HEADS, tq, LANES), qmap),
                  pl.BlockSpec((seq, KV_RANK), lambda b, i: (b, 0)),
                  pl.BlockSpec((seq, LANES), lambda b, i: (b, 0))],
        out_specs=pl.BlockSpec((MLA_HEADS, tq, KV_RANK), qmap),
        out_shape=jax.ShapeDtypeStruct((MLA_HEADS, t, KV_RANK), BF),
        scratch_shapes=[pltpu.VMEM((rows, LANES), F32), pltpu.VMEM((rows, LANES), F32),
                        pltpu.VMEM((rows, KV_RANK), F32)],
        compiler_params=_cparams("parallel", "arbitrary"),
        name="mla_attn_prompt",
    )(ql, qp, kc, kp)


def _mla_attn_s_kernel(pt_ref, ql_ref, qp_ref, new_ref, cache_hbm, o_ref, m_sc, l_sc, acc_sc, kc_sc, kp_sc,
                       page_buf, page_sem, *, n_pages, chunk, page_size, s_new):
    b = pl.program_id(0)
    slot = lax.rem(b, 2)

    def page_copy(seq, k, slot_):
        return pltpu.make_async_copy(cache_hbm.at[pt_ref[seq * n_pages + k]], page_buf.at[slot_, k],
                                     page_sem.at[slot_])

    def start_pages(seq, slot_):
        def body(g, carry):
            for j in range(PAGE_BURST):
                page_copy(seq, g * PAGE_BURST + j, slot_).start(priority=j % 2)
            return carry
        lax.fori_loop(0, n_pages // PAGE_BURST, body, 0)

    @pl.when(b == 0)
    def _():
        start_pages(0, 0)

    @pl.when(b + 1 < pl.num_programs(0))
    def _():
        start_pages(b + 1, 1 - slot)

    for k in range(n_pages):
        page_copy(b, k, slot).wait()

    m_sc[...] = jnp.full(m_sc.shape, NEG, F32)
    l_sc[...] = jnp.zeros(l_sc.shape, F32)
    acc_sc[...] = jnp.zeros(acc_sc.shape, F32)
    kp_sc[ROPE_DIM:, :] = jnp.zeros((LANES - ROPE_DIM, kp_sc.shape[1]), BF)

    rows = MLA_HEADS * s_new
    ql = ql_ref[...].reshape(rows, KV_RANK).astype(BF)
    qp = qp_ref[...].reshape(rows, LANES).astype(BF)
    for c in range(n_pages // chunk):
        for k in range(c * chunk, (c + 1) * chunk):
            kb = page_buf[slot, k]
            sl = slice(k * page_size, (k + 1) * page_size)
            kc_sc[:, sl] = kb[:KV_RANK].astype(BF)
            kp_sc[:ROPE_DIM, sl] = kb[KV_RANK:].astype(BF)
        csl = slice(c * chunk * page_size, (c + 1) * chunk * page_size)
        kc_t = kc_sc[:, csl]
        s = _dot(ql, kc_t) + _dot(qp, kp_sc[:, csl])
        _softmax_update(s, lambda p, kc_t=kc_t: _dot_nt(p, kc_t), m_sc, l_sc, acc_sc)

    pad_r = jnp.zeros((LANES - s_new, KV_RANK + ROPE_DIM), F32)
    nb = jnp.concatenate([new_ref[...], pad_r], axis=0)
    nc = nb[:, :KV_RANK].astype(BF)
    npe = jnp.concatenate([nb[:, KV_RANK:], jnp.zeros((LANES, LANES - ROPE_DIM), F32)], axis=1).astype(BF)
    sn = _dot_nt(ql, nc) + _dot_nt(qp, npe)
    q_pos = lax.broadcasted_iota(jnp.int32, sn.shape, 0) % s_new
    k_pos = lax.broadcasted_iota(jnp.int32, sn.shape, 1)
    sn = jnp.where(k_pos <= q_pos, sn, NEG)
    _softmax_update(sn, lambda p: _dot(p, nc), m_sc, l_sc, acc_sc)
    o = acc_sc[...] / _lane_rep(l_sc[...], KV_RANK)
    o_ref[...] = o.reshape(MLA_HEADS, s_new, KV_RANK)


def _mla_attn_sample(ql, qp, rows_new, cache, page_table, *, chunk=16):
    _, bs, s_new, _ = ql.shape
    n_pages = page_table.shape[1]
    page_size = cache.shape[1]
    row_w = cache.shape[2]
    rows = MLA_HEADS * s_new
    assert n_pages % PAGE_BURST == 0 and n_pages % min(chunk, n_pages) == 0
    pt = page_table.reshape(-1)
    qmap = lambda b, pt: (0, b, 0, 0)
    cache_t = jnp.transpose(cache, (0, 2, 1))
    grid_spec = pltpu.PrefetchScalarGridSpec(
        num_scalar_prefetch=1,
        grid=(bs,),
        in_specs=[pl.BlockSpec((MLA_HEADS, None, s_new, KV_RANK), qmap),
                  pl.BlockSpec((MLA_HEADS, None, s_new, LANES), qmap),
                  pl.BlockSpec((None, s_new, row_w), lambda b, pt: (b, 0, 0)),
                  pl.BlockSpec(memory_space=pl.ANY)],
        out_specs=pl.BlockSpec((MLA_HEADS, None, s_new, KV_RANK), qmap),
        scratch_shapes=[pltpu.VMEM((rows, LANES), F32), pltpu.VMEM((rows, LANES), F32),
                        pltpu.VMEM((rows, KV_RANK), F32),
                        pltpu.VMEM((KV_RANK, n_pages * page_size), BF),
                        pltpu.VMEM((LANES, n_pages * page_size), BF),
                        pltpu.VMEM((2, n_pages, row_w, page_size), F32),
                        pltpu.SemaphoreType.DMA((2,))],
    )
    return pl.pallas_call(
        functools.partial(_mla_attn_s_kernel, n_pages=n_pages, chunk=min(chunk, n_pages),
                          page_size=page_size, s_new=s_new),
        grid_spec=grid_spec,
        out_shape=jax.ShapeDtypeStruct((MLA_HEADS, bs, s_new, KV_RANK), F32),
        compiler_params=_cparams("arbitrary"),
        name="mla_attn_sample",
    )(pt, ql, qp, rows_new, cache_t)


def _mla_out_kernel(x_ref, o_ref, wuv_ref, wo_ref, y_ref):
    vs = [_dot(o_ref[hh].astype(BF), wuv_ref[hh]).astype(BF) for hh in range(MLA_HEADS)]
    v = jnp.concatenate(vs, axis=1)
    y_ref[...] = x_ref[...] + _dot(v, wo_ref[...])


def _mla_out(x, o, w_uv_h, w_o, *, tm=1024):
    t, d = x.shape
    tm = min(tm, t)
    return pl.pallas_call(
        _mla_out_kernel,
        grid=(t // tm,),
        in_specs=[pl.BlockSpec((tm, d), lambda i: (i, 0)),
                  pl.BlockSpec((MLA_HEADS, tm, KV_RANK), lambda i: (0, i, 0)),
                  _const_spec(w_uv_h.shape), _const_spec(w_o.shape)],
        out_specs=pl.BlockSpec((tm, d), lambda i: (i, 0)),
        out_shape=jax.ShapeDtypeStruct((t, d), F32),
        compiler_params=_cparams("parallel"),
        name="mla_out",
    )(x, o, w_uv_h, w_o)


def _conv_kernel(x_ref, st_ref, gm_ref, win_ref, wc_ref, wout_ref, y_ref, cs_ref, carry_sc,
                 *, tiles_per_seq, seq_rows):
    x = x_ref[...]
    tm, d = x.shape
    h = _rms(x, gm_ref[...]).astype(BF)
    bch = _dot(h, win_ref[...])
    gate_b = bch[:, :d]
    z = bch[:, d:2 * d] * bch[:, 2 * d:]
    z1 = pltpu.roll(z, 1, axis=0)
    z2 = pltpu.roll(z, 2, axis=0)
    row = lax.broadcasted_iota(jnp.int32, z.shape, 0)
    if tiles_per_seq is not None:
        i = pl.program_id(0)
        first = (i % tiles_per_seq) == 0
        st = st_ref[...]
        prev2 = jnp.where(first, st[0:1], carry_sc[6:7])
        prev1 = jnp.where(first, st[1:2], carry_sc[7:8])
        z1 = jnp.where(row == 0, prev1, z1)
        z2 = jnp.where(row == 0, prev2, jnp.where(row == 1, prev1, z2))
        carry_sc[...] = z[tm - 8:tm]
        cs_ref[...] = z[tm - 2:tm]
    else:
        nb = tm // seq_rows
        st = st_ref[...]
        j = lax.broadcasted_iota(jnp.int32, (nb, seq_rows, d), 1)
        z3 = z.reshape(nb, seq_rows, d)
        p1 = st[:, 1:2, :]
        p2 = st[:, 0:1, :]
        z1 = jnp.where(j == 0, p1, z1.reshape(nb, seq_rows, d)).reshape(tm, d)
        z2 = jnp.where(j == 0, p2, jnp.where(j == 1, p1, z2.reshape(nb, seq_rows, d))).reshape(tm, d)
        cs_ref[...] = z3[:, seq_rows - 2:, :]
    wc = wc_ref[...]
    y = wc[0:1] * z2 + wc[1:2] * z1 + wc[2:3] * z
    y_ref[...] = x + _dot((gate_b * y).astype(BF), wout_ref[...])


def _conv_prompt(x, state, gm, w_in, w_conv, w_out, *, batch, seq, tm=1024):
    t, d = x.shape
    tps = seq // tm
    return pl.pallas_call(
        functools.partial(_conv_kernel, tiles_per_seq=tps, seq_rows=seq),
        grid=(t // tm,),
        in_specs=[pl.BlockSpec((tm, d), lambda i: (i, 0)),
                  pl.BlockSpec((None, 2, d), lambda i: (i // tps, 0, 0)),
                  _const_spec(gm.shape), _const_spec(w_in.shape), _const_spec(w_conv.shape),
                  _const_spec(w_out.shape)],
        out_specs=[pl.BlockSpec((tm, d), lambda i: (i, 0)),
                   pl.BlockSpec((None, 2, d), lambda i: (i // tps, 0, 0))],
        out_shape=[jax.ShapeDtypeStruct((t, d), F32), jax.ShapeDtypeStruct((batch, 2, d), F32)],
        scratch_shapes=[pltpu.VMEM((8, d), F32)],
        compiler_params=_cparams("arbitrary"),
        name="conv_prompt",
    )(x, state, gm, w_in, w_conv, w_out)


def _conv_sample(x, state, gm, w_in, w_conv, w_out, *, batch, seq, tm=512):
    t, d = x.shape
    tm = min(tm, t)
    nb = tm // seq
    return pl.pallas_call(
        functools.partial(_conv_kernel, tiles_per_seq=None, seq_rows=seq),
        grid=(t // tm,),
        in_specs=[pl.BlockSpec((tm, d), lambda i: (i, 0)),
                  pl.BlockSpec((nb, 2, d), lambda i: (i, 0, 0)),
                  _const_spec(gm.shape), _const_spec(w_in.shape), _const_spec(w_conv.shape),
                  _const_spec(w_out.shape)],
        out_specs=[pl.BlockSpec((tm, d), lambda i: (i, 0)),
                   pl.BlockSpec((nb, 2, d), lambda i: (i, 0, 0))],
        out_shape=[jax.ShapeDtypeStruct((t, d), F32), jax.ShapeDtypeStruct((batch, 2, d), F32)],
        scratch_shapes=[pltpu.VMEM((8, d), F32)],
        compiler_params=_cparams("parallel"),
        name="conv_sample",
    )(x, state, gm, w_in, w_conv, w_out)


GW = HPG * HD


def _kv_group(h, w_ref, g):
    k = _dot(h, w_ref[:, (3 + g) * GW:(4 + g) * GW])
    v = _dot(h, w_ref[:, (6 + g) * GW:(7 + g) * GW])
    return jnp.concatenate([k, v], axis=1)


def _dil_proj_kernel(x_ref, gm_ref, w_ref, q_ref, kv0_ref, kv1_ref, kv2_ref, *, scale):
    h = _rms(x_ref[...], gm_ref[...]).astype(BF)
    q_ref[...] = (_dot(h, w_ref[:, :3 * GW]) * scale).astype(q_ref.dtype)
    for g, kv_ref in enumerate((kv0_ref, kv1_ref, kv2_ref)):
        kv_ref[...] = _kv_group(h, w_ref, g)


def _dil_proj(x, gm, w_qkv, *, q_dtype, tm=256):
    t, d = x.shape
    tm = min(tm, t)
    row = lambda i: (i, 0)
    return pl.pallas_call(
        functools.partial(_dil_proj_kernel, scale=1.0 / math.sqrt(HD)),
        grid=(t // tm,),
        in_specs=[pl.BlockSpec((tm, d), row), _const_spec(gm.shape), _const_spec(w_qkv.shape)],
        out_specs=[pl.BlockSpec((tm, 3 * GW), row)] + [pl.BlockSpec((tm, 2 * GW), row)] * 3,
        out_shape=[jax.ShapeDtypeStruct((t, 3 * GW), q_dtype)]
                  + [jax.ShapeDtypeStruct((t, 2 * GW), F32)] * 3,
        compiler_params=_cparams("parallel"),
        name="dil_proj",
    )(x, gm, w_qkv)


def _dil_proj_p_kernel(x_ref, gm_ref, w_ref, q0_ref, q1_ref, q2_ref, kv0_ref, kv1_ref, kv2_ref, kvt_ref,
                       *, scale):
    tm = x_ref.shape[0]
    h = _rms(x_ref[...], gm_ref[...]).astype(BF)
    q = _dot(h, w_ref[:, :3 * GW]) * scale
    for g, q_ref in enumerate((q0_ref, q1_ref, q2_ref)):
        for s in range(HPG):
            q_ref[s] = q[:, g * GW + s * HD: g * GW + (s + 1) * HD]
    for g, kv_ref in enumerate((kv0_ref, kv1_ref, kv2_ref)):
        kv = _kv_group(h, w_ref, g)
        for s in range(2 * HPG):
            kv_ref[s] = kv[:, s * HD:(s + 1) * HD]
            if g == 2:
                kvt_ref[pl.ds(s, tm, stride=2 * HPG), :] = kv[:, s * HD:(s + 1) * HD]


def _dil_proj_prompt(x, gm, w_qkv, *, batch, seq, tm=512):
    t, d = x.shape
    tps = seq // tm
    slab = lambda i: (i // tps, 0, i % tps, 0)
    return pl.pallas_call(
        functools.partial(_dil_proj_p_kernel, scale=1.0 / math.sqrt(HD)),
        grid=(t // tm,),
        in_specs=[pl.BlockSpec((tm, d), lambda i: (i, 0)), _const_spec(gm.shape), _const_spec(w_qkv.shape)],
        out_specs=[pl.BlockSpec((None, HPG, tm, HD), slab)] * 3
                  + [pl.BlockSpec((None, 2 * HPG, tm, HD), slab)] * 3
                  + [pl.BlockSpec((tm * 2 * HPG, HD), lambda i: (i, 0))],
        out_shape=[jax.ShapeDtypeStruct((batch, HPG, seq, HD), F32)] * 3
                  + [jax.ShapeDtypeStruct((batch, 2 * HPG, seq, HD), F32)] * 3
                  + [jax.ShapeDtypeStruct((t * 2 * HPG, HD), F32)],
        compiler_params=_cparams("parallel"),
        name="dil_proj_prompt",
    )(x, gm, w_qkv)


def _dil_attn_p_kernel(q_ref, kv_ref, bias_ref, o_ref, lse_ref, *, dil, n_i):
    lane_grp = lax.broadcasted_iota(jnp.int32, (WIN_TILE, LANES), 1) // (LANES // HPG)
    col = lax.broadcasted_iota(jnp.int32, (WIN_TILE, 2 * WIN_TILE), 1)
    span = WIN_TILE * dil

    def rows_at(start):
        return pl.ds(start, WIN_TILE) if dil == 1 else pl.ds(start, WIN_TILE, stride=dil)

    def unit(u, carry):
        r = u // n_i
        i = u % n_i
        start = r + i * span
        cur = rows_at(start)
        prev = rows_at(jnp.maximum(start - span, r))
        prev_mask = jnp.where((col < WIN_TILE) & (i == 0), NEG, 0.0)
        lse_tile = jnp.zeros((WIN_TILE, LANES), F32)
        for hh in range(HPG):
            q = q_ref[hh, cur, :].astype(BF)
            if n_i == 1:
                k_prev = v_prev = jnp.zeros((WIN_TILE, HD), F32)
            else:
                k_prev, v_prev = kv_ref[hh, prev, :], kv_ref[HPG + hh, prev, :]
            k = jnp.concatenate([k_prev, kv_ref[hh, cur, :]], axis=0).astype(BF)
            v = jnp.concatenate([v_prev, kv_ref[HPG + hh, cur, :]], axis=0).astype(BF)
            s = _dot_nt(q, k) + bias_ref[hh] + prev_mask
            m = jnp.max(s, axis=-1, keepdims=True)
            e = jnp.exp(s - m)
            l = jnp.sum(e, axis=-1, keepdims=True)
            p = (e / l).astype(BF)
            o_ref[hh, cur, :] = _dot(p, v)
            lse_tile = jnp.where(lane_grp == hh, m + jnp.log(l), lse_tile)
        lse_ref[cur, :] = lse_tile
        return carry

    lax.fori_loop(0, dil * n_i, unit, 0, unroll=8)


def _dil_attn_prompt(q, kv, bias, *, g, batch, seq):
    dil = DILATIONS[g]
    n_i = seq // dil // WIN_TILE
    return pl.pallas_call(
        functools.partial(_dil_attn_p_kernel, dil=dil, n_i=n_i),
        grid=(batch,),
        in_specs=[pl.BlockSpec((None, HPG, seq, HD), lambda b: (b, 0, 0, 0)),
                  pl.BlockSpec((None, 2 * HPG, seq, HD), lambda b: (b, 0, 0, 0)),
                  _const_spec(bias.shape)],
        out_specs=[pl.BlockSpec((None, HPG, seq, HD), lambda b: (b, 0, 0, 0)),
                   pl.BlockSpec((None, seq, LANES), lambda b: (b, 0, 0))],
        out_shape=[jax.ShapeDtypeStruct((batch, HPG, seq, HD), F32),
                   jax.ShapeDtypeStruct((batch, seq, LANES), F32)],
        compiler_params=_cparams("parallel"),
        name=f"dil_attn_prompt_g{g}",
    )(q, kv, bias)


def _dil_comb_kernel(x_ref, o0_ref, o1_ref, o2_ref, l0_ref, l1_ref, l2_ref, w_ref, y_ref):
    l0, l1, l2 = l0_ref[...], l1_ref[...], l2_ref[...]
    m = jnp.maximum(jnp.maximum(l0, l1), l2)
    e0, e1, e2 = jnp.exp(l0 - m), jnp.exp(l1 - m), jnp.exp(l2 - m)
    den = e0 + e1 + e2
    ws = (e0 / den, e1 / den, e2 / den)
    tm = x_ref.shape[0]
    parts = []
    for hh in range(HPG):
        acc = jnp.zeros((tm, HD), F32)
        for wg, o_ref in zip(ws, (o0_ref, o1_ref, o2_ref)):
            wcol = wg[:, hh * (LANES // HPG): hh * (LANES // HPG) + 1]
            acc = acc + wcol * o_ref[hh]
        parts.append(acc.astype(BF))
    y_ref[...] = x_ref[...] + _dot(jnp.concatenate(parts, axis=1), w_ref[...])


def _dil_comb(x, os_, lses, w_o, *, batch, seq, tm=512):
    t, d = x.shape
    tps = seq // tm
    row = lambda i: (i, 0)
    return pl.pallas_call(
        _dil_comb_kernel,
        grid=(t // tm,),
        in_specs=[pl.BlockSpec((tm, d), row)]
                 + [pl.BlockSpec((None, HPG, tm, HD), lambda i: (i // tps, 0, i % tps, 0))] * 3
                 + [pl.BlockSpec((None, tm, LANES), lambda i: (i // tps, i % tps, 0))] * 3
                 + [_const_spec(w_o.shape)],
        out_specs=pl.BlockSpec((tm, d), row),
        out_shape=jax.ShapeDtypeStruct((t, d), F32),
        compiler_params=_cparams("parallel"),
        name="dil_comb",
    )(x, *os_, *lses, w_o)


def _dil_attn_s_kernel(q_ref, kn0_ref, kn1_ref, kn2_ref, b0_ref, b1_ref, b2_ref,
                       t0_ref, t1_ref, t2_ref, n0_ref, n1_ref, n2_ref, o_ref, *, s_new, nb):
    for e in range(nb):
        rs = pl.ds(e * s_new, s_new)
        _dil_attn_s_one(q_ref.at[rs], (kn0_ref.at[rs], kn1_ref.at[rs], kn2_ref.at[rs]),
                        (b0_ref.at[e], b1_ref.at[e], b2_ref.at[e]), (t0_ref, t1_ref, t2_ref),
                        (n0_ref, n1_ref, n2_ref), o_ref.at[rs], s_new=s_new)


def _dil_attn_s_one(q_ref, kn_refs, b_refs, t_refs, n_refs, o_ref, *, s_new):
    q = q_ref[...]
    rows_w = lax.broadcasted_iota(jnp.int32, (LANES, GW), 0)
    lanes_w = lax.broadcasted_iota(jnp.int32, (LANES, GW), 1)
    head_sel = (lanes_w // HD) == (rows_w // s_new)
    logits = []
    vals = []
    for g, (kn_ref, b_ref, t_ref, n_ref) in enumerate(zip(kn_refs, b_refs, t_refs, n_refs)):
        qg = q[:, g * GW:(g + 1) * GW]
        wq = jnp.where(head_sel, jnp.tile(qg, (LANES // s_new, 1)), 0.0).astype(BF)
        n_rows = b_ref.shape[0] // (2 * HPG)
        if len(b_ref.shape) == 2:
            slab = lambda s: b_ref[pl.ds(s, n_rows, stride=2 * HPG), :]
        else:
            slab = lambda s: b_ref[:, pl.ds(s, s_new, stride=2 * HPG), :].reshape(-1, HD)
        buf_k = jnp.concatenate([slab(s) for s in range(HPG)], axis=1)
        buf_v = jnp.concatenate([slab(HPG + s) for s in range(HPG)], axis=1)
        new = jnp.concatenate([kn_ref[...], jnp.zeros((LANES - s_new, 2 * GW), F32)], axis=0)
        logits.append(_dot_nt(buf_k.astype(BF), wq) + t_ref[...])
        vals.append(buf_v.astype(BF))
        logits.append(_dot_nt(new[:, :GW].astype(BF), wq) + n_ref[...])
        vals.append(new[:, GW:].astype(BF))
    m = logits[0].max(axis=0, keepdims=True)
    for lg in logits[1:]:
        m = jnp.maximum(m, lg.max(axis=0, keepdims=True))
    ps = [jnp.exp(lg - m) for lg in logits]
    den = ps[0].sum(axis=0, keepdims=True)
    for p in ps[1:]:
        den = den + p.sum(axis=0, keepdims=True)
    inv = 1.0 / den
    out = jnp.zeros((LANES, GW), F32)
    for p, v in zip(ps, vals):
        out = out + _dot_tn((p * inv).astype(BF), v)
    for hh in range(HPG):
        o_ref[:, hh * HD:(hh + 1) * HD] = out[hh * s_new:(hh + 1) * s_new, hh * HD:(hh + 1) * HD]


def _dil_attn_sample(q, kv_new, bufs, tabs, ntabs, *, batch, s_new, nb=2):
    t = batch * s_new
    nb = nb if batch % nb == 0 else 1
    bufs2, buf_specs, tabs = [], [], list(tabs)
    for g, b in enumerate(bufs):
        r, dil = b.shape[1], DILATIONS[g]
        if dil > s_new:
            bufs2.append(b.reshape(batch, r // dil, dil * 2 * HPG, HD))
            buf_specs.append(pl.BlockSpec((nb, r // dil, s_new * 2 * HPG, HD), lambda b: (b, 0, 0, 0)))
            tabs[g] = tabs[g].reshape(r // dil, dil, LANES)[:, :s_new].reshape(-1, LANES)
        else:
            bufs2.append(b.reshape(batch, r * 2 * HPG, HD))
            buf_specs.append(pl.BlockSpec((nb, r * 2 * HPG, HD), lambda b: (b, 0, 0)))
    return pl.pallas_call(
        functools.partial(_dil_attn_s_kernel, s_new=s_new, nb=nb),
        grid=(batch // nb,),
        in_specs=[pl.BlockSpec((nb * s_new, 3 * GW), lambda b: (b, 0))]
                 + [pl.BlockSpec((nb * s_new, 2 * GW), lambda b: (b, 0))] * 3
                 + buf_specs
                 + [_const_spec(tb.shape) for tb in tabs]
                 + [_const_spec(tb.shape) for tb in ntabs],
        out_specs=pl.BlockSpec((nb * s_new, GW), lambda b: (b, 0)),
        out_shape=jax.ShapeDtypeStruct((t, GW), F32),
        compiler_params=_cparams("parallel"),
        name="dil_attn_sample",
    )(q, *kv_new, *bufs2, *tabs, *ntabs)


def _sgu_kernel(x_ref, gm_ref, wuv_ref, gs_ref, wmix_ref, bmix_ref, wout_ref, y_ref, *rest, emit_v):
    if emit_v:
        v_ref, mixed_sc = rest
    else:
        (mixed_sc,) = rest
    x = x_ref[...]
    tm, d = x.shape
    h = _rms(x, gm_ref[...]).astype(BF)
    uv = jax.nn.gelu(_dot(h, wuv_ref[...]))
    w = uv.shape[1] // 2
    u = uv[:, :w]
    v = _rms(uv[:, w:], gs_ref[...])
    if emit_v:
        v_ref[...] = v
    vb = v.astype(BF)
    gd = w // SGU_GROUPS
    for c in range(tm // CHUNK):
        rs = slice(c * CHUNK, (c + 1) * CHUNK)
        for g in range(SGU_GROUPS):
            cs = slice(g * gd, (g + 1) * gd)
            mixed_sc[rs, cs] = _dot(wmix_ref[g], vb[rs, cs]) + bmix_ref[:, cs]
    y_ref[...] = x + _dot((u * mixed_sc[...]).astype(BF), wout_ref[...])


def _sgu(x, gm, w_uv, g_sgu, wmix, bmix, w_out, *, emit_v, tm=1024):
    t, d = x.shape
    tm = min(tm, t)
    w = w_uv.shape[1] // 2
    row = lambda i: (i, 0)
    out_specs = [pl.BlockSpec((tm, d), row)]
    out_shape = [jax.ShapeDtypeStruct((t, d), F32)]
    if emit_v:
        out_specs.append(pl.BlockSpec((tm, w), row))
        out_shape.append(jax.ShapeDtypeStruct((t, w), F32))
    consts = [gm, w_uv, g_sgu, wmix, bmix, w_out]
    return pl.pallas_call(
        functools.partial(_sgu_kernel, emit_v=emit_v),
        grid=(t // tm,),
        in_specs=[pl.BlockSpec((tm, d), row)] + [_const_spec(c.shape) for c in consts],
        out_specs=out_specs,
        out_shape=out_shape,
        scratch_shapes=[pltpu.VMEM((tm, w), F32)],
        compiler_params=_cparams("parallel"),
        name="sgu",
    )(x, *consts)


def _rot_half_cols(w):
    half = w.shape[-1] // 2
    return jnp.concatenate([-w[..., half:], w[..., :half]], axis=-1)


def _pad_last(w, n):
    return jnp.pad(w, [(0, 0)] * (w.ndim - 1) + [(0, n - w.shape[-1])])


def _rope_tables(pos):
    half = ROPE_DIM // 2
    inv = ROPE_THETA ** (-jnp.arange(half, dtype=F32) / half)
    ang = pos.astype(F32)[:, None] * inv[None, :]
    cos, sin = jnp.cos(ang), jnp.sin(ang)
    return (_pad_last(jnp.concatenate([cos, cos], axis=1), LANES),
            _pad_last(jnp.concatenate([sin, sin], axis=1), LANES))


def _t5_bucket(dist):
    max_exact = N_BUCKETS // 2
    n = jnp.maximum(dist, 1).astype(F32)
    large = max_exact + (jnp.log(n / max_exact) / math.log(MAX_DISTANCE / max_exact)
                         * (N_BUCKETS - max_exact)).astype(jnp.int32)
    return jnp.where(dist < max_exact, dist, jnp.minimum(large, N_BUCKETS - 1))


def _group_bias(rel_bias, g):
    n_keys = WINDOWS[g] // DILATIONS[g] + 1
    dist = DILATIONS[g] * jnp.arange(n_keys, dtype=jnp.int32)
    return rel_bias[_t5_bucket(dist)][:, g * HPG:(g + 1) * HPG].T


def _prompt_bias_table(bias_g):
    period = 3 * WIN_TILE
    line = jnp.concatenate([bias_g[:, ::-1], jnp.full((HPG, period - WIN_TILE - 1), NEG, F32)], axis=1)
    flat = jnp.tile(line, (1, WIN_TILE))[:, :WIN_TILE * (period - 1)]
    return flat.reshape(HPG, WIN_TILE, period - 1)[:, :, :2 * WIN_TILE]


def _sample_bias_tables(bias_g, g, s_new):
    win, dil = WINDOWS[g], DILATIONS[g]
    n_keys = win // dil + 1
    line = jnp.concatenate([bias_g[:, :, None], jnp.full((HPG, n_keys, dil - 1), NEG, F32)], axis=2)
    line = line.reshape(HPG, n_keys * dil)
    line = jnp.concatenate([line[:, :win + 1], jnp.full((HPG, s_new), NEG, F32)], axis=1)
    buf = jnp.stack([line[:, j + 1: win + j + 1][:, ::-1] for j in range(s_new)])
    buf = jnp.transpose(buf, (2, 1, 0)).reshape(win, HPG * s_new)
    lpad = jnp.concatenate([jnp.full((HPG, s_new - 1), NEG, F32), line[:, :s_new]], axis=1)
    new = jnp.stack([lpad[:, j: j + s_new][:, ::-1] for j in range(s_new)])
    new = jnp.transpose(new, (2, 1, 0)).reshape(s_new, HPG * s_new)
    pad = lambda t, r: jnp.pad(t, ((0, r - t.shape[0]), (0, LANES - t.shape[1])), constant_values=NEG)
    return pad(buf, win), pad(new, LANES)


def _prep_weights(p, s_new):
    w = {}
    w["g_mix0"] = p["norm_mix"][0][None]
    w["w_dq"] = p["w_dq"].astype(BF)
    w["g_q"] = p["g_q"][None]
    q_rank = p["w_uq"].shape[0]
    wuq = p["w_uq"].reshape(q_rank, MLA_HEADS, NOPE_DIM + ROPE_DIM)
    w["w_uq_nope"] = wuq[:, :, :NOPE_DIM].reshape(q_rank, -1).astype(BF)
    wr = wuq[:, :, NOPE_DIM:]
    w["w_uq_rope"] = _pad_last(wr, LANES).reshape(q_rank, -1).astype(BF)
    w["w_uq_rot"] = _pad_last(_rot_half_cols(wr), LANES).reshape(q_rank, -1).astype(BF)
    w["w_uk_t"] = jnp.transpose(p["w_uk"], (1, 2, 0)).astype(BF)
    w["w_dkv"] = _pad_last(p["w_dkv"], KV_RANK + LANES).astype(BF)
    w["w_dkv_rot"] = _pad_last(_rot_half_cols(p["w_dkv"][:, KV_RANK:]), LANES).astype(BF)
    w["g_kv"] = p["g_kv"][None]
    w["w_uv_h"] = jnp.transpose(p["w_uv"], (1, 0, 2)).astype(BF)
    w["w_o_mla"] = p["w_o_mla"].astype(BF)
    w["w_in_conv"] = p["w_in_conv"].astype(BF)
    w["w_out_conv"] = p["w_out_conv"].astype(BF)
    w["w_qkv"] = p["w_qkv_c"].astype(BF)
    w["w_o_c"] = p["w_o_c"].astype(BF)
    biases = [_group_bias(p["rel_bias"], g) for g in range(3)]
    w["bias_p"] = [_prompt_bias_table(b) for b in biases]
    tabs = [_sample_bias_tables(b, g, s_new) for g, b in enumerate(biases)]
    w["tab_s"] = [tb[0] for tb in tabs]
    w["ntab_s"] = [tb[1] for tb in tabs]
    w["w_uv_d"] = p["w_uv_d"].astype(BF)
    w["g_sgu"] = p["g_sgu"][None]
    w["w_out_d"] = p["w_out_d"].astype(BF)
    gd = p["w_uv_d"].shape[1] // 2 // SGU_GROUPS
    tril = jnp.tril(jnp.ones((CHUNK, CHUNK), F32))
    w["wmix_p"] = (p["w_s"] * tril).astype(BF)
    w["bmix_p"] = jnp.repeat(p["b_s"].T, gd, axis=1)
    ws_s = p["w_s"][:, :s_new, :s_new] * jnp.tril(jnp.ones((s_new, s_new), F32))
    eye = jnp.eye(CHUNK // s_new, dtype=F32)
    w["wmix_s"] = jnp.einsum("ab,gij->gaibj", eye, ws_s).reshape(SGU_GROUPS, CHUNK, CHUNK).astype(BF)
    w["bmix_s"] = jnp.repeat(jnp.tile(p["b_s"][:, :s_new].T, (CHUNK // s_new, 1)), gd, axis=1)
    w["w_ffn1"] = p["w_ffn1"].astype(BF)
    w["w_ffn2"] = p["w_ffn2"].astype(BF)
    return w


def kernel(x_prompt, x_sample, cache_mla, page_table, state_conv, state_win1, state_win2, state_win3, norm_mix, norm_ffn, norm_final, w_dq, g_q, w_uq, w_dkv, g_kv, w_uk, w_uv, w_o_mla, w_in_conv, w_conv, w_out_conv, w_qkv_c, w_o_c, rel_bias, w_uv_d, g_sgu, w_s, b_s, w_out_d, w_ffn1, w_ffn2):
    bp, sp, d = x_prompt.shape
    bs, ss, _ = x_sample.shape
    depth = norm_mix.shape[0]
    past_len = page_table.shape[1] * cache_mla.shape[1]
    params = dict(norm_mix=norm_mix, w_dq=w_dq, g_q=g_q, w_uq=w_uq, w_dkv=w_dkv, g_kv=g_kv, w_uk=w_uk,
                  w_uv=w_uv, w_o_mla=w_o_mla, w_in_conv=w_in_conv, w_out_conv=w_out_conv,
                  w_qkv_c=w_qkv_c, w_o_c=w_o_c, rel_bias=rel_bias, w_uv_d=w_uv_d, g_sgu=g_sgu,
                  w_s=w_s, b_s=b_s, w_out_d=w_out_d, w_ffn1=w_ffn1, w_ffn2=w_ffn2)
    w = _prep_weights(params, ss)
    xp = x_prompt.reshape(bp * sp, d)
    xs = x_sample.reshape(bs * ss, d)
    outs = {}
    assert depth == 4, "one layer of each mixer kind"
    gf = norm_final[None]
    gms = [norm_mix[i][None] for i in range(depth)]
    ffn = lambda x, i: _ffn(x, norm_ffn[i][None], w["w_ffn1"], w["w_ffn2"], gf, layer=i,
                            final_norm=(i == depth - 1))

    tm_p = min(1024, sp)
    cos_p, sin_p = _rope_tables(jnp.arange(sp, dtype=jnp.int32))
    ql, qp, rows_p, kc, kp = _mla_proj(xp, cos_p, sin_p, w, pos_blocks=sp // tm_p, q_dtype=BF, tm=tm_p)
    o = _mla_attn_prompt(ql, qp, kc, kp, batch=bp, seq=sp)
    xp = _mla_out(xp, o, w["w_uv_h"], w["w_o_mla"])
    outs["mla_rows_p"] = rows_p.reshape(bp, sp, -1)
    pos_s = past_len + jnp.arange(ss, dtype=jnp.int32)
    cos_s, sin_s = _rope_tables(jnp.tile(pos_s, bs))
    tm_s = min(512, bs * ss)
    ql, qp, rows_s, _, _ = _mla_proj(xs, cos_s, sin_s, w, pos_blocks=bs * ss // tm_s, q_dtype=F32, tm=tm_s)
    outs["mla_rows_s"] = rows_s.reshape(bs, ss, -1)
    o = _mla_attn_sample(ql.reshape(MLA_HEADS, bs, ss, KV_RANK), qp.reshape(MLA_HEADS, bs, ss, LANES),
                         outs["mla_rows_s"], cache_mla, page_table)
    xs = _mla_out(xs, o.reshape(MLA_HEADS, bs * ss, KV_RANK), w["w_uv_h"], w["w_o_mla"])
    xp = ffn(xp, 0)
    xs = ffn(xs, 0)
    zeros_p = jnp.zeros((bp, 2, d), F32)
    xp, outs["conv_p"] = _conv_prompt(xp, zeros_p, gms[1], w["w_in_conv"], w_conv, w["w_out_conv"],
                                      batch=bp, seq=sp)
    xs, outs["conv_s"] = _conv_sample(xs, state_conv, gms[1], w["w_in_conv"], w_conv, w["w_out_conv"],
                                      batch=bs, seq=ss)
    xp = ffn(xp, 1)
    xs = ffn(xs, 1)
    q0, q1, q2, kv0, kv1, kv2, kvt2 = _dil_proj_prompt(xp, gms[2], w["w_qkv"], batch=bp, seq=sp)
    qs_p, kvs = (q0, q1, q2), (kv0, kv1, kv2)
    res = [_dil_attn_prompt(qs_p[g], kvs[g], w["bias_p"][g], g=g, batch=bp, seq=sp) for g in range(3)]
    xp = _dil_comb(xp, [r[0] for r in res], [r[1] for r in res], w["w_o_c"], batch=bp, seq=sp)
    for g in range(2):
        n_last = min(WINDOWS[g], sp)
        last = kvs[g][:, :, sp - n_last:, :]
        outs[f"win{g + 1}_p"] = jnp.transpose(last, (0, 2, 1, 3)).reshape(bp, n_last, 2, HPG, HD)
    n_last = min(WINDOWS[2], sp)
    outs["win3_p"] = kvt2.reshape(bp, sp, 2, HPG, HD)[:, sp - n_last:]
    qs, kn0, kn1, kn2 = _dil_proj(xs, gms[2], w["w_qkv"], q_dtype=F32)
    kns = (kn0, kn1, kn2)
    o = _dil_attn_sample(qs, kns, (state_win1, state_win2, state_win3), w["tab_s"], w["ntab_s"],
                         batch=bs, s_new=ss)
    xs = _proj_res(xs, o, w["w_o_c"])
    for g in range(3):
        outs[f"win{g + 1}_s"] = kns[g].reshape(bs, ss, 2, HPG, HD)
    xp = ffn(xp, 2)
    xs = ffn(xs, 2)
    (xp,) = _sgu(xp, gms[3], w["w_uv_d"], w["g_sgu"], w["wmix_p"], w["bmix_p"], w["w_out_d"], emit_v=False)
    xs, v_s = _sgu(xs, gms[3], w["w_uv_d"], w["g_sgu"], w["wmix_s"], w["bmix_s"], w["w_out_d"], emit_v=True)
    outs["sgu_v_s"] = v_s.reshape(bs, ss, -1)
    xp = ffn(xp, 3)
    xs = ffn(xs, 3)
    return (xp.reshape(bp, sp, d), xs.reshape(bs, ss, d), outs["mla_rows_p"], outs["mla_rows_s"],
            outs["conv_p"], outs["conv_s"], outs["win1_p"], outs["win1_s"], outs["win2_p"],
            outs["win2_s"], outs["win3_p"], outs["win3_s"], outs["sgu_v_s"])
```
